```python
import functools
import jax, jax.numpy as jnp
from jax import lax
import numpy as np

D_MODEL = 1024
BATCH = 4
SEQ = 8192
DEPTH = 1
DEC_BATCH = 32
DEC_SEQ = 8
PAST_LEN = 16384
PAGE_SIZE = 128

N_HEADS = 8
HEAD_DIM = 64
ATTN_DIM = N_HEADS * HEAD_DIM
ROPE_THETA = 500000.0
IDX_HEADS = 8
IDX_DIM = 64
TOPK_MAX = 256
Q_BLOCK = 128
CONV_DIM = 512
CONV_WIDTH = 31
PEER_HEADS = 8
PEER_KEYS = 128
PEER_QDIM = 256
PEER_HALF = PEER_QDIM // 2
PEER_TOPK = 16
N_EXPERTS = PEER_KEYS * PEER_KEYS
PEER_BLOCK = 256
EPS = 1e-6
IN_SIZES = (CONV_DIM, CONV_DIM, ATTN_DIM, ATTN_DIM, ATTN_DIM, IDX_HEADS * IDX_DIM, IDX_DIM, IDX_HEADS, D_MODEL, D_MODEL)
IN_DIM = 2 * CONV_DIM + 3 * ATTN_DIM + IDX_HEADS * IDX_DIM + IDX_DIM + IDX_HEADS + 2 * D_MODEL

kernel_name = "hybrid_conv_dsa_peer_adaln_step"


def rms_norm(x, g):
    xf = x.astype(jnp.float32)
    xf = xf * lax.rsqrt(jnp.mean(xf * xf, axis=-1, keepdims=True) + EPS)
    return (xf * g.astype(jnp.float32)).astype(x.dtype)


def layer_norm(x, g, b):
    xf = x.astype(jnp.float32)
    mu = jnp.mean(xf, axis=-1, keepdims=True)
    var = jnp.mean(jnp.square(xf - mu), axis=-1, keepdims=True)
    y = (xf - mu) * lax.rsqrt(var + EPS) * g.astype(jnp.float32) + b.astype(jnp.float32)
    return y.astype(x.dtype)


def partial_rotary(x, pos):
    rot = x.shape[-1] // 4
    half = rot // 2
    inv_freq = ROPE_THETA ** (-(jnp.arange(half, dtype=jnp.float32) * (2.0 / rot)))
    ang = pos.astype(jnp.float32)[:, None] * inv_freq[None, :]
    cos = jnp.cos(ang)[:, None, :]
    sin = jnp.sin(ang)[:, None, :]
    x1 = x[..., :half].astype(jnp.float32)
    x2 = x[..., half:rot].astype(jnp.float32)
    rotated = jnp.concatenate([x1 * cos - x2 * sin, x1 * sin + x2 * cos], axis=-1).astype(x.dtype)
    return jnp.concatenate([rotated, x[..., rot:]], axis=-1)


def indexer_scores(iq, iw, ik):
    s = jnp.einsum('bthd,bld->bthl', iq, ik).astype(jnp.float32)
    return jnp.einsum('bthl,bth->btl', jax.nn.relu(s), iw.astype(jnp.float32))


def sparse_attend(q, k_sel, v_sel, valid):
    s = jnp.einsum('bthd,btkhd->bthk', q, k_sel).astype(jnp.float32) * (HEAD_DIM ** -0.5)
    s = jnp.where(valid[:, :, None, :], s, -jnp.inf)
    p = jax.nn.softmax(s, axis=-1).astype(v_sel.dtype)
    o = jnp.einsum('bthk,btkhd->bthd', p, v_sel)
    return o.reshape(o.shape[0], o.shape[1], ATTN_DIM)


def dsa_prompt(q, k, v, iq, ik, iw):
    B, S = q.shape[0], q.shape[1]
    n_sel = min(TOPK_MAX, S // 4)
    key_pos = jnp.arange(S)
    bidx = jnp.arange(B)[:, None, None]

    def block(i):
        start = i * Q_BLOCK
        qb = lax.dynamic_slice_in_dim(q, start, Q_BLOCK, axis=1)
        iqb = lax.dynamic_slice_in_dim(iq, start, Q_BLOCK, axis=1)
        iwb = lax.dynamic_slice_in_dim(iw, start, Q_BLOCK, axis=1)
        qpos = start + jnp.arange(Q_BLOCK)
        sc = indexer_scores(iqb, iwb, ik)
        sc = jnp.where(key_pos[None, None, :] <= qpos[None, :, None], sc, -jnp.inf)
        _, sel = lax.top_k(sc, n_sel)
        valid = sel <= qpos[None, :, None]
        return sparse_attend(qb, k[bidx, sel], v[bidx, sel], valid)

    out = lax.map(block, jnp.arange(S // Q_BLOCK))
    return jnp.swapaxes(out, 0, 1).reshape(B, S, ATTN_DIM)


def dsa_sample(q, k, v, iq, ik, iw, layer, cache_k, cache_v, cache_idx_k, page_table):
    DB, T = q.shape[0], q.shape[1]
    past = page_table.shape[1] * PAGE_SIZE
    L = past + T
    n_sel = min(TOPK_MAX, L // 4)
    ik_past = cache_idx_k[layer, page_table].reshape(DB, past, IDX_DIM)
    ik_all = jnp.concatenate([ik_past, ik], axis=1)
    qpos = past + jnp.arange(T)
    sc = indexer_scores(iq, iw, ik_all)
    sc = jnp.where(jnp.arange(L)[None, None, :] <= qpos[None, :, None], sc, -jnp.inf)
    _, sel = lax.top_k(sc, n_sel)
    valid = sel <= qpos[None, :, None]
    bidx = jnp.arange(DB)[:, None, None]
    in_past = (sel < past)[..., None, None]
    ps = jnp.minimum(sel, past - 1)
    phys = page_table[bidx, ps // PAGE_SIZE]
    off = ps % PAGE_SIZE
    cur = jnp.clip(sel - past, 0, T - 1)
    k_sel = jnp.where(in_past, cache_k[layer, phys, off], k[bidx, cur])
    v_sel = jnp.where(in_past, cache_v[layer, phys, off], v[bidx, cur])
    return sparse_attend(q, k_sel, v_sel, valid)


def peer_ffn(h, w_peer_q, peer_k1, peer_k2, peer_u, peer_v):
    shape = h.shape
    flat = h.reshape(-1, shape[-1])
    n = flat.shape[0]
    n_blk = -(-n // PEER_BLOCK)
    flat = jnp.pad(flat, ((0, n_blk * PEER_BLOCK - n), (0, 0))).reshape(n_blk, PEER_BLOCK, shape[-1])

    def block(xb):
        qh = (xb @ w_peer_q).reshape(PEER_BLOCK, PEER_HEADS, PEER_QDIM)
        s1 = jnp.einsum('nhd,kd->nhk', qh[..., :PEER_HALF], peer_k1).astype(jnp.float32)
        s2 = jnp.einsum('nhd,kd->nhk', qh[..., PEER_HALF:], peer_k2).astype(jnp.float32)
        v1, i1 = lax.top_k(s1, PEER_TOPK)
        v2, i2 = lax.top_k(s2, PEER_TOPK)
        cand = (v1[..., :, None] + v2[..., None, :]).reshape(PEER_BLOCK, PEER_HEADS, PEER_TOPK * PEER_TOPK)
        cand_idx = (i1[..., :, None] * PEER_KEYS + i2[..., None, :]).reshape(PEER_BLOCK, PEER_HEADS, PEER_TOPK * PEER_TOPK)
        top, ci = lax.top_k(cand, PEER_TOPK)
        expert = jnp.take_along_axis(cand_idx, ci, axis=-1)
        g = jax.nn.softmax(top, axis=-1)
        z = jax.nn.gelu(jnp.einsum('nhkd,nd->nhk', peer_u[expert], xb), approximate=False)
        coef = (g * z.astype(jnp.float32)).astype(xb.dtype)
        return jnp.einsum('nhk,nhkd->nd', coef, peer_v[expert])

    out = lax.map(block, flat)
    return out.reshape(n_blk * PEER_BLOCK, shape[-1])[:n].reshape(shape)


def decoder_layer(x, c, pos, conv_buf, attend, w_ada, b_ada, g_mix, w_in, w_dw, b_dw, ln_g, ln_b,
                  w_conv_out, w_attn_out, w_o, g_ffn, w_peer_q, peer_k1, peer_k2, peer_u, peer_v):
    B, T = x.shape[0], x.shape[1]
    shift1, scale1, gate1, shift2, scale2, gate2 = jnp.split(jax.nn.silu(c) @ w_ada + b_ada, 6, axis=-1)
    h = rms_norm(x, g_mix) * (1 + scale1[:, None, :]) + shift1[:, None, :]
    split_at = np.cumsum(IN_SIZES)[:-1].tolist()
    glu_a, glu_b, q, k, v, iq, ik, iw, gate_a, gate_b = jnp.split(h @ w_in, split_at, axis=-1)
    q = partial_rotary(q.reshape(B, T, N_HEADS, HEAD_DIM), pos)
    k = partial_rotary(k.reshape(B, T, N_HEADS, HEAD_DIM), pos)
    v = v.reshape(B, T, N_HEADS, HEAD_DIM)
    iq = partial_rotary(iq.reshape(B, T, IDX_HEADS, IDX_DIM), pos)
    ik = partial_rotary(ik[:, :, None, :], pos)[:, :, 0, :]
    iw = iw * ((IDX_HEADS * IDX_DIM) ** -0.5)
    u = glu_a * jax.nn.sigmoid(glu_b)
    u_ext = jnp.concatenate([conv_buf, u], axis=1)
    dw = lax.conv_general_dilated(u_ext, w_dw[:, None, :], (1,), 'VALID',
                                  dimension_numbers=('NWC', 'WIO', 'NWC'),
                                  feature_group_count=CONV_DIM) + b_dw
    branch_a = jax.nn.silu(layer_norm(dw, ln_g, ln_b)) @ w_conv_out
    branch_b = attend(q, k, v, iq, ik, iw) @ w_attn_out
    mixed = jax.nn.sigmoid(gate_a) * branch_a + jax.nn.sigmoid(gate_b) * branch_b
    x = x + gate1[:, None, :] * (mixed @ w_o)
    h2 = rms_norm(x, g_ffn) * (1 + scale2[:, None, :]) + shift2[:, None, :]
    x = x + gate2[:, None, :] * peer_ffn(h2, w_peer_q, peer_k1, peer_k2, peer_u, peer_v)
    return x, k, v, ik, u_ext[:, -(CONV_WIDTH - 1):, :]


def setup_inputs(seed: int = 0) -> dict:
    key = jax.random.key(seed)
    ks = jax.random.split(key, 32)
    n_pages = PAST_LEN // PAGE_SIZE
    n_phys = (DEC_BATCH * n_pages * 5) // 4
    f32 = jnp.float32

    def nrm(i, shape, scale):
        return jax.random.normal(ks[i], shape, f32) * scale

    page_table = jax.random.permutation(ks[8], n_phys)[:DEC_BATCH * n_pages].reshape(DEC_BATCH, n_pages).astype(jnp.int32)
    return {
        "x_prompt": nrm(0, (BATCH, SEQ, D_MODEL), 1.0),
        "x_sample": nrm(1, (DEC_BATCH, DEC_SEQ, D_MODEL), 1.0),
        "c_prompt": nrm(2, (BATCH, D_MODEL), 1.0),
        "c_sample": nrm(3, (DEC_BATCH, D_MODEL), 1.0),
        "cache_k": nrm(4, (DEPTH, n_phys, PAGE_SIZE, N_HEADS, HEAD_DIM), 1.0),
        "cache_v": nrm(5, (DEPTH, n_phys, PAGE_SIZE, N_HEADS, HEAD_DIM), 1.0),
        "cache_idx_k": nrm(6, (DEPTH, n_phys, PAGE_SIZE, IDX_DIM), 1.0),
        "state_conv": nrm(7, (DEPTH, DEC_BATCH, CONV_WIDTH - 1, CONV_DIM), 0.5),
        "page_table": page_table,
        "w_ada": nrm(9, (DEPTH, D_MODEL, 6 * D_MODEL), 0.5 * D_MODEL ** -0.5),
        "b_ada": nrm(10, (DEPTH, 6 * D_MODEL), 0.02),
        "g_norm_mix": 1.0 + nrm(11, (DEPTH, D_MODEL), 0.02),
        "w_in": nrm(12, (DEPTH, D_MODEL, IN_DIM), D_MODEL ** -0.5),
        "w_dw": nrm(13, (DEPTH, CONV_WIDTH, CONV_DIM), CONV_WIDTH ** -0.5),
        "b_dw": nrm(14, (DEPTH, CONV_DIM), 0.02),
        "ln_conv_g": 1.0 + nrm(15, (DEPTH, CONV_DIM), 0.02),
        "ln_conv_b": nrm(16, (DEPTH, CONV_DIM), 0.02),
        "w_conv_out": nrm(17, (DEPTH, CONV_DIM, D_MODEL), CONV_DIM ** -0.5),
        "w_attn_out": nrm(18, (DEPTH, ATTN_DIM, D_MODEL), ATTN_DIM ** -0.5),
        "w_o": nrm(19, (DEPTH, D_MODEL, D_MODEL), D_MODEL ** -0.5),
        "g_norm_ffn": 1.0 + nrm(20, (DEPTH, D_MODEL), 0.02),
        "w_peer_q": nrm(21, (DEPTH, D_MODEL, PEER_HEADS * PEER_QDIM), D_MODEL ** -0.5),
        "peer_k1": nrm(22, (DEPTH, PEER_KEYS, PEER_HALF), PEER_HALF ** -0.5),
        "peer_k2": nrm(23, (DEPTH, PEER_KEYS, PEER_HALF), PEER_HALF ** -0.5),
        "peer_u": nrm(24, (DEPTH, N_EXPERTS, D_MODEL), D_MODEL ** -0.5),
        "peer_v": nrm(25, (DEPTH, N_EXPERTS, D_MODEL), PEER_HEADS ** -0.5),
        "g_norm_final": 1.0 + nrm(26, (D_MODEL,), 0.02),
    }


def reference(x_prompt, x_sample, c_prompt, c_sample, cache_k, cache_v, cache_idx_k, state_conv, page_table,
              w_ada, b_ada, g_norm_mix, w_in, w_dw, b_dw, ln_conv_g, ln_conv_b, w_conv_out, w_attn_out, w_o,
              g_norm_ffn, w_peer_q, peer_k1, peer_k2, peer_u, peer_v, g_norm_final):
    B, S = x_prompt.shape[0], x_prompt.shape[1]
    DB, T = x_sample.shape[0], x_sample.shape[1]
    past = page_table.shape[1] * PAGE_SIZE
    pos_prompt = jnp.arange(S)
    pos_sample = past + jnp.arange(T)
    xp, xs = x_prompt, x_sample
    kp, vp, ikp, cvp, ksl, vsl, iks, cvs = [], [], [], [], [], [], [], []
    for l in range(DEPTH):
        lw = (w_ada[l], b_ada[l], g_norm_mix[l], w_in[l], w_dw[l], b_dw[l], ln_conv_g[l], ln_conv_b[l],
              w_conv_out[l], w_attn_out[l], w_o[l], g_norm_ffn[l], w_peer_q[l], peer_k1[l], peer_k2[l],
              peer_u[l], peer_v[l])
        conv0 = jnp.zeros((B, CONV_WIDTH - 1, CONV_DIM), xp.dtype)
        xp, k_p, v_p, ik_p, cv_p = decoder_layer(xp, c_prompt, pos_prompt, conv0, dsa_prompt, *lw)
        attend_s = functools.partial(dsa_sample, layer=l, cache_k=cache_k, cache_v=cache_v,
                                     cache_idx_k=cache_idx_k, page_table=page_table)
        xs, k_s, v_s, ik_s, cv_s = decoder_layer(xs, c_sample, pos_sample, state_conv[l], attend_s, *lw)
        kp.append(k_p); vp.append(v_p); ikp.append(ik_p); cvp.append(cv_p)
        ksl.append(k_s); vsl.append(v_s); iks.append(ik_s); cvs.append(cv_s)
    y_prompt = rms_norm(xp, g_norm_final)
    y_sample = rms_norm(xs, g_norm_final)
    n_pp = S // PAGE_SIZE
    new_k_prompt = jnp.stack(kp).reshape(DEPTH, B, n_pp, PAGE_SIZE, N_HEADS, HEAD_DIM)
    new_v_prompt = jnp.stack(vp).reshape(DEPTH, B, n_pp, PAGE_SIZE, N_HEADS, HEAD_DIM)
    new_idx_k_prompt = jnp.stack(ikp).reshape(DEPTH, B, n_pp, PAGE_SIZE, IDX_DIM)
    new_conv_prompt = jnp.stack(cvp)
    new_k_sample = jnp.stack(ksl)
    new_v_sample = jnp.stack(vsl)
    new_idx_k_sample = jnp.stack(iks)
    new_conv_sample = jnp.stack(cvs)
    return (y_prompt, y_sample, new_k_prompt, new_v_prompt, new_idx_k_prompt, new_conv_prompt,
            new_k_sample, new_v_sample, new_idx_k_sample, new_conv_sample)
```

```python
import functools
import math

import jax
import jax.numpy as jnp
from jax import lax
from jax.experimental import pallas as pl
from jax.experimental.pallas import tpu as pltpu

F32 = jnp.float32
BF16 = jnp.bfloat16
I32 = jnp.int32

N_HEADS = 8
HEAD_DIM = 64
ATTN_DIM = N_HEADS * HEAD_DIM
ROPE_THETA = 500000.0
IDX_HEADS = 8
IDX_DIM = 64
TOPK_MAX = 256
CONV_DIM = 512
CONV_WIDTH = 31
HALO = 32
PEER_HEADS = 8
PEER_KEYS = 128
PEER_HALF = 128
PEER_QDIM = 256
PEER_TOPK = 16
PAGE = 128
EPS = 1e-6
LANES = 128
INT_MIN = -(2 ** 31)
INT_MAX = 2 ** 31 - 1
NEG = -1e30
NO_RANK = 999.0
VMEM_LIMIT = 56 * 1024 * 1024

C_GLU, C_Q, C_K, C_V, C_IQ, C_IKW, C_GA, C_GB, C_END = 0, 1024, 1536, 2048, 2560, 3072, 3200, 4224, 5248


def _dg(a, b, ca, cb):
    return lax.dot_general(a, b, (((ca,), (cb,)), ((), ())), preferred_element_type=F32)


def _split(x):
    hi = x.astype(BF16)
    lo = (x - hi.astype(F32)).astype(BF16)
    return hi, lo


def _dot3(ah, al, bh, bl, ca, cb):
    return _dg(ah, bh, ca, cb) + _dg(al, bh, ca, cb) + _dg(ah, bl, ca, cb)


def _sigmoid(x):
    return 1.0 / (1.0 + jnp.exp(-x))


def _const_spec(shape):
    n = len(shape)
    return pl.BlockSpec(shape, lambda *_: (0,) * n, pipeline_mode=pl.Buffered(1))


def _params(sem):
    return pltpu.CompilerParams(dimension_semantics=sem, vmem_limit_bytes=VMEM_LIMIT)


def _sort_key(x):
    bits = lax.bitcast_convert_type(x, I32)
    key = jnp.where(bits < 0, bits ^ jnp.int32(INT_MAX), bits)
    return jnp.where(key == -1, 0, key)


def _ada_kernel(c_ref, w_ref, b_ref, o_ref):
    c = c_ref[...]
    s = c * _sigmoid(c)
    sh, sl = _split(s)
    wh, wl = _split(w_ref[...])
    o_ref[...] = _dot3(sh, sl, wh, wl, 1, 0) + b_ref[...]


def _ada(c, w_ada, b_ada):
    rows, d = c.shape
    n = w_ada.shape[1]
    tn = 1536
    return pl.pallas_call(
        _ada_kernel,
        grid=(n // tn,),
        in_specs=[pl.BlockSpec((rows, d), lambda j: (0, 0)),
                  pl.BlockSpec((d, tn), lambda j: (0, j)),
                  pl.BlockSpec((1, tn), lambda j: (0, j))],
        out_specs=pl.BlockSpec((rows, tn), lambda j: (0, j)),
        out_shape=jax.ShapeDtypeStruct((rows, n), F32),
        compiler_params=_params(("parallel",)),
        name="adaln",
    )(c, w_ada, b_ada.reshape(1, n))


def _tile_lanes(t, width):
    return jnp.concatenate([t] * (width // LANES), axis=1) if width > LANES else t


def _rope(x, c, sa, sb):
    w = x.shape[1]
    return (x * _tile_lanes(c, w) + pltpu.roll(x, w - 8, 1) * _tile_lanes(sa, w)
            + pltpu.roll(x, 8, 1) * _tile_lanes(sb, w))


def _inproj_kernel(x_ref, sc_ref, sh_ref, g_ref, wh_ref, wl_ref, cs_ref, sa_ref, sb_ref,
                   u_ref, q_ref, k_ref, v_ref, kb_ref, vb_ref, iq_ref, ikw_ref, ga_ref, gb_ref):
    x = x_ref[0]
    h = x * lax.rsqrt(jnp.mean(x * x, axis=-1, keepdims=True) + EPS) * g_ref[...]
    h = h * (1.0 + sc_ref[0]) + sh_ref[0]
    hh, hl = _split(h)
    c, sa, sb = cs_ref[...], sa_ref[...], sb_ref[...]

    glu = _dg(hh, wh_ref[:, C_GLU:C_Q], 1, 0)
    u_ref[0] = glu[:, :CONV_DIM] * _sigmoid(glu[:, CONV_DIM:])

    qk = _dg(hh, wh_ref[:, C_Q:C_V], 1, 0)
    q_ref[0] = _rope(qk[:, :ATTN_DIM], c, sa, sb)
    k = _rope(qk[:, ATTN_DIM:], c, sa, sb)
    k_ref[0] = k
    kb_ref[0] = k.astype(BF16)
    v = _dg(hh, wh_ref[:, C_V:C_IQ], 1, 0)
    v_ref[0] = v
    vb_ref[0] = v.astype(BF16)

    wih = wh_ref[:, C_IQ:C_GA]
    idx = _dg(hh, wih, 1, 0) + _dg(hl, wih, 1, 0) + _dg(hh, wl_ref[...], 1, 0)
    iq_ref[0] = _rope(idx[:, :ATTN_DIM], c, sa, sb)
    lane = lax.broadcasted_iota(I32, (1, LANES), 1)
    is_ik = lane < IDX_DIM
    iw_scale = float((IDX_HEADS * IDX_DIM) ** -0.5)
    ikw_ref[0] = _rope(idx[:, ATTN_DIM:], jnp.where(is_ik, c, iw_scale),
                       jnp.where(is_ik, sa, 0.0), jnp.where(is_ik, sb, 0.0))

    ga_ref[0] = _sigmoid(_dg(hh, wh_ref[:, C_GA:C_GB], 1, 0))
    gb_ref[0] = _sigmoid(_dg(hh, wh_ref[:, C_GB:C_END], 1, 0))


def _mod_spec(arr, tm):
    d = arr.shape[-1]
    if arr.shape[1] == 1:
        return pl.BlockSpec((1, 1, d), lambda b, i: (b, 0, 0))
    return pl.BlockSpec((1, tm, d), lambda b, i: (b, i, 0))


def _inproj(x, scale1, shift1, g_mix, wh, wl, cs, sa, sb, tm):
    bx, s, d = x.shape
    row = lambda w: pl.BlockSpec((1, tm, w), lambda b, i: (b, i, 0))
    tab = pl.BlockSpec((tm, LANES), lambda b, i: (i, 0))
    out_w = (CONV_DIM, ATTN_DIM, ATTN_DIM, ATTN_DIM, ATTN_DIM, ATTN_DIM, ATTN_DIM, LANES, d, d)
    out_t = (F32, F32, F32, F32, BF16, BF16, F32, F32, F32, F32)
    return pl.pallas_call(
        _inproj_kernel,
        grid=(bx, s // tm),
        in_specs=[row(d), _mod_spec(scale1, tm), _mod_spec(shift1, tm), _const_spec((1, d)),
                  _const_spec(wh.shape), _const_spec(wl.shape), tab, tab, tab],
        out_specs=[row(w) for w in out_w],
        out_shape=[jax.ShapeDtypeStruct((bx, s, w), t) for w, t in zip(out_w, out_t)],
        compiler_params=_params(("parallel", "parallel")),
        name="inproj",
    )(x, scale1, shift1, g_mix.reshape(1, d), wh, wl, cs, sa, sb)


def _conv_kernel(u_ref, hist_ref, ga_ref, wdw_ref, bdw_ref, lg_ref, lb_ref, wo_ref, o_ref, win_ref, *, t):
    @pl.when(pl.program_id(1) == 0)
    def _():
        win_ref[0:HALO, :] = hist_ref[0]

    win_ref[HALO:HALO + t, :] = u_ref[0]
    off = HALO - (CONV_WIDTH - 1)
    acc = jnp.zeros((t, CONV_DIM), F32)
    for j in range(CONV_WIDTH):
        acc = acc + win_ref[off + j:off + j + t, :] * wdw_ref[j:j + 1, :]
    dw = acc + bdw_ref[...]
    mu = jnp.mean(dw, axis=-1, keepdims=True)
    var = jnp.mean(jnp.square(dw - mu), axis=-1, keepdims=True)
    y = (dw - mu) * lax.rsqrt(var + EPS) * lg_ref[...] + lb_ref[...]
    y = y * _sigmoid(y)
    o_ref[0] = ga_ref[0] * _dg(y.astype(BF16), wo_ref[...], 1, 0)
    tail = win_ref[t:t + HALO, :]
    win_ref[0:HALO, :] = tail


def _conv_branch(u, hist, ga, w_dw, b_dw, ln_g, ln_b, w_out_b, t):
    bx, s, _ = u.shape
    d = ga.shape[-1]
    wdw = jnp.pad(w_dw, ((0, HALO - CONV_WIDTH), (0, 0)))
    vec = lambda a: a.reshape(1, CONV_DIM)
    return pl.pallas_call(
        functools.partial(_conv_kernel, t=t),
        grid=(bx, s // t),
        in_specs=[pl.BlockSpec((1, t, CONV_DIM), lambda b, i: (b, i, 0)),
                  pl.BlockSpec((1, HALO, CONV_DIM), lambda b, i: (b, 0, 0)),
                  pl.BlockSpec((1, t, d), lambda b, i: (b, i, 0)),
                  _const_spec((HALO, CONV_DIM)), _const_spec((1, CONV_DIM)), _const_spec((1, CONV_DIM)),
                  _const_spec((1, CONV_DIM)), _const_spec((CONV_DIM, d))],
        out_specs=pl.BlockSpec((1, t, d), lambda b, i: (b, i, 0)),
        out_shape=jax.ShapeDtypeStruct((bx, s, d), F32),
        scratch_shapes=[pltpu.VMEM((t + HALO, CONV_DIM), F32)],
        compiler_params=_params(("parallel", "arbitrary")),
        name="conv_branch",
    )(u, hist, ga, wdw, vec(b_dw), vec(ln_g), vec(ln_b), w_out_b)


def _select_threshold(count_fn, n_sel, rows, pos_bits):
    def bit_body(it, lo):
        cand = lo + lax.shift_left(jnp.int32(1), 31 - it)
        cnt = count_fn(lambda k, p: k >= cand)
        return jnp.where(cnt >= n_sel, cand, lo)

    tau = lax.fori_loop(0, 32, bit_body, jnp.full((rows, 1), INT_MIN, I32))
    need = n_sel - count_fn(lambda k, p: k > tau)

    def pos_body(it, x):
        cand = x + lax.shift_left(jnp.int32(1), pos_bits - 1 - it)
        cnt = count_fn(lambda k, p: (k == tau) & (p < cand))
        return jnp.where(cnt < need, cand, x)

    cut = lax.fori_loop(0, pos_bits, pos_body, jnp.zeros((rows, 1), I32))
    return tau, cut


def _dsa_prompt_kernel(iq_ref, ikwq_ref, q_ref, ikw_ref, kb_ref, vb_ref, o_ref,
                       key_ref, ik3_ref, a3_ref, w_ref, q2_ref, m_ref, l_ref, acc_ref,
                       *, n_sel, qb, kc, s_len):
    i = pl.program_id(1)
    n_ch = (i * qb + qb + kc - 1) // kc
    n_sub = kc // LANES
    lane = lax.broadcasted_iota(I32, (1, LANES), 1)
    low_half = lane < HEAD_DIM

    @pl.when(i == 0)
    def _():
        def body(c, carry):
            blk = ikw_ref[0, pl.ds(pl.multiple_of(c * kc, kc), kc), :]
            ikm = jnp.where(low_half, blk, 0.0)
            ikd = ikm + pltpu.roll(ikm, HEAD_DIM, 1)
            hi, lo = _split(ikd)
            ik3_ref[c] = jnp.concatenate([hi, lo], axis=1)
            return carry
        lax.fori_loop(0, s_len // kc, body, 0)

    iq = iq_ref[0]
    ikwq = ikwq_ref[0]
    q = q_ref[0] * (HEAD_DIM ** -0.5)
    for h in range(IDX_HEADS):
        keep = low_half if h % 2 == 0 else jnp.logical_not(low_half)
        pair = slice((h // 2) * LANES, (h // 2 + 1) * LANES)
        hi, lo = _split(jnp.where(keep, iq[:, pair], 0.0))
        a3_ref[h] = jnp.concatenate([hi, hi, lo], axis=1)
        w_ref[h] = jnp.broadcast_to(ikwq[:, IDX_DIM + h:IDX_DIM + h + 1], (qb, LANES))
        q2_ref[h] = jnp.where(keep, q[:, pair], 0.0).astype(BF16)

    qpos = i * qb + lax.broadcasted_iota(I32, (qb, 1), 0)
    sub_lane = lax.broadcasted_iota(I32, (1, kc), 1)

    def score_body(c, carry):
        kk = ik3_ref[c]
        tot = jnp.zeros((qb, kc), F32)
        for h in range(IDX_HEADS):
            a3 = a3_ref[h]
            s = _dg(a3[:, :2 * LANES], kk, 1, 1) + _dg(a3[:, 2 * LANES:], kk[:, :LANES], 1, 1)
            tot = tot + jnp.maximum(s, 0.0) * _tile_lanes(w_ref[h], kc)
        causal = (c * kc + sub_lane) <= qpos
        key_ref[c] = _sort_key(jnp.where(causal, tot, -jnp.inf))
        return carry
    lax.fori_loop(0, n_ch, score_body, 0)

    def count_fn(pred):
        def body(c, acc):
            kblk = key_ref[c]
            for j in range(n_sub):
                pos = c * kc + j * LANES + lane
                acc = acc + pred(kblk[:, j * LANES:(j + 1) * LANES], pos).astype(I32)
            return acc
        acc = lax.fori_loop(0, n_ch, body, jnp.zeros((qb, LANES), I32))
        return jnp.sum(acc, axis=1, keepdims=True)

    tau, cut = _select_threshold(count_fn, n_sel, qb, max(1, (s_len - 1).bit_length()) + 1)

    m_ref[...] = jnp.full(m_ref.shape, NEG, F32)
    l_ref[...] = jnp.zeros(l_ref.shape, F32)
    acc_ref[...] = jnp.zeros(acc_ref.shape, F32)

    def attn_body(c, carry):
        kblk = key_ref[c]
        pos = c * kc + sub_lane
        sel = ((kblk > tau) | ((kblk == tau) & (pos <= cut))) & (pos <= qpos)
        bias = jnp.where(sel, 0.0, NEG)
        rows = pl.ds(pl.multiple_of(c * kc, kc), kc)
        for hp in range(N_HEADS // 2):
            kpair = kb_ref[0, rows, hp * LANES:(hp + 1) * LANES]
            vpair = vb_ref[0, rows, hp * LANES:(hp + 1) * LANES]
            outs, alphas = [], []
            for h in (2 * hp, 2 * hp + 1):
                s = _dg(q2_ref[h], kpair, 1, 1) + bias
                m_old = m_ref[h]
                m_new = jnp.maximum(m_old, jnp.max(s, axis=1, keepdims=True))
                p = jnp.exp(s - _tile_lanes(m_new, kc))
                alpha = jnp.exp(m_old - m_new)
                psum = p[:, :LANES]
                for j in range(1, n_sub):
                    psum = psum + p[:, j * LANES:(j + 1) * LANES]
                l_ref[h] = alpha * l_ref[h] + psum
                m_ref[h] = m_new
                outs.append(_dg(p.astype(BF16), vpair, 1, 0))
                alphas.append(alpha)
            acc_ref[hp] = (jnp.where(low_half, alphas[0], alphas[1]) * acc_ref[hp]
                           + jnp.where(low_half, outs[0], outs[1]))
        return carry
    lax.fori_loop(0, n_ch, attn_body, 0)

    for hp in range(N_HEADS // 2):
        l0 = jnp.sum(l_ref[2 * hp], axis=1, keepdims=True)
        l1 = jnp.sum(l_ref[2 * hp + 1], axis=1, keepdims=True)
        o_ref[0, :, hp * LANES:(hp + 1) * LANES] = acc_ref[hp] / jnp.where(low_half, l0, l1)


def _dsa_prompt(iq, ikw, q, kb, vb, qb, kc):
    b, s, _ = q.shape
    n_sel = min(TOPK_MAX, s // 4)
    full = lambda w: pl.BlockSpec((1, s, w), lambda bb, i: (bb, 0, 0), pipeline_mode=pl.Buffered(1))
    blk = lambda w: pl.BlockSpec((1, qb, w), lambda bb, i: (bb, i, 0))
    return pl.pallas_call(
        functools.partial(_dsa_prompt_kernel, n_sel=n_sel, qb=qb, kc=kc, s_len=s),
        grid=(b, s // qb),
        in_specs=[blk(ATTN_DIM), blk(LANES), blk(ATTN_DIM), full(LANES), full(ATTN_DIM), full(ATTN_DIM)],
        out_specs=blk(ATTN_DIM),
        out_shape=jax.ShapeDtypeStruct((b, s, ATTN_DIM), F32),
        scratch_shapes=[pltpu.VMEM((s // kc, qb, kc), I32),
                        pltpu.VMEM((s // kc, kc, 2 * LANES), BF16),
                        pltpu.VMEM((IDX_HEADS, qb, 3 * LANES), BF16),
                        pltpu.VMEM((IDX_HEADS, qb, LANES), F32),
                        pltpu.VMEM((N_HEADS, qb, LANES), BF16),
                        pltpu.VMEM((N_HEADS, qb, LANES), F32),
                        pltpu.VMEM((N_HEADS, qb, LANES), F32),
                        pltpu.VMEM((N_HEADS // 2, qb, LANES), F32)],
        compiler_params=_params(("parallel", "arbitrary")),
        name="dsa_prompt",
    )(iq, ikw, q, ikw, kb, vb)


def _sample_select_kernel(pt_ref, qh_ref, ql_ref, w_ref, new_ref, *rest, g, n_grp, n_sel, past, t):
    pages = rest[:g]
    o_ref, key_ref, cur_ref = rest[g:]
    j = pl.program_id(1)
    lane = lax.broadcasted_iota(I32, (1, LANES), 1)
    qh, ql, w = qh_ref[0], ql_ref[0], w_ref[0]

    def scores(ik):
        kh, kl = _split(ik)
        s = _dot3(qh, ql, kh, kl, 1, 1)
        tot = jnp.maximum(s, 0.0) * w
        return jnp.sum(tot.reshape(IDX_HEADS, t, LANES), axis=0)

    @pl.when(j < n_grp)
    def _():
        key_ref[j] = jnp.concatenate([_sort_key(scores(pages[r][0])) for r in range(g)], axis=1)

    @pl.when(j == n_grp)
    def _():
        tpos = lax.broadcasted_iota(I32, (t, 1), 0)
        cur_ref[...] = _sort_key(jnp.where(lane <= tpos, scores(new_ref[0]), -jnp.inf))

        def count_fn(pred):
            def body(c, acc):
                kblk = key_ref[c]
                for r in range(g):
                    pos = (c * g + r) * LANES + lane
                    acc = acc + pred(kblk[:, r * LANES:(r + 1) * LANES], pos).astype(I32)
                return acc
            acc = lax.fori_loop(0, n_grp, body, jnp.zeros((t, LANES), I32))
            acc = acc + pred(cur_ref[...], past + lane).astype(I32)
            return jnp.sum(acc, axis=1, keepdims=True)

        tau, cut = _select_threshold(count_fn, n_sel, t, max(1, (past + LANES - 1).bit_length()) + 1)
        wide_lane = lax.broadcasted_iota(I32, (1, g * LANES), 1)

        def out_body(c, carry):
            kblk = key_ref[c]
            pos = c * g * LANES + wide_lane
            sel = (kblk > tau) | ((kblk == tau) & (pos <= cut))
            o_ref[0, c] = jnp.where(sel, 0.0, NEG)
            return carry
        lax.fori_loop(0, n_grp, out_body, 0)
        kcur = cur_ref[...]
        pos = past + lane
        sel = ((kcur > tau) | ((kcur == tau) & (pos <= cut))) & (lane <= tpos)
        cur_bias = jnp.where(sel, 0.0, NEG)
        o_ref[0, n_grp] = jnp.concatenate([cur_bias] + [jnp.full((t, LANES), NEG, F32)] * (g - 1), axis=1)


def _page_specs(g, n_grp, layer_base, block_tail):
    nd = len(block_tail)

    def make(r):
        def index_map(b, j, pt):
            return (layer_base + pt[b, jnp.minimum(j, n_grp - 1) * g + r],) + (0,) * nd
        return pl.BlockSpec((1,) + block_tail, index_map)
    return [make(r) for r in range(g)]


def _sample_select(page_table, qh, ql, w, ik_new, idx_pages, layer_base, g, t):
    db, n_pages = page_table.shape
    n_grp = n_pages // g
    past = n_pages * PAGE
    n_sel = min(TOPK_MAX, (past + t) // 4)
    rows = IDX_HEADS * t
    per_b = lambda shape: pl.BlockSpec((1,) + shape, lambda b, j, pt: (b,) + (0,) * len(shape))
    kern = functools.partial(_sample_select_kernel, g=g, n_grp=n_grp, n_sel=n_sel, past=past, t=t)
    return pl.pallas_call(
        kern,
        grid_spec=pltpu.PrefetchScalarGridSpec(
            num_scalar_prefetch=1,
            grid=(db, n_grp + 1),
            in_specs=[per_b((rows, IDX_DIM)), per_b((rows, IDX_DIM)), per_b((rows, LANES)),
                      per_b((PAGE, IDX_DIM))] + _page_specs(g, n_grp, layer_base, (PAGE, IDX_DIM)),
            out_specs=per_b((n_grp + 1, t, g * LANES)),
            scratch_shapes=[pltpu.VMEM((n_grp, t, g * LANES), I32), pltpu.VMEM((t, LANES), I32)]),
        out_shape=jax.ShapeDtypeStruct((db, n_grp + 1, t, g * LANES), F32),
        compiler_params=_params(("parallel", "arbitrary")),
        name="sample_select",
    )(page_table, qh, ql, w, ik_new, *([idx_pages] * g))


def _sample_attend_kernel(pt_ref, q_ref, bias_ref, knew_ref, vnew_ref, *rest, g, n_grp, t):
    kpages, vpages = rest[:g], rest[g:2 * g]
    o_ref, m_ref, l_ref, acc_ref = rest[2 * g:]
    j = pl.program_id(1)
    rows = N_HEADS * t

    @pl.when(j == 0)
    def _():
        m_ref[...] = jnp.full(m_ref.shape, NEG, F32)
        l_ref[...] = jnp.zeros(l_ref.shape, F32)
        acc_ref[...] = jnp.zeros(acc_ref.shape, F32)

    def step(kmat, vmat, bias):
        n_sub = kmat.shape[0] // LANES
        s = _dg(q_ref[0], kmat.astype(BF16), 1, 1) + jnp.concatenate([bias] * N_HEADS, axis=0)
        m_old = m_ref[...]
        m_new = jnp.maximum(m_old, jnp.max(s, axis=1, keepdims=True))
        p = jnp.exp(s - _tile_lanes(m_new, kmat.shape[0]))
        alpha = jnp.exp(m_old - m_new)
        psum = p[:, :LANES]
        for r in range(1, n_sub):
            psum = psum + p[:, r * LANES:(r + 1) * LANES]
        l_ref[...] = alpha * l_ref[...] + psum
        m_ref[...] = m_new
        acc_ref[...] = (_tile_lanes(alpha, ATTN_DIM) * acc_ref[...]
                        + _dg(p.astype(BF16), vmat.astype(BF16), 1, 0))

    @pl.when(j < n_grp)
    def _():
        step(jnp.concatenate([kp[0] for kp in kpages], axis=0),
             jnp.concatenate([vp[0] for vp in vpages], axis=0), bias_ref[0, 0])

    @pl.when(j == n_grp)
    def _():
        step(knew_ref[0], vnew_ref[0], bias_ref[0, 0][:, :LANES])
        full = acc_ref[...] / jnp.sum(l_ref[...], axis=1, keepdims=True)
        lane = lax.broadcasted_iota(I32, (1, ATTN_DIM), 1)
        out = jnp.zeros((t, ATTN_DIM), F32)
        for h in range(N_HEADS):
            in_head = (lane >= h * HEAD_DIM) & (lane < (h + 1) * HEAD_DIM)
            out = out + jnp.where(in_head, full[h * t:(h + 1) * t, :], 0.0)
        o_ref[0] = out


def _sample_attend(page_table, qrows, bias, k_new, v_new, k_pages, v_pages, layer_base, g, t):
    db, n_pages = page_table.shape
    n_grp = n_pages // g
    rows = N_HEADS * t
    per_b = lambda shape: pl.BlockSpec((1,) + shape, lambda b, j, pt: (b,) + (0,) * len(shape))
    kern = functools.partial(_sample_attend_kernel, g=g, n_grp=n_grp, t=t)
    return pl.pallas_call(
        kern,
        grid_spec=pltpu.PrefetchScalarGridSpec(
            num_scalar_prefetch=1,
            grid=(db, n_grp + 1),
            in_specs=[per_b((rows, ATTN_DIM)),
                      pl.BlockSpec((1, 1, t, g * LANES), lambda b, j, pt: (b, j, 0, 0)),
                      per_b((PAGE, ATTN_DIM)), per_b((PAGE, ATTN_DIM))]
                     + _page_specs(g, n_grp, layer_base, (PAGE, ATTN_DIM))
                     + _page_specs(g, n_grp, layer_base, (PAGE, ATTN_DIM)),
            out_specs=per_b((t, ATTN_DIM)),
            scratch_shapes=[pltpu.VMEM((rows, LANES), F32), pltpu.VMEM((rows, LANES), F32),
                            pltpu.VMEM((rows, ATTN_DIM), F32)]),
        out_shape=jax.ShapeDtypeStruct((db, t, ATTN_DIM), F32),
        compiler_params=_params(("parallel", "arbitrary")),
        name="sample_attend",
    )(page_table, qrows, bias, k_new, v_new, *([k_pages] * g), *([v_pages] * g))


def _mix_kernel(x_ref, ag_ref, gb_ref, at_ref, g1_ref, sc2_ref, sh2_ref, gf_ref,
                wat_ref, wo_ref, wqh_ref, wql_ref, k1h_ref, k1l_ref, k2h_ref, k2l_ref,
                x1_ref, h2_ref, s1_ref, s2_ref):
    bb = _dg(at_ref[0].astype(BF16), wat_ref[...], 1, 0)
    mixed = ag_ref[0] + gb_ref[0] * bb
    x1 = x_ref[0] + g1_ref[0] * _dg(mixed.astype(BF16), wo_ref[...], 1, 0)
    x1_ref[0] = x1
    h2 = x1 * lax.rsqrt(jnp.mean(x1 * x1, axis=-1, keepdims=True) + EPS) * gf_ref[...]
    h2 = h2 * (1.0 + sc2_ref[0]) + sh2_ref[0]
    h2_ref[0] = h2.astype(BF16)
    hh, hl = _split(h2)
    qp = _dot3(hh, hl, wqh_ref[...], wql_ref[...], 1, 0)
    for h in range(PEER_HEADS):
        ah, al = _split(qp[:, h * PEER_QDIM:h * PEER_QDIM + PEER_HALF])
        bh, bl = _split(qp[:, h * PEER_QDIM + PEER_HALF:(h + 1) * PEER_QDIM])
        s1_ref[h] = _dot3(k1h_ref[...], k1l_ref[...], ah, al, 1, 1)
        s2_ref[h] = _dot3(k2h_ref[...], k2l_ref[...], bh, bl, 1, 1)


def _mix(x, ag, gb, attn, gate1, scale2, shift2, g_ffn, wat_b, wo_b, wq_h, wq_l, k1, k2, tm):
    bx, s, d = x.shape
    n = bx * s
    row = lambda w: pl.BlockSpec((1, tm, w), lambda b, i: (b, i, 0))
    tr = pl.BlockSpec((PEER_HEADS, PEER_KEYS, tm), lambda b, i: (0, 0, b * (s // tm) + i))
    k1h, k1l = _split(k1)
    k2h, k2l = _split(k2)
    return pl.pallas_call(
        _mix_kernel,
        grid=(bx, s // tm),
        in_specs=[row(d), row(d), row(d), row(ATTN_DIM), _mod_spec(gate1, tm), _mod_spec(scale2, tm),
                  _mod_spec(shift2, tm), _const_spec((1, d)), _const_spec(wat_b.shape), _const_spec(wo_b.shape),
                  _const_spec(wq_h.shape), _const_spec(wq_l.shape)] + [_const_spec((PEER_KEYS, PEER_HALF))] * 4,
        out_specs=[row(d), row(d), tr, tr],
        out_shape=[jax.ShapeDtypeStruct((bx, s, d), F32), jax.ShapeDtypeStruct((bx, s, d), BF16),
                   jax.ShapeDtypeStruct((PEER_HEADS, PEER_KEYS, n), F32),
                   jax.ShapeDtypeStruct((PEER_HEADS, PEER_KEYS, n), F32)],
        compiler_params=_params(("parallel", "parallel")),
        name="mix_peer_query",
    )(x, ag, gb, attn, gate1, scale2, shift2, g_ffn.reshape(1, d), wat_b, wo_b, wq_h, wq_l, k1h, k1l, k2h, k2l)


A_PER_STEP = 8
N_CAND = 80


def _top16(s):
    rows = lax.broadcasted_iota(I32, s.shape, 0).astype(F32)
    rank = jnp.full(s.shape, NO_RANK, F32)
    vals = []
    for it in range(PEER_TOPK):
        m = jnp.max(s, axis=0, keepdims=True)
        first = jnp.min(jnp.where(s == m, rows, float(PEER_KEYS)), axis=0, keepdims=True)
        pick = rows == first
        rank = jnp.where(pick, float(it), rank)
        s = jnp.where(pick, -jnp.inf, s)
        vals.append(m)
    return jnp.concatenate(vals, axis=0), rank


def _peer_select_kernel(s1_ref, s2_ref, e1_ref, jr_ref, e2_ref, r2_ref):
    tn = s1_ref.shape[-1]
    r = lax.broadcasted_iota(I32, (N_CAND, 1), 0)
    mid_i = lax.shift_right_arithmetic(r - 16, 3) + 1
    mid_j = (r - 16) & 7
    flat = jnp.where(r < 16, r, jnp.where(r < 72, mid_i * 16 + mid_j, (r - 64) * 16)).astype(F32)
    limit = jnp.where(mid_i == 1, 8, jnp.where(mid_i == 2, 5, jnp.where(mid_i == 3, 4, jnp.where(mid_i == 4, 3, 2))))
    cell_ok = (r < 16) | (r >= 72) | (mid_j < limit)
    row8 = lax.broadcasted_iota(I32, (8, 1), 0)

    for h in range(PEER_HEADS):
        s1, s2 = s1_ref[h], s2_ref[h]
        v1, rank1 = _top16(s1)
        v2, rank2 = _top16(s2)
        groups = [v1[0:1] + v2[0:16]] + [v1[i:i + 1] + v2[0:8] for i in range(1, 8)] + [v1[8:16] + v2[0:1]]
        cand = jnp.where(cell_ok, jnp.concatenate(groups, axis=0), -jnp.inf)
        chosen = jnp.zeros((N_CAND, tn), F32)
        m0 = v1[0:1] + v2[0:1]
        zsum = jnp.zeros((1, tn), F32)
        for it in range(PEER_TOPK):
            m = jnp.max(cand, axis=0, keepdims=True)
            first = jnp.min(jnp.where(cand == m, flat, 4096.0), axis=0, keepdims=True)
            pick = flat == first
            chosen = jnp.where(pick, 1.0, chosen)
            cand = jnp.where(pick, -jnp.inf, cand)
            zsum = zsum + jnp.exp(m - m0)
        j_low = jnp.zeros((8, tn), F32)
        j_low = jnp.where(row8 == 0, jnp.sum(chosen[0:16], axis=0, keepdims=True), j_low)
        for i in range(1, 8):
            j_low = jnp.where(row8 == i, jnp.sum(chosen[8 + 8 * i:16 + 8 * i], axis=0, keepdims=True), j_low)
        j_high = chosen[72:80]
        jr = jnp.zeros((PEER_KEYS, tn), F32)
        for i in range(PEER_TOPK):
            ji = j_low[i:i + 1] if i < 8 else j_high[i - 8:i - 7]
            jr = jnp.where(rank1 == float(i), ji, jr)
        jr_ref[h] = jr
        e1_ref[h] = jnp.where(rank1 < NO_RANK, jnp.exp(s1 - v1[0:1]), 0.0) / zsum
        e2_ref[h] = jnp.where(rank2 < NO_RANK, jnp.exp(s2 - v2[0:1]), 0.0)
        r2_ref[h] = rank2


def _peer_select(s1t, s2t, tn):
    _, _, n = s1t.shape
    spec = pl.BlockSpec((PEER_HEADS, PEER_KEYS, tn), lambda i: (0, 0, i))
    shape = jax.ShapeDtypeStruct(s1t.shape, F32)
    return pl.pallas_call(
        _peer_select_kernel,
        grid=(n // tn,),
        in_specs=[spec, spec],
        out_specs=[spec] * 4,
        out_shape=[shape] * 4,
        compiler_params=_params(("parallel",)),
        name="peer_select",
    )(s1t, s2t)


def _peer_expert_kernel(h2_ref, x1_ref, g2_ref, gfin_ref, u_ref, vt_ref, e1_ref, jr_ref, e2_ref, r2_ref,
                        y_ref, acc_ref, pt_ref, *, a_per, final_norm):
    j = pl.program_id(2)

    @pl.when(j == 0)
    def _():
        acc_ref[...] = jnp.zeros(acc_ref.shape, F32)

    zt = _dg(u_ref[...], h2_ref[0], 1, 1)
    gz = 0.5 * zt * (1.0 + lax.erf(zt * (2.0 ** -0.5)))

    for al in range(a_per):
        wsum = jnp.zeros((PEER_KEYS, zt.shape[1]), F32)
        for h in range(PEER_HEADS):
            jr = jr_ref[h, al:al + 1, :]
            e1 = e1_ref[h, al:al + 1, :]
            wsum = wsum + jnp.where(r2_ref[h] < jr, e2_ref[h] * e1, 0.0)
        rows = slice(al * PEER_KEYS, (al + 1) * PEER_KEYS)
        pt_ref[rows, :] = (wsum * gz[rows, :]).astype(BF16)
    acc_ref[...] += _dg(vt_ref[...], pt_ref[...], 1, 0)

    @pl.when(j == pl.num_programs(2) - 1)
    def _():
        x2 = x1_ref[0] + g2_ref[0] * acc_ref[...].T
        if final_norm:
            x2 = x2 * lax.rsqrt(jnp.mean(x2 * x2, axis=-1, keepdims=True) + EPS) * gfin_ref[...]
        y_ref[0] = x2


def _peer_expert(h2b, x1, gate2, g_final, u_b, vt_b, e1, jr, e2, r2, tn, a_per, final_norm):
    bx, s, d = x1.shape
    n_exp = u_b.shape[0]
    ec = a_per * PEER_KEYS
    tiles = s // tn
    row = lambda w: pl.BlockSpec((1, tn, w), lambda b, i, j: (b, i, 0))
    mod = (pl.BlockSpec((1, 1, d), lambda b, i, j: (b, 0, 0)) if gate2.shape[1] == 1
           else pl.BlockSpec((1, tn, d), lambda b, i, j: (b, i, 0)))
    per_a = pl.BlockSpec((PEER_HEADS, a_per, tn), lambda b, i, j: (0, j, b * tiles + i))
    per_b = pl.BlockSpec((PEER_HEADS, PEER_KEYS, tn), lambda b, i, j: (0, 0, b * tiles + i))
    return pl.pallas_call(
        functools.partial(_peer_expert_kernel, a_per=a_per, final_norm=final_norm),
        grid=(bx, tiles, n_exp // ec),
        in_specs=[row(d), row(d), mod, pl.BlockSpec((1, d), lambda b, i, j: (0, 0)),
                  pl.BlockSpec((ec, d), lambda b, i, j: (j, 0)),
                  pl.BlockSpec((d, ec), lambda b, i, j: (0, j)),
                  per_a, per_a, per_b, per_b],
        out_specs=row(d),
        out_shape=jax.ShapeDtypeStruct((bx, s, d), F32),
        scratch_shapes=[pltpu.VMEM((d, tn), F32), pltpu.VMEM((ec, tn), BF16)],
        compiler_params=_params(("parallel", "parallel", "arbitrary")),
        name="peer_expert",
    )(h2b, x1, gate2, g_final.reshape(1, d), u_b, vt_b, e1, jr, e2, r2)


def _rope_tables(pos):
    rot = HEAD_DIM // 4
    half = rot // 2
    inv_freq = ROPE_THETA ** (-(jnp.arange(half, dtype=F32) * (2.0 / rot)))
    ang = pos.astype(F32)[:, None] * inv_freq[None, :]
    cos, sin = jnp.cos(ang), jnp.sin(ang)
    t = pos.shape[0]
    ones = jnp.ones((t, HEAD_DIM - rot), F32)
    zeros = jnp.zeros((t, HEAD_DIM - rot), F32)
    z8 = jnp.zeros((t, half), F32)
    c = jnp.concatenate([cos, cos, ones], axis=1)
    sa = jnp.concatenate([-sin, z8, zeros], axis=1)
    sb = jnp.concatenate([z8, sin, zeros], axis=1)
    dup = lambda a: jnp.concatenate([a, a], axis=1)
    return dup(c), dup(sa), dup(sb)


def _pack_w_in(w_in):
    d = w_in.shape[0]
    cols = C_IKW - C_GLU
    pad = jnp.zeros((d, LANES - IDX_DIM - IDX_HEADS), w_in.dtype)
    return jnp.concatenate([w_in[:, :cols + IDX_DIM + IDX_HEADS], pad, w_in[:, cols + IDX_DIM + IDX_HEADS:]], axis=1)


def _pick_tile(n, pref):
    t = min(n, pref)
    while n % t:
        t //= 2
    return t


def kernel(x_prompt, x_sample, c_prompt, c_sample, cache_k, cache_v, cache_idx_k, state_conv, page_table, w_ada, b_ada, g_norm_mix, w_in, w_dw, b_dw, ln_conv_g, ln_conv_b, w_conv_out, w_attn_out, w_o, g_norm_ffn, w_peer_q, peer_k1, peer_k2, peer_u, peer_v, g_norm_final):
    b, s, d = x_prompt.shape
    db, t, _ = x_sample.shape
    depth = w_ada.shape[0]
    n_phys = cache_k.shape[1]
    n_pages = page_table.shape[1]
    past = n_pages * PAGE
    ns = db * t
    g_pages = math.gcd(n_pages, 8)

    tabs_p = _rope_tables(jnp.arange(s))
    tabs_s = _rope_tables(jnp.tile(past + jnp.arange(t), db))
    idx_pages = cache_idx_k.reshape(depth * n_phys, PAGE, IDX_DIM)
    k_pages = cache_k.reshape(depth * n_phys, PAGE, ATTN_DIM)
    v_pages = cache_v.reshape(depth * n_phys, PAGE, ATTN_DIM)
    c_all = jnp.concatenate([c_prompt, c_sample], axis=0)
    c_rows = -(-c_all.shape[0] // 8) * 8
    c_all = jnp.pad(c_all, ((0, c_rows - c_all.shape[0]), (0, 0)))

    tm_p, tm_s = _pick_tile(s, 256), _pick_tile(ns, 256)
    tc_p = _pick_tile(s, 256)
    qb = _pick_tile(s, 128)
    kc = _pick_tile(s, 512)
    tn_sel_p, tn_sel_s = _pick_tile(b * s, 256), _pick_tile(ns, 256)
    tn_exp_p, tn_exp_s = _pick_tile(s, 512), _pick_tile(ns, 512)

    xp, xs = x_prompt, x_sample.reshape(1, ns, d)
    outs = [[] for _ in range(8)]
    for l in range(depth):
        mod = _ada(c_all, w_ada[l], b_ada[l])
        mod_p = [m.reshape(b, 1, d) for m in jnp.split(mod[:b], 6, axis=-1)]
        mod_s = [jnp.repeat(m, t, axis=0).reshape(1, ns, d) for m in jnp.split(mod[b:b + db], 6, axis=-1)]
        w_pack = _pack_w_in(w_in[l])
        wh = w_pack.astype(BF16)
        wl = (w_pack[:, C_IQ:C_GA] - wh[:, C_IQ:C_GA].astype(F32)).astype(BF16)
        wco_b, wat_b, wo_b = w_conv_out[l].astype(BF16), w_attn_out[l].astype(BF16), w_o[l].astype(BF16)
        wq_h, wq_l = _split(w_peer_q[l])
        u_b = peer_u[l].astype(BF16)
        vt_b = peer_v[l].astype(BF16).T
        conv_w = (w_dw[l], b_dw[l], ln_conv_g[l], ln_conv_b[l], wco_b)

        u, q, k, v, kb, vb, iq, ikw, ga, gb = _inproj(xp, mod_p[1], mod_p[0], g_norm_mix[l], wh, wl, *tabs_p, tm_p)
        hist0 = jnp.zeros((b, HALO, CONV_DIM), F32)
        ag = _conv_branch(u, hist0, ga, *conv_w, tc_p)
        attn = _dsa_prompt(iq, ikw, q, kb, vb, qb, kc)
        x1, h2b, s1t, s2t = _mix(xp, ag, gb, attn, mod_p[2], mod_p[4], mod_p[3], g_norm_ffn[l],
                                 wat_b, wo_b, wq_h, wq_l, peer_k1[l], peer_k2[l], tm_p)
        sel = _peer_select(s1t, s2t, tn_sel_p)
        last = l == depth - 1
        xp = _peer_expert(h2b, x1, mod_p[5], g_norm_final, u_b, vt_b, *sel, tn_exp_p, A_PER_STEP, last)
        outs[0].append(k.reshape(b, s // PAGE, PAGE, N_HEADS, HEAD_DIM))
        outs[1].append(v.reshape(b, s // PAGE, PAGE, N_HEADS, HEAD_DIM))
        outs[2].append(ikw[:, :, :IDX_DIM].reshape(b, s // PAGE, PAGE, IDX_DIM))
        outs[3].append(jnp.concatenate([hist0[:, HALO - (CONV_WIDTH - 1):], u], axis=1)[:, -(CONV_WIDTH - 1):])

        u, q, k, v, kb, vb, iq, ikw, ga, gb = _inproj(xs, mod_s[1], mod_s[0], g_norm_mix[l], wh, wl, *tabs_s, tm_s)
        u3 = u.reshape(db, t, CONV_DIM)
        hist = jnp.pad(state_conv[l], ((0, 0), (HALO - (CONV_WIDTH - 1), 0), (0, 0)))
        ag = _conv_branch(u3, hist, ga.reshape(db, t, d), *conv_w, t).reshape(1, ns, d)
        iq_ht = iq.reshape(db, t, IDX_HEADS, IDX_DIM).transpose(0, 2, 1, 3).reshape(db, IDX_HEADS * t, IDX_DIM)
        iqh, iql = _split(iq_ht)
        iw = ikw.reshape(db, t, LANES)[:, :, IDX_DIM:IDX_DIM + IDX_HEADS]
        w_rows = jnp.broadcast_to(iw.transpose(0, 2, 1).reshape(db, IDX_HEADS * t, 1), (db, IDX_HEADS * t, LANES))
        pad_rows = lambda a: jnp.pad(a.reshape(db, t, -1), ((0, 0), (0, PAGE - t), (0, 0)))
        bias = _sample_select(page_table, iqh, iql, w_rows, pad_rows(ikw[..., :IDX_DIM]), idx_pages,
                              l * n_phys, g_pages, t)
        q4 = q.reshape(db, t, N_HEADS, HEAD_DIM) * (HEAD_DIM ** -0.5)
        eye = jnp.eye(N_HEADS, dtype=F32)
        qrows = jnp.einsum('bthd,hg->bhtgd', q4, eye).reshape(db, N_HEADS * t, ATTN_DIM).astype(BF16)
        attn = _sample_attend(page_table, qrows, bias, pad_rows(k), pad_rows(v), k_pages, v_pages,
                              l * n_phys, g_pages, t).reshape(1, ns, ATTN_DIM)
        x1, h2b, s1t, s2t = _mix(xs, ag, gb, attn, mod_s[2], mod_s[4], mod_s[3], g_norm_ffn[l],
                                 wat_b, wo_b, wq_h, wq_l, peer_k1[l], peer_k2[l], tm_s)
        sel = _peer_select(s1t, s2t, tn_sel_s)
        xs = _peer_expert(h2b, x1, mod_s[5], g_norm_final, u_b, vt_b, *sel, tn_exp_s, A_PER_STEP, last)
        outs[4].append(k.reshape(db, t, N_HEADS, HEAD_DIM))
        outs[5].append(v.reshape(db, t, N_HEADS, HEAD_DIM))
        outs[6].append(ikw.reshape(db, t, LANES)[:, :, :IDX_DIM])
        outs[7].append(jnp.concatenate([state_conv[l], u3], axis=1)[:, -(CONV_WIDTH - 1):])

    stacked = [jnp.stack(o) for o in outs]
    return (xp, xs.reshape(db, t, d), *stacked)
```

```python
import functools
import math

import jax
import jax.numpy as jnp
from jax import lax
from jax.experimental import pallas as pl
from jax.experimental.pallas import tpu as pltpu

F32 = jnp.float32
BF16 = jnp.bfloat16
I32 = jnp.int32

N_HEADS = 8
HEAD_DIM = 64
ATTN_DIM = N_HEADS * HEAD_DIM
ROPE_THETA = 500000.0
IDX_HEADS = 8
IDX_DIM = 64
TOPK_MAX = 256
CONV_DIM = 512
CONV_WIDTH = 31
HALO = 32
PEER_HEADS = 8
PEER_KEYS = 128
PEER_HALF = 128
PEER_QDIM = 256
PEER_TOPK = 16
PAGE = 128
EPS = 1e-6
LANES = 128
INT_MIN = -(2 ** 31)
INT_MAX = 2 ** 31 - 1
NEG = -1e30
NO_RANK = 999.0
COUNT_ROWS = 128
VMEM_LIMIT = 56 * 1024 * 1024

C_GLU, C_Q, C_K, C_V, C_IQ, C_IKW, C_GA, C_GB, C_END = 0, 1024, 1536, 2048, 2560, 3072, 3200, 4224, 5248


def _dg(a, b, ca, cb):
    return lax.dot_general(a, b, (((ca,), (cb,)), ((), ())), preferred_element_type=F32)


def _split(x):
    hi = x.astype(BF16)
    lo = (x - hi.astype(F32)).astype(BF16)
    return hi, lo


def _dot3(ah, al, bh, bl, ca, cb):
    return _dg(ah, bh, ca, cb) + _dg(al, bh, ca, cb) + _dg(ah, bl, ca, cb)


def _sigmoid(x):
    return 1.0 / (1.0 + jnp.exp(-x))


def _const_spec(shape):
    n = len(shape)
    return pl.BlockSpec(shape, lambda *_: (0,) * n, pipeline_mode=pl.Buffered(1))


def _params(sem):
    return pltpu.CompilerParams(dimension_semantics=sem, vmem_limit_bytes=VMEM_LIMIT)


def _sort_key(x):
    bits = lax.bitcast_convert_type(x, I32)
    key = jnp.where(bits < 0, bits ^ jnp.int32(INT_MAX), bits)
    return jnp.where(key == -1, 0, key)


def _ada_kernel(c_ref, w_ref, b_ref, o_ref):
    c = c_ref[...]
    s = c * _sigmoid(c)
    sh, sl = _split(s)
    wh, wl = _split(w_ref[...])
    o_ref[...] = _dot3(sh, sl, wh, wl, 1, 0) + b_ref[...]


def _ada(c, w_ada, b_ada):
    rows, d = c.shape
    n = w_ada.shape[1]
    tn = 1536
    return pl.pallas_call(
        _ada_kernel,
        grid=(n // tn,),
        in_specs=[pl.BlockSpec((rows, d), lambda j: (0, 0)),
                  pl.BlockSpec((d, tn), lambda j: (0, j)),
                  pl.BlockSpec((1, tn), lambda j: (0, j))],
        out_specs=pl.BlockSpec((rows, tn), lambda j: (0, j)),
        out_shape=jax.ShapeDtypeStruct((rows, n), F32),
        compiler_params=_params(("parallel",)),
        name="adaln",
    )(c, w_ada, b_ada.reshape(1, n))


def _tile_lanes(t, width):
    return jnp.concatenate([t] * (width // LANES), axis=1) if width > LANES else t


def _rope(x, c, sa, sb):
    w = x.shape[1]
    return (x * _tile_lanes(c, w) + pltpu.roll(x, w - 8, 1) * _tile_lanes(sa, w)
            + pltpu.roll(x, 8, 1) * _tile_lanes(sb, w))


def _inproj_kernel(x_ref, sc_ref, sh_ref, g_ref, wh_ref, wl_ref, cs_ref, sa_ref, sb_ref,
                   u_ref, q_ref, k_ref, v_ref, kb_ref, vb_ref, iq_ref, ikw_ref, ga_ref, gb_ref):
    x = x_ref[0]
    h = x * lax.rsqrt(jnp.mean(x * x, axis=-1, keepdims=True) + EPS) * g_ref[...]
    h = h * (1.0 + sc_ref[0]) + sh_ref[0]
    hh, hl = _split(h)
    c, sa, sb = cs_ref[...], sa_ref[...], sb_ref[...]

    glu = _dg(hh, wh_ref[:, C_GLU:C_Q], 1, 0)
    u_ref[0] = glu[:, :CONV_DIM] * _sigmoid(glu[:, CONV_DIM:])

    qk = _dg(hh, wh_ref[:, C_Q:C_V], 1, 0)
    q_ref[0] = _rope(qk[:, :ATTN_DIM], c, sa, sb)
    k = _rope(qk[:, ATTN_DIM:], c, sa, sb)
    k_ref[0] = k
    kb_ref[0] = k.astype(BF16)
    v = _dg(hh, wh_ref[:, C_V:C_IQ], 1, 0)
    v_ref[0] = v
    vb_ref[0] = v.astype(BF16)

    wih = wh_ref[:, C_IQ:C_GA]
    idx = _dg(hh, wih, 1, 0) + _dg(hl, wih, 1, 0) + _dg(hh, wl_ref[...], 1, 0)
    iq_ref[0] = _rope(idx[:, :ATTN_DIM], c, sa, sb)
    lane = lax.broadcasted_iota(I32, (1, LANES), 1)
    is_ik = lane < IDX_DIM
    iw_scale = float((IDX_HEADS * IDX_DIM) ** -0.5)
    ikw_ref[0] = _rope(idx[:, ATTN_DIM:], jnp.where(is_ik, c, iw_scale),
                       jnp.where(is_ik, sa, 0.0), jnp.where(is_ik, sb, 0.0))

    ga_ref[0] = _sigmoid(_dg(hh, wh_ref[:, C_GA:C_GB], 1, 0))
    gb_ref[0] = _sigmoid(_dg(hh, wh_ref[:, C_GB:C_END], 1, 0))


def _mod_spec(arr, tm):
    d = arr.shape[-1]
    if arr.shape[1] == 1:
        return pl.BlockSpec((1, 1, d), lambda b, i: (b, 0, 0))
    return pl.BlockSpec((1, tm, d), lambda b, i: (b, i, 0))


def _inproj(x, scale1, shift1, g_mix, wh, wl, cs, sa, sb, tm):
    bx, s, d = x.shape
    row = lambda w: pl.BlockSpec((1, tm, w), lambda b, i: (b, i, 0))
    tab = pl.BlockSpec((tm, LANES), lambda b, i: (i, 0))
    out_w = (CONV_DIM, ATTN_DIM, ATTN_DIM, ATTN_DIM, ATTN_DIM, ATTN_DIM, ATTN_DIM, LANES, d, d)
    out_t = (F32, F32, F32, F32, BF16, BF16, F32, F32, F32, F32)
    return pl.pallas_call(
        _inproj_kernel,
        grid=(bx, s // tm),
        in_specs=[row(d), _mod_spec(scale1, tm), _mod_spec(shift1, tm), _const_spec((1, d)),
                  _const_spec(wh.shape), _const_spec(wl.shape), tab, tab, tab],
        out_specs=[row(w) for w in out_w],
        out_shape=[jax.ShapeDtypeStruct((bx, s, w), t) for w, t in zip(out_w, out_t)],
        compiler_params=_params(("parallel", "parallel")),
        name="inproj",
    )(x, scale1, shift1, g_mix.reshape(1, d), wh, wl, cs, sa, sb)


def _conv_kernel(u_ref, hist_ref, ga_ref, wdw_ref, bdw_ref, lg_ref, lb_ref, wo_ref, o_ref, win_ref, *, t):
    @pl.when(pl.program_id(1) == 0)
    def _():
        win_ref[0:HALO, :] = hist_ref[0]

    win_ref[HALO:HALO + t, :] = u_ref[0]
    off = HALO - (CONV_WIDTH - 1)
    acc = jnp.zeros((t, CONV_DIM), F32)
    for j in range(CONV_WIDTH):
        acc = acc + win_ref[off + j:off + j + t, :] * wdw_ref[j:j + 1, :]
    dw = acc + bdw_ref[...]
    mu = jnp.mean(dw, axis=-1, keepdims=True)
    var = jnp.mean(jnp.square(dw - mu), axis=-1, keepdims=True)
    y = (dw - mu) * lax.rsqrt(var + EPS) * lg_ref[...] + lb_ref[...]
    y = y * _sigmoid(y)
    o_ref[0] = ga_ref[0] * _dg(y.astype(BF16), wo_ref[...], 1, 0)
    tail = win_ref[t:t + HALO, :]
    win_ref[0:HALO, :] = tail


def _conv_branch(u, hist, ga, w_dw, b_dw, ln_g, ln_b, w_out_b, t):
    bx, s, _ = u.shape
    d = ga.shape[-1]
    wdw = jnp.pad(w_dw, ((0, HALO - CONV_WIDTH), (0, 0)))
    vec = lambda a: a.reshape(1, CONV_DIM)
    return pl.pallas_call(
        functools.partial(_conv_kernel, t=t),
        grid=(bx, s // t),
        in_specs=[pl.BlockSpec((1, t, CONV_DIM), lambda b, i: (b, i, 0)),
                  pl.BlockSpec((1, HALO, CONV_DIM), lambda b, i: (b, 0, 0)),
                  pl.BlockSpec((1, t, d), lambda b, i: (b, i, 0)),
                  _const_spec((HALO, CONV_DIM)), _const_spec((1, CONV_DIM)), _const_spec((1, CONV_DIM)),
                  _const_spec((1, CONV_DIM)), _const_spec((CONV_DIM, d))],
        out_specs=pl.BlockSpec((1, t, d), lambda b, i: (b, i, 0)),
        out_shape=jax.ShapeDtypeStruct((bx, s, d), F32),
        scratch_shapes=[pltpu.VMEM((t + HALO, CONV_DIM), F32)],
        compiler_params=_params(("parallel", "arbitrary")),
        name="conv_branch",
    )(u, hist, ga, wdw, vec(b_dw), vec(ln_g), vec(ln_b), w_out_b)


def _select_threshold(count_fn, n_sel, rows, pos_bits, cut_ref):
    def bit_body(it, lo):
        cand = lo + lax.shift_left(jnp.int32(1), 31 - it)
        cnt = count_fn(lambda k, p, c: k >= c, cand)
        return jnp.where(cnt >= n_sel, cand, lo)

    tau = lax.fori_loop(0, 32, bit_body, jnp.full((rows, 1), INT_MIN, I32))
    surplus = count_fn(lambda k, p, c: k >= c, tau) - n_sel
    cut_ref[...] = jnp.full(cut_ref.shape, INT_MAX, I32)

    @pl.when(jnp.max(surplus.astype(F32)) > 0.0)
    def _():
        need = n_sel - count_fn(lambda k, p, c: k > c, tau)

        def pos_body(it, x):
            cand = x + lax.shift_left(jnp.int32(1), pos_bits - 1 - it)
            cnt = count_fn(lambda k, p, c, d: (k == c) & (p < d), tau, cand)
            return jnp.where(cnt < need, cand, x)

        cut = lax.fori_loop(0, pos_bits, pos_body, jnp.zeros((rows, 1), I32))
        cut_ref[...] = jnp.broadcast_to(cut, cut_ref.shape)

    return tau


def _hi_lo_f32(x):
    hi = x.astype(BF16).astype(F32)
    return hi, x - hi


def _dsa_prompt_kernel(iq_ref, ikwq_ref, q_ref, ikw_ref, kb_ref, vb_ref, o_ref,
                       key_ref, ik3_ref, a3_ref, w_ref, q2_ref, cut_ref, s_ref,
                       *, n_sel, qb, kc, s_len):
    i = pl.program_id(1)
    n_ch = (i * qb + qb + kc - 1) // kc
    n_sub = kc // LANES
    lane = lax.broadcasted_iota(I32, (1, LANES), 1)
    low_half = lane < HEAD_DIM

    @pl.when(i == 0)
    def _():
        def body(c, carry):
            blk = ikw_ref[0, pl.ds(pl.multiple_of(c * kc, kc), kc), :]
            hi, lo = _hi_lo_f32(blk)
            first = jnp.where(low_half, hi, pltpu.roll(lo, HEAD_DIM, 1))
            second = jnp.where(low_half, hi, 0.0)
            ik3_ref[c] = jnp.concatenate([first, second], axis=1).astype(BF16)
            return carry
        lax.fori_loop(0, s_len // kc, body, 0)

    iq = iq_ref[0]
    ikwq = ikwq_ref[0]
    q = q_ref[0] * (HEAD_DIM ** -0.5)
    for h in range(IDX_HEADS):
        pair = slice((h // 2) * LANES, (h // 2 + 1) * LANES)
        x = iq[:, pair] if h % 2 == 0 else pltpu.roll(iq[:, pair], HEAD_DIM, 1)
        hi, lo = _hi_lo_f32(x)
        first = jnp.where(low_half, hi, pltpu.roll(hi, HEAD_DIM, 1))
        second = jnp.where(low_half, lo, 0.0)
        a3_ref[h] = jnp.concatenate([first, second], axis=1).astype(BF16)
        w_ref[h] = jnp.broadcast_to(ikwq[:, IDX_DIM + h:IDX_DIM + h + 1], (qb, LANES))
        keep = low_half if h % 2 == 0 else jnp.logical_not(low_half)
        q2_ref[h] = jnp.where(keep, q[:, pair], 0.0).astype(BF16)

    qpos = i * qb + lax.broadcasted_iota(I32, (qb, 1), 0)
    sub_lane = lax.broadcasted_iota(I32, (1, kc), 1)

    def score_body(c, carry):
        kk = ik3_ref[c]
        tot = jnp.zeros((qb, kc), F32)
        for h in range(IDX_HEADS):
            s = _dg(a3_ref[h], kk, 1, 1)
            tot = tot + jnp.maximum(s, 0.0) * _tile_lanes(w_ref[h], kc)
        causal = (c * kc + sub_lane) <= qpos
        key_ref[c] = _sort_key(jnp.where(causal, tot, -jnp.inf))
        return carry
    lax.fori_loop(0, n_ch, score_body, 0)

    def count_fn(pred, *thr):
        counts = []
        for r0 in range(0, qb, COUNT_ROWS):
            rs = slice(r0, min(r0 + COUNT_ROWS, qb))
            nrows = rs.stop - rs.start
            thr_b = [jnp.broadcast_to(x[rs], (nrows, LANES)) for x in thr]

            def body(c, acc):
                kblk = key_ref[c, rs, :]
                for j in range(n_sub):
                    pos = c * kc + j * LANES + lane
                    acc = acc + pred(kblk[:, j * LANES:(j + 1) * LANES], pos, *thr_b).astype(I32)
                return acc
            acc = lax.fori_loop(0, n_ch, body, jnp.zeros((nrows, LANES), I32))
            counts.append(jnp.sum(acc, axis=1, keepdims=True))
        return jnp.concatenate(counts, axis=0) if len(counts) > 1 else counts[0]

    tau = _select_threshold(count_fn, n_sel, qb, max(1, (s_len - 1).bit_length()) + 1, cut_ref)

    for r0 in range(0, qb, COUNT_ROWS):
        rs = slice(r0, min(r0 + COUNT_ROWS, qb))
        shape = (rs.stop - rs.start, LANES)
        tau_b = jnp.broadcast_to(tau[rs], shape)
        qpos_b = jnp.broadcast_to(qpos[rs], shape)
        cut_b = cut_ref[rs, :]

        def bias_body(c, carry):
            kblk = key_ref[c, rs, :]
            for j in range(n_sub):
                k = kblk[:, j * LANES:(j + 1) * LANES]
                pos = c * kc + j * LANES + lane
                sel = ((k > tau_b) | ((k == tau_b) & (pos <= cut_b))) & (pos <= qpos_b)
                key_ref[c, rs, j * LANES:(j + 1) * LANES] = lax.bitcast_convert_type(jnp.where(sel, 0.0, NEG), I32)
            return carry
        lax.fori_loop(0, n_ch, bias_body, 0)

    def fold(x, op):
        out = x[:, :LANES]
        for j in range(1, n_sub):
            out = op(out, x[:, j * LANES:(j + 1) * LANES])
        return out

    def attn_body(c, carry):
        ms, ls, accs = carry
        rows = pl.ds(pl.multiple_of(c * kc, kc), kc)
        pair_cols = lambda h: slice((h // 2) * LANES, (h // 2 + 1) * LANES)

        def issue_logits(h):
            s_ref[h % 2] = _dg(q2_ref[h], kb_ref[0, rows, pair_cols(h)], 1, 1)

        issue_logits(0)
        new_m, new_l, new_acc = [], [], []
        for hp in range(N_HEADS // 2):
            vpair = vb_ref[0, rows, pair_cols(2 * hp)]
            pvs, alphas = [], []
            for h in (2 * hp, 2 * hp + 1):
                if h + 1 < N_HEADS:
                    issue_logits(h + 1)
                s = s_ref[h % 2] + lax.bitcast_convert_type(key_ref[c], F32)
                m_new = jnp.maximum(ms[h], jnp.max(fold(s, jnp.maximum), axis=1, keepdims=True))
                p = jnp.exp(s - _tile_lanes(m_new, kc))
                alpha = jnp.exp(ms[h] - m_new)
                new_m.append(m_new)
                new_l.append(alpha * ls[h] + fold(p, jnp.add))
                pvs.append(_dg(p.astype(BF16), vpair, 1, 0))
                alphas.append(alpha)
            new_acc.append(jnp.where(low_half, alphas[0], alphas[1]) * accs[hp]
                           + jnp.where(low_half, pvs[0], pvs[1]))
        return tuple(new_m), tuple(new_l), tuple(new_acc)

    zero = jnp.zeros((qb, LANES), F32)
    init = ((jnp.full((qb, LANES), NEG, F32),) * N_HEADS, (zero,) * N_HEADS, (zero,) * (N_HEADS // 2))
    _, ls, accs = lax.fori_loop(0, n_ch, attn_body, init)
    for hp in range(N_HEADS // 2):
        l0 = jnp.sum(ls[2 * hp], axis=1, keepdims=True)
        l1 = jnp.sum(ls[2 * hp + 1], axis=1, keepdims=True)
        o_ref[0, :, hp * LANES:(hp + 1) * LANES] = accs[hp] / jnp.where(low_half, l0, l1)


def _dsa_prompt(iq, ikw, q, kb, vb, qb, kc):
    b, s, _ = q.shape
    n_sel = min(TOPK_MAX, s // 4)
    full = lambda w: pl.BlockSpec((1, s, w), lambda bb, i: (bb, 0, 0), pipeline_mode=pl.Buffered(1))
    blk = lambda w: pl.BlockSpec((1, qb, w), lambda bb, i: (bb, i, 0))
    return pl.pallas_call(
        functools.partial(_dsa_prompt_kernel, n_sel=n_sel, qb=qb, kc=kc, s_len=s),
        grid=(b, s // qb),
        in_specs=[blk(ATTN_DIM), blk(LANES), blk(ATTN_DIM), full(LANES), full(ATTN_DIM), full(ATTN_DIM)],
        out_specs=blk(ATTN_DIM),
        out_shape=jax.ShapeDtypeStruct((b, s, ATTN_DIM), F32),
        scratch_shapes=[pltpu.VMEM((s // kc, qb, kc), I32),
                        pltpu.VMEM((s // kc, kc, 2 * LANES), BF16),
                        pltpu.VMEM((IDX_HEADS, qb, 2 * LANES), BF16),
                        pltpu.VMEM((IDX_HEADS, qb, LANES), F32),
                        pltpu.VMEM((N_HEADS, qb, LANES), BF16),
                        pltpu.VMEM((qb, LANES), I32),
                        pltpu.VMEM((2, qb, kc), F32)],
        compiler_params=_params(("parallel", "arbitrary")),
        name="dsa_prompt",
    )(iq, ikw, q, ikw, kb, vb)


def _sample_select_kernel(pt_ref, qh_ref, ql_ref, w_ref, new_ref, *rest, g, n_grp, n_sel, past, t):
    pages = rest[:g]
    o_ref, key_ref, cur_ref, cut_ref = rest[g:]
    j = pl.program_id(1)
    lane = lax.broadcasted_iota(I32, (1, LANES), 1)
    qh, ql, w = qh_ref[0], ql_ref[0], w_ref[0]

    def scores(ik):
        kh, kl = _split(ik)
        s = _dot3(qh, ql, kh, kl, 1, 1)
        tot = jnp.maximum(s, 0.0) * w
        return jnp.sum(tot.reshape(IDX_HEADS, t, LANES), axis=0)

    @pl.when(j < n_grp)
    def _():
        key_ref[j] = jnp.concatenate([_sort_key(scores(pages[r][0])) for r in range(g)], axis=1)

    @pl.when(j == n_grp)
    def _():
        tpos = lax.broadcasted_iota(I32, (t, 1), 0)
        cur_ref[...] = _sort_key(jnp.where(lane <= tpos, scores(new_ref[0]), -jnp.inf))

        def count_fn(pred, *thr):
            thr_b = [jnp.broadcast_to(x, (t, LANES)) for x in thr]

            def body(c, acc):
                kblk = key_ref[c]
                for r in range(g):
                    pos = (c * g + r) * LANES + lane
                    acc = acc + pred(kblk[:, r * LANES:(r + 1) * LANES], pos, *thr_b).astype(I32)
                return acc
            acc = lax.fori_loop(0, n_grp, body, jnp.zeros((t, LANES), I32))
            acc = acc + pred(cur_ref[...], past + lane, *thr_b).astype(I32)
            return jnp.sum(acc, axis=1, keepdims=True)

        tau = _select_threshold(count_fn, n_sel, t, max(1, (past + LANES - 1).bit_length()) + 1, cut_ref)
        wide_lane = lax.broadcasted_iota(I32, (1, g * LANES), 1)
        cut = cut_ref[...]

        def out_body(c, carry):
            kblk = key_ref[c]
            pos = c * g * LANES + wide_lane
            sel = (kblk > tau) | ((kblk == tau) & (pos <= _tile_lanes(cut, g * LANES)))
            o_ref[0, c] = jnp.where(sel, 0.0, NEG)
            return carry
        lax.fori_loop(0, n_grp, out_body, 0)
        kcur = cur_ref[...]
        pos = past + lane
        sel = ((kcur > tau) | ((kcur == tau) & (pos <= cut))) & (lane <= tpos)
        cur_bias = jnp.where(sel, 0.0, NEG)
        o_ref[0, n_grp] = jnp.concatenate([cur_bias] + [jnp.full((t, LANES), NEG, F32)] * (g - 1), axis=1)


def _page_specs(g, n_grp, layer_base, block_tail):
    nd = len(block_tail)

    def make(r):
        def index_map(b, j, pt):
            return (layer_base + pt[b, jnp.minimum(j, n_grp - 1) * g + r],) + (0,) * nd
        return pl.BlockSpec((1,) + block_tail, index_map)
    return [make(r) for r in range(g)]


def _sample_select(page_table, qh, ql, w, ik_new, idx_pages, layer_base, g, t):
    db, n_pages = page_table.shape
    n_grp = n_pages // g
    past = n_pages * PAGE
    n_sel = min(TOPK_MAX, (past + t) // 4)
    rows = IDX_HEADS * t
    per_b = lambda shape: pl.BlockSpec((1,) + shape, lambda b, j, pt: (b,) + (0,) * len(shape))
    kern = functools.partial(_sample_select_kernel, g=g, n_grp=n_grp, n_sel=n_sel, past=past, t=t)
    return pl.pallas_call(
        kern,
        grid_spec=pltpu.PrefetchScalarGridSpec(
            num_scalar_prefetch=1,
            grid=(db, n_grp + 1),
            in_specs=[per_b((rows, IDX_DIM)), per_b((rows, IDX_DIM)), per_b((rows, LANES)),
                      per_b((PAGE, IDX_DIM))] + _page_specs(g, n_grp, layer_base, (PAGE, IDX_DIM)),
            out_specs=per_b((n_grp + 1, t, g * LANES)),
            scratch_shapes=[pltpu.VMEM((n_grp, t, g * LANES), I32), pltpu.VMEM((t, LANES), I32),
                            pltpu.VMEM((t, LANES), I32)]),
        out_shape=jax.ShapeDtypeStruct((db, n_grp + 1, t, g * LANES), F32),
        compiler_params=_params(("parallel", "arbitrary")),
        name="sample_select",
    )(page_table, qh, ql, w, ik_new, *([idx_pages] * g))


def _sample_attend_kernel(pt_ref, q_ref, bias_ref, knew_ref, vnew_ref, *rest, g, n_grp, t):
    kpages, vpages = rest[:g], rest[g:2 * g]
    o_ref, m_ref, l_ref, acc_ref = rest[2 * g:]
    j = pl.program_id(1)
    rows = N_HEADS * t

    @pl.when(j == 0)
    def _():
        m_ref[...] = jnp.full(m_ref.shape, NEG, F32)
        l_ref[...] = jnp.zeros(l_ref.shape, F32)
        acc_ref[...] = jnp.zeros(acc_ref.shape, F32)

    def step(kmat, vmat, bias):
        n_sub = kmat.shape[0] // LANES
        s = _dg(q_ref[0], kmat.astype(BF16), 1, 1) + jnp.concatenate([bias] * N_HEADS, axis=0)
        m_old = m_ref[...]
        m_new = jnp.maximum(m_old, jnp.max(s, axis=1, keepdims=True))
        p = jnp.exp(s - _tile_lanes(m_new, kmat.shape[0]))
        alpha = jnp.exp(m_old - m_new)
        psum = p[:, :LANES]
        for r in range(1, n_sub):
            psum = psum + p[:, r * LANES:(r + 1) * LANES]
        l_ref[...] = alpha * l_ref[...] + psum
        m_ref[...] = m_new
        acc_ref[...] = (_tile_lanes(alpha, ATTN_DIM) * acc_ref[...]
                        + _dg(p.astype(BF16), vmat.astype(BF16), 1, 0))

    @pl.when(j < n_grp)
    def _():
        step(jnp.concatenate([kp[0] for kp in kpages], axis=0),
             jnp.concatenate([vp[0] for vp in vpages], axis=0), bias_ref[0, 0])

    @pl.when(j == n_grp)
    def _():
        step(knew_ref[0], vnew_ref[0], bias_ref[0, 0][:, :LANES])
        full = acc_ref[...] / jnp.sum(l_ref[...], axis=1, keepdims=True)
        lane = lax.broadcasted_iota(I32, (1, ATTN_DIM), 1)
        out = jnp.zeros((t, ATTN_DIM), F32)
        for h in range(N_HEADS):
            in_head = (lane >= h * HEAD_DIM) & (lane < (h + 1) * HEAD_DIM)
            out = out + jnp.where(in_head, full[h * t:(h + 1) * t, :], 0.0)
        o_ref[0] = out


def _sample_attend(page_table, qrows, bias, k_new, v_new, k_pages, v_pages, layer_base, g, t):
    db, n_pages = page_table.shape
    n_grp = n_pages // g
    rows = N_HEADS * t
    per_b = lambda shape: pl.BlockSpec((1,) + shape, lambda b, j, pt: (b,) + (0,) * len(shape))
    kern = functools.partial(_sample_attend_kernel, g=g, n_grp=n_grp, t=t)
    return pl.pallas_call(
        kern,
        grid_spec=pltpu.PrefetchScalarGridSpec(
            num_scalar_prefetch=1,
            grid=(db, n_grp + 1),
            in_specs=[per_b((rows, ATTN_DIM)),
                      pl.BlockSpec((1, 1, t, g * LANES), lambda b, j, pt: (b, j, 0, 0)),
                      per_b((PAGE, ATTN_DIM)), per_b((PAGE, ATTN_DIM))]
                     + _page_specs(g, n_grp, layer_base, (PAGE, ATTN_DIM))
                     + _page_specs(g, n_grp, layer_base, (PAGE, ATTN_DIM)),
            out_specs=per_b((t, ATTN_DIM)),
            scratch_shapes=[pltpu.VMEM((rows, LANES), F32), pltpu.VMEM((rows, LANES), F32),
                            pltpu.VMEM((rows, ATTN_DIM), F32)]),
        out_shape=jax.ShapeDtypeStruct((db, t, ATTN_DIM), F32),
        compiler_params=_params(("parallel", "arbitrary")),
        name="sample_attend",
    )(page_table, qrows, bias, k_new, v_new, *([k_pages] * g), *([v_pages] * g))


def _mix_kernel(x_ref, ag_ref, gb_ref, at_ref, g1_ref, sc2_ref, sh2_ref, gf_ref,
                wat_ref, wo_ref, wqh_ref, wql_ref, k1h_ref, k1l_ref, k2h_ref, k2l_ref,
                x1_ref, h2_ref, s1_ref, s2_ref):
    bb = _dg(at_ref[0].astype(BF16), wat_ref[...], 1, 0)
    mixed = ag_ref[0] + gb_ref[0] * bb
    x1 = x_ref[0] + g1_ref[0] * _dg(mixed.astype(BF16), wo_ref[...], 1, 0)
    x1_ref[0] = x1
    h2 = x1 * lax.rsqrt(jnp.mean(x1 * x1, axis=-1, keepdims=True) + EPS) * gf_ref[...]
    h2 = h2 * (1.0 + sc2_ref[0]) + sh2_ref[0]
    h2_ref[0] = h2.astype(BF16)
    hh, hl = _split(h2)
    qp = _dot3(hh, hl, wqh_ref[...], wql_ref[...], 1, 0)
    for h in range(PEER_HEADS):
        ah, al = _split(qp[:, h * PEER_QDIM:h * PEER_QDIM + PEER_HALF])
        bh, bl = _split(qp[:, h * PEER_QDIM + PEER_HALF:(h + 1) * PEER_QDIM])
        s1_ref[h] = _dot3(k1h_ref[...], k1l_ref[...], ah, al, 1, 1)
        s2_ref[h] = _dot3(k2h_ref[...], k2l_ref[...], bh, bl, 1, 1)


def _mix(x, ag, gb, attn, gate1, scale2, shift2, g_ffn, wat_b, wo_b, wq_h, wq_l, k1, k2, tm):
    bx, s, d = x.shape
    n = bx * s
    row = lambda w: pl.BlockSpec((1, tm, w), lambda b, i: (b, i, 0))
    tr = pl.BlockSpec((PEER_HEADS, PEER_KEYS, tm), lambda b, i: (0, 0, b * (s // tm) + i))
    k1h, k1l = _split(k1)
    k2h, k2l = _split(k2)
    return pl.pallas_call(
        _mix_kernel,
        grid=(bx, s // tm),
        in_specs=[row(d), row(d), row(d), row(ATTN_DIM), _mod_spec(gate1, tm), _mod_spec(scale2, tm),
                  _mod_spec(shift2, tm), _const_spec((1, d)), _const_spec(wat_b.shape), _const_spec(wo_b.shape),
                  _const_spec(wq_h.shape), _const_spec(wq_l.shape)] + [_const_spec((PEER_KEYS, PEER_HALF))] * 4,
        out_specs=[row(d), row(d), tr, tr],
        out_shape=[jax.ShapeDtypeStruct((bx, s, d), F32), jax.ShapeDtypeStruct((bx, s, d), BF16),
                   jax.ShapeDtypeStruct((PEER_HEADS, PEER_KEYS, n), F32),
                   jax.ShapeDtypeStruct((PEER_HEADS, PEER_KEYS, n), F32)],
        compiler_params=_params(("parallel", "parallel")),
        name="mix_peer_query",
    )(x, ag, gb, attn, gate1, scale2, shift2, g_ffn.reshape(1, d), wat_b, wo_b, wq_h, wq_l, k1h, k1l, k2h, k2l)


A_PER_STEP = 8
N_CAND = 80


def _top16(s):
    rows = lax.broadcasted_iota(I32, s.shape, 0).astype(F32)
    rank = jnp.full(s.shape, NO_RANK, F32)
    vals = []
    for it in range(PEER_TOPK):
        m = jnp.max(s, axis=0, keepdims=True)
        first = jnp.min(jnp.where(s == m, rows, float(PEER_KEYS)), axis=0, keepdims=True)
        pick = rows == first
        rank = jnp.where(pick, float(it), rank)
        s = jnp.where(pick, -jnp.inf, s)
        vals.append(m)
    return jnp.concatenate(vals, axis=0), rank


def _peer_select_kernel(s1_ref, s2_ref, e1_ref, jr_ref, e2_ref, r2_ref):
    tn = s1_ref.shape[-1]
    r = lax.broadcasted_iota(I32, (N_CAND, 1), 0)
    mid_i = lax.shift_right_arithmetic(r - 16, 3) + 1
    mid_j = (r - 16) & 7
    flat = jnp.where(r < 16, r, jnp.where(r < 72, mid_i * 16 + mid_j, (r - 64) * 16)).astype(F32)
    limit = jnp.where(mid_i == 1, 8, jnp.where(mid_i == 2, 5, jnp.where(mid_i == 3, 4, jnp.where(mid_i == 4, 3, 2))))
    cell_ok = (r < 16) | (r >= 72) | (mid_j < limit)
    row8 = lax.broadcasted_iota(I32, (8, 1), 0)

    for h in range(PEER_HEADS):
        s1, s2 = s1_ref[h], s2_ref[h]
        v1, rank1 = _top16(s1)
        v2, rank2 = _top16(s2)
        groups = [v1[0:1] + v2[0:16]] + [v1[i:i + 1] + v2[0:8] for i in range(1, 8)] + [v1[8:16] + v2[0:1]]
        cand = jnp.where(cell_ok, jnp.concatenate(groups, axis=0), -jnp.inf)
        chosen = jnp.zeros((N_CAND, tn), F32)
        m0 = v1[0:1] + v2[0:1]
        zsum = jnp.zeros((1, tn), F32)
        for it in range(PEER_TOPK):
            m = jnp.max(cand, axis=0, keepdims=True)
            first = jnp.min(jnp.where(cand == m, flat, 4096.0), axis=0, keepdims=True)
            pick = flat == first
            chosen = jnp.where(pick, 1.0, chosen)
            cand = jnp.where(pick, -jnp.inf, cand)
            zsum = zsum + jnp.exp(m - m0)
        j_low = jnp.zeros((8, tn), F32)
        j_low = jnp.where(row8 == 0, jnp.sum(chosen[0:16], axis=0, keepdims=True), j_low)
        for i in range(1, 8):
            j_low = jnp.where(row8 == i, jnp.sum(chosen[8 + 8 * i:16 + 8 * i], axis=0, keepdims=True), j_low)
        j_high = chosen[72:80]
        jr = jnp.zeros((PEER_KEYS, tn), F32)
        for i in range(PEER_TOPK):
            ji = j_low[i:i + 1] if i < 8 else j_high[i - 8:i - 7]
            jr = jnp.where(rank1 == float(i), ji, jr)
        jr_ref[h] = jr
        e1_ref[h] = jnp.where(rank1 < NO_RANK, jnp.exp(s1 - v1[0:1]), 0.0) / zsum
        e2_ref[h] = jnp.where(rank2 < NO_RANK, jnp.exp(s2 - v2[0:1]), 0.0).astype(BF16)
        r2_ref[h] = rank2.astype(BF16)


def _peer_select(s1t, s2t, tn):
    _, _, n = s1t.shape
    spec = pl.BlockSpec((PEER_HEADS, PEER_KEYS, tn), lambda i: (0, 0, i))
    shape = jax.ShapeDtypeStruct(s1t.shape, F32)
    shape_b = jax.ShapeDtypeStruct(s1t.shape, BF16)
    return pl.pallas_call(
        _peer_select_kernel,
        grid=(n // tn,),
        in_specs=[spec, spec],
        out_specs=[spec] * 4,
        out_shape=[shape, shape, shape_b, shape_b],
        compiler_params=_params(("parallel",)),
        name="peer_select",
    )(s1t, s2t)


def _peer_expert_kernel(h2_ref, x1_ref, g2_ref, gfin_ref, u_ref, vt_ref, e1_ref, jr_ref, e2_ref, r2_ref,
                        y_ref, acc_ref, pt_ref, *, a_per, final_norm):
    j = pl.program_id(2)

    @pl.when(j == 0)
    def _():
        acc_ref[...] = jnp.zeros(acc_ref.shape, F32)

    zt = _dg(u_ref[...], h2_ref[0], 1, 1)
    gz = (0.5 * zt * (1.0 + lax.erf(zt * (2.0 ** -0.5)))).astype(BF16)

    for al in range(a_per):
        wsum = jnp.zeros((PEER_KEYS, zt.shape[1]), BF16)
        for h in range(PEER_HEADS):
            jr = jr_ref[h, al:al + 1, :].astype(BF16)
            e1 = e1_ref[h, al:al + 1, :].astype(BF16)
            wsum = wsum + jnp.where(r2_ref[h] < jr, e2_ref[h] * e1, jnp.zeros((), BF16))
        rows = slice(al * PEER_KEYS, (al + 1) * PEER_KEYS)
        pt_ref[rows, :] = wsum * gz[rows, :]
    acc_ref[...] += _dg(vt_ref[...], pt_ref[...], 1, 0)

    @pl.when(j == pl.num_programs(2) - 1)
    def _():
        x2 = x1_ref[0] + g2_ref[0] * acc_ref[...].T
        if final_norm:
            x2 = x2 * lax.rsqrt(jnp.mean(x2 * x2, axis=-1, keepdims=True) + EPS) * gfin_ref[...]
        y_ref[0] = x2


def _peer_expert(h2b, x1, gate2, g_final, u_b, vt_b, e1, jr, e2, r2, tn, a_per, final_norm):
    bx, s, d = x1.shape
    n_exp = u_b.shape[0]
    ec = a_per * PEER_KEYS
    tiles = s // tn
    row = lambda w: pl.BlockSpec((1, tn, w), lambda b, i, j: (b, i, 0))
    mod = (pl.BlockSpec((1, 1, d), lambda b, i, j: (b, 0, 0)) if gate2.shape[1] == 1
           else pl.BlockSpec((1, tn, d), lambda b, i, j: (b, i, 0)))
    per_a = pl.BlockSpec((PEER_HEADS, a_per, tn), lambda b, i, j: (0, j, b * tiles + i))
    per_b = pl.BlockSpec((PEER_HEADS, PEER_KEYS, tn), lambda b, i, j: (0, 0, b * tiles + i))
    return pl.pallas_call(
        functools.partial(_peer_expert_kernel, a_per=a_per, final_norm=final_norm),
        grid=(bx, tiles, n_exp // ec),
        in_specs=[row(d), row(d), mod, pl.BlockSpec((1, d), lambda b, i, j: (0, 0)),
                  pl.BlockSpec((ec, d), lambda b, i, j: (j, 0)),
                  pl.BlockSpec((d, ec), lambda b, i, j: (0, j)),
                  per_a, per_a, per_b, per_b],
        out_specs=row(d),
        out_shape=jax.ShapeDtypeStruct((bx, s, d), F32),
        scratch_shapes=[pltpu.VMEM((d, tn), F32), pltpu.VMEM((ec, tn), BF16)],
        compiler_params=_params(("parallel", "parallel", "arbitrary")),
        name="peer_expert",
    )(h2b, x1, gate2, g_final.reshape(1, d), u_b, vt_b, e1, jr, e2, r2)


def _rope_tables(pos):
    rot = HEAD_DIM // 4
    half = rot // 2
    inv_freq = ROPE_THETA ** (-(jnp.arange(half, dtype=F32) * (2.0 / rot)))
    ang = pos.astype(F32)[:, None] * inv_freq[None, :]
    cos, sin = jnp.cos(ang), jnp.sin(ang)
    t = pos.shape[0]
    ones = jnp.ones((t, HEAD_DIM - rot), F32)
    zeros = jnp.zeros((t, HEAD_DIM - rot), F32)
    z8 = jnp.zeros((t, half), F32)
    c = jnp.concatenate([cos, cos, ones], axis=1)
    sa = jnp.concatenate([-sin, z8, zeros], axis=1)
    sb = jnp.concatenate([z8, sin, zeros], axis=1)
    dup = lambda a: jnp.concatenate([a, a], axis=1)
    return dup(c), dup(sa), dup(sb)


def _pack_w_in(w_in):
    d = w_in.shape[0]
    cols = C_IKW - C_GLU
    pad = jnp.zeros((d, LANES - IDX_DIM - IDX_HEADS), w_in.dtype)
    return jnp.concatenate([w_in[:, :cols + IDX_DIM + IDX_HEADS], pad, w_in[:, cols + IDX_DIM + IDX_HEADS:]], axis=1)


def _pick_tile(n, pref):
    t = min(n, pref)
    while n % t:
        t //= 2
    return t


def kernel(x_prompt, x_sample, c_prompt, c_sample, cache_k, cache_v, cache_idx_k, state_conv, page_table, w_ada, b_ada, g_norm_mix, w_in, w_dw, b_dw, ln_conv_g, ln_conv_b, w_conv_out, w_attn_out, w_o, g_norm_ffn, w_peer_q, peer_k1, peer_k2, peer_u, peer_v, g_norm_final):
    b, s, d = x_prompt.shape
    db, t, _ = x_sample.shape
    depth = w_ada.shape[0]
    n_phys = cache_k.shape[1]
    n_pages = page_table.shape[1]
    past = n_pages * PAGE
    ns = db * t
    g_pages = math.gcd(n_pages, 8)

    tabs_p = _rope_tables(jnp.arange(s))
    tabs_s = _rope_tables(jnp.tile(past + jnp.arange(t), db))
    idx_pages = cache_idx_k.reshape(depth * n_phys, PAGE, IDX_DIM)
    k_pages = cache_k.astype(BF16).reshape(depth * n_phys, PAGE, ATTN_DIM)
    v_pages = cache_v.astype(BF16).reshape(depth * n_phys, PAGE, ATTN_DIM)
    c_all = jnp.concatenate([c_prompt, c_sample], axis=0)
    c_rows = -(-c_all.shape[0] // 8) * 8
    c_all = jnp.pad(c_all, ((0, c_rows - c_all.shape[0]), (0, 0)))

    tm_p, tm_s = _pick_tile(s, 256), _pick_tile(ns, 256)
    tc_p = _pick_tile(s, 256)
    qb = _pick_tile(s, 256)
    kc = _pick_tile(s, 512)
    tn_sel_p, tn_sel_s = _pick_tile(b * s, 256), _pick_tile(ns, 256)
    tn_exp_p, tn_exp_s = _pick_tile(s, 512), _pick_tile(ns, 512)

    xp, xs = x_prompt, x_sample.reshape(1, ns, d)
    outs = [[] for _ in range(8)]
    for l in range(depth):
        mod = _ada(c_all, w_ada[l], b_ada[l])
        mod_p = [m.reshape(b, 1, d) for m in jnp.split(mod[:b], 6, axis=-1)]
        mod_s = [jnp.repeat(m, t, axis=0).reshape(1, ns, d) for m in jnp.split(mod[b:b + db], 6, axis=-1)]
        w_pack = _pack_w_in(w_in[l])
        wh = w_pack.astype(BF16)
        wl = (w_pack[:, C_IQ:C_GA] - wh[:, C_IQ:C_GA].astype(F32)).astype(BF16)
        wco_b, wat_b, wo_b = w_conv_out[l].astype(BF16), w_attn_out[l].astype(BF16), w_o[l].astype(BF16)
        wq_h, wq_l = _split(w_peer_q[l])
        u_b = peer_u[l].astype(BF16)
        vt_b = peer_v[l].astype(BF16).T
        conv_w = (w_dw[l], b_dw[l], ln_conv_g[l], ln_conv_b[l], wco_b)

        u, q, k, v, kb, vb, iq, ikw, ga, gb = _inproj(xp, mod_p[1], mod_p[0], g_norm_mix[l], wh, wl, *tabs_p, tm_p)
        hist0 = jnp.zeros((b, HALO, CONV_DIM), F32)
        ag = _conv_branch(u, hist0, ga, *conv_w, tc_p)
        attn = _dsa_prompt(iq, ikw, q, kb, vb, qb, kc)
        x1, h2b, s1t, s2t = _mix(xp, ag, gb, attn, mod_p[2], mod_p[4], mod_p[3], g_norm_ffn[l],
                                 wat_b, wo_b, wq_h, wq_l, peer_k1[l], peer_k2[l], tm_p)
        sel = _peer_select(s1t, s2t, tn_sel_p)
        last = l == depth - 1
        xp = _peer_expert(h2b, x1, mod_p[5], g_norm_final, u_b, vt_b, *sel, tn_exp_p, A_PER_STEP, last)
        outs[0].append(k.reshape(b, s // PAGE, PAGE, N_HEADS, HEAD_DIM))
        outs[1].append(v.reshape(b, s // PAGE, PAGE, N_HEADS, HEAD_DIM))
        outs[2].append(ikw[:, :, :IDX_DIM].reshape(b, s // PAGE, PAGE, IDX_DIM))
        outs[3].append(jnp.concatenate([hist0[:, HALO - (CONV_WIDTH - 1):], u], axis=1)[:, -(CONV_WIDTH - 1):])

        u, q, k, v, kb, vb, iq, ikw, ga, gb = _inproj(xs, mod_s[1], mod_s[0], g_norm_mix[l], wh, wl, *tabs_s, tm_s)
        u3 = u.reshape(db, t, CONV_DIM)
        hist = jnp.pad(state_conv[l], ((0, 0), (HALO - (CONV_WIDTH - 1), 0), (0, 0)))
        ag = _conv_branch(u3, hist, ga.reshape(db, t, d), *conv_w, t).reshape(1, ns, d)
        iq_ht = iq.reshape(db, t, IDX_HEADS, IDX_DIM).transpose(0, 2, 1, 3).reshape(db, IDX_HEADS * t, IDX_DIM)
        iqh, iql = _split(iq_ht)
        iw = ikw.reshape(db, t, LANES)[:, :, IDX_DIM:IDX_DIM + IDX_HEADS]
        w_rows = jnp.broadcast_to(iw.transpose(0, 2, 1).reshape(db, IDX_HEADS * t, 1), (db, IDX_HEADS * t, LANES))
        pad_rows = lambda a: jnp.pad(a.reshape(db, t, -1), ((0, 0), (0, PAGE - t), (0, 0)))
        bias = _sample_select(page_table, iqh, iql, w_rows, pad_rows(ikw[..., :IDX_DIM]), idx_pages,
                              l * n_phys, g_pages, t)
        q4 = q.reshape(db, t, N_HEADS, HEAD_DIM) * (HEAD_DIM ** -0.5)
        eye = jnp.eye(N_HEADS, dtype=F32)
        qrows = jnp.einsum('bthd,hg->bhtgd', q4, eye).reshape(db, N_HEADS * t, ATTN_DIM).astype(BF16)
        attn = _sample_attend(page_table, qrows, bias, pad_rows(k), pad_rows(v), k_pages, v_pages,
                              l * n_phys, g_pages, t).reshape(1, ns, ATTN_DIM)
        x1, h2b, s1t, s2t = _mix(xs, ag, gb, attn, mod_s[2], mod_s[4], mod_s[3], g_norm_ffn[l],
                                 wat_b, wo_b, wq_h, wq_l, peer_k1[l], peer_k2[l], tm_s)
        sel = _peer_select(s1t, s2t, tn_sel_s)
        xs = _peer_expert(h2b, x1, mod_s[5], g_norm_final, u_b, vt_b, *sel, tn_exp_s, A_PER_STEP, last)
        outs[4].append(k.reshape(db, t, N_HEADS, HEAD_DIM))
        outs[5].append(v.reshape(db, t, N_HEADS, HEAD_DIM))
        outs[6].append(ikw.reshape(db, t, LANES)[:, :, :IDX_DIM])
        outs[7].append(jnp.concatenate([state_conv[l], u3], axis=1)[:, -(CONV_WIDTH - 1):])

    stacked = [jnp.stack(o) for o in outs]
    return (xp, xs.reshape(db, t, d), *stacked)
```

```python
import functools
import math

import jax
import jax.numpy as jnp
from jax import lax
from jax.experimental import pallas as pl
from jax.experimental.pallas import tpu as pltpu

F32 = jnp.float32
BF16 = jnp.bfloat16
I32 = jnp.int32

N_HEADS = 8
HEAD_DIM = 64
ATTN_DIM = N_HEADS * HEAD_DIM
ROPE_THETA = 500000.0
IDX_HEADS = 8
IDX_DIM = 64
TOPK_MAX = 256
CONV_DIM = 512
CONV_WIDTH = 31
HALO = 32
PEER_HEADS = 8
PEER_KEYS = 128
PEER_HALF = 128
PEER_QDIM = 256
PEER_TOPK = 16
PAGE = 128
EPS = 1e-6
LANES = 128
INT_MIN = -(2 ** 31)
INT_MAX = 2 ** 31 - 1
NEG = -1e30
NO_RANK = 999.0
COUNT_ROWS = 128
VMEM_LIMIT = 56 * 1024 * 1024

C_GLU, C_Q, C_K, C_V, C_IQ, C_IKW, C_GA, C_GB, C_END = 0, 1024, 1536, 2048, 2560, 3072, 3200, 4224, 5248


def _dg(a, b, ca, cb):
    return lax.dot_general(a, b, (((ca,), (cb,)), ((), ())), preferred_element_type=F32)


def _split(x):
    hi = x.astype(BF16)
    lo = (x - hi.astype(F32)).astype(BF16)
    return hi, lo


def _dot3(ah, al, bh, bl, ca, cb):
    return _dg(ah, bh, ca, cb) + _dg(al, bh, ca, cb) + _dg(ah, bl, ca, cb)


def _sigmoid(x):
    return 1.0 / (1.0 + jnp.exp(-x))


def _const_spec(shape):
    n = len(shape)
    return pl.BlockSpec(shape, lambda *_: (0,) * n, pipeline_mode=pl.Buffered(1))


def _params(sem):
    return pltpu.CompilerParams(dimension_semantics=sem, vmem_limit_bytes=VMEM_LIMIT)


def _sort_key(x):
    bits = lax.bitcast_convert_type(x, I32)
    key = jnp.where(bits < 0, bits ^ jnp.int32(INT_MAX), bits)
    return jnp.where(key == -1, 0, key)


def _ada_kernel(c_ref, w_ref, b_ref, o_ref):
    c = c_ref[...]
    s = c * _sigmoid(c)
    sh, sl = _split(s)
    wh, wl = _split(w_ref[...])
    o_ref[...] = _dot3(sh, sl, wh, wl, 1, 0) + b_ref[...]


def _ada(c, w_ada, b_ada):
    rows, d = c.shape
    n = w_ada.shape[1]
    tn = 1536
    return pl.pallas_call(
        _ada_kernel,
        grid=(n // tn,),
        in_specs=[pl.BlockSpec((rows, d), lambda j: (0, 0)),
                  pl.BlockSpec((d, tn), lambda j: (0, j)),
                  pl.BlockSpec((1, tn), lambda j: (0, j))],
        out_specs=pl.BlockSpec((rows, tn), lambda j: (0, j)),
        out_shape=jax.ShapeDtypeStruct((rows, n), F32),
        compiler_params=_params(("parallel",)),
        name="adaln",
    )(c, w_ada, b_ada.reshape(1, n))


def _tile_lanes(t, width):
    return jnp.concatenate([t] * (width // LANES), axis=1) if width > LANES else t


def _rope(x, c, sa, sb):
    w = x.shape[1]
    return (x * _tile_lanes(c, w) + pltpu.roll(x, w - 8, 1) * _tile_lanes(sa, w)
            + pltpu.roll(x, 8, 1) * _tile_lanes(sb, w))


def _inproj_kernel(x_ref, sc_ref, sh_ref, g_ref, wh_ref, wl_ref, cs_ref, sa_ref, sb_ref,
                   u_ref, q_ref, k_ref, v_ref, kb_ref, vb_ref, iq_ref, ikw_ref, ga_ref, gb_ref):
    x = x_ref[0]
    h = x * lax.rsqrt(jnp.mean(x * x, axis=-1, keepdims=True) + EPS) * g_ref[...]
    h = h * (1.0 + sc_ref[0]) + sh_ref[0]
    hh, hl = _split(h)
    c, sa, sb = cs_ref[...], sa_ref[...], sb_ref[...]

    glu = _dg(hh, wh_ref[:, C_GLU:C_Q], 1, 0)
    u_ref[0] = glu[:, :CONV_DIM] * _sigmoid(glu[:, CONV_DIM:])

    qk = _dg(hh, wh_ref[:, C_Q:C_V], 1, 0)
    q_ref[0] = _rope(qk[:, :ATTN_DIM], c, sa, sb)
    k = _rope(qk[:, ATTN_DIM:], c, sa, sb)
    k_ref[0] = k
    kb_ref[0] = k.astype(BF16)
    v = _dg(hh, wh_ref[:, C_V:C_IQ], 1, 0)
    v_ref[0] = v
    vb_ref[0] = v.astype(BF16)

    wih = wh_ref[:, C_IQ:C_GA]
    idx = _dg(hh, wih, 1, 0) + _dg(hl, wih, 1, 0) + _dg(hh, wl_ref[...], 1, 0)
    iq_ref[0] = _rope(idx[:, :ATTN_DIM], c, sa, sb)
    lane = lax.broadcasted_iota(I32, (1, LANES), 1)
    is_ik = lane < IDX_DIM
    iw_scale = float((IDX_HEADS * IDX_DIM) ** -0.5)
    ikw_ref[0] = _rope(idx[:, ATTN_DIM:], jnp.where(is_ik, c, iw_scale),
                       jnp.where(is_ik, sa, 0.0), jnp.where(is_ik, sb, 0.0))

    ga_ref[0] = _sigmoid(_dg(hh, wh_ref[:, C_GA:C_GB], 1, 0))
    gb_ref[0] = _sigmoid(_dg(hh, wh_ref[:, C_GB:C_END], 1, 0))


def _mod_spec(arr, tm):
    d = arr.shape[-1]
    if arr.shape[1] == 1:
        return pl.BlockSpec((1, 1, d), lambda b, i: (b, 0, 0))
    return pl.BlockSpec((1, tm, d), lambda b, i: (b, i, 0))


def _inproj(x, scale1, shift1, g_mix, wh, wl, cs, sa, sb, tm):
    bx, s, d = x.shape
    row = lambda w: pl.BlockSpec((1, tm, w), lambda b, i: (b, i, 0))
    tab = pl.BlockSpec((tm, LANES), lambda b, i: (i, 0))
    out_w = (CONV_DIM, ATTN_DIM, ATTN_DIM, ATTN_DIM, ATTN_DIM, ATTN_DIM, ATTN_DIM, LANES, d, d)
    out_t = (F32, F32, F32, F32, BF16, BF16, F32, F32, F32, F32)
    return pl.pallas_call(
        _inproj_kernel,
        grid=(bx, s // tm),
        in_specs=[row(d), _mod_spec(scale1, tm), _mod_spec(shift1, tm), _const_spec((1, d)),
                  _const_spec(wh.shape), _const_spec(wl.shape), tab, tab, tab],
        out_specs=[row(w) for w in out_w],
        out_shape=[jax.ShapeDtypeStruct((bx, s, w), t) for w, t in zip(out_w, out_t)],
        compiler_params=_params(("parallel", "parallel")),
        name="inproj",
    )(x, scale1, shift1, g_mix.reshape(1, d), wh, wl, cs, sa, sb)


def _conv_kernel(u_ref, hist_ref, ga_ref, wdw_ref, bdw_ref, lg_ref, lb_ref, wo_ref, o_ref, win_ref, *, t):
    @pl.when(pl.program_id(1) == 0)
    def _():
        win_ref[0:HALO, :] = hist_ref[0]

    win_ref[HALO:HALO + t, :] = u_ref[0]
    off = HALO - (CONV_WIDTH - 1)
    acc = jnp.zeros((t, CONV_DIM), F32)
    for j in range(CONV_WIDTH):
        acc = acc + win_ref[off + j:off + j + t, :] * wdw_ref[j:j + 1, :]
    dw = acc + bdw_ref[...]
    mu = jnp.mean(dw, axis=-1, keepdims=True)
    var = jnp.mean(jnp.square(dw - mu), axis=-1, keepdims=True)
    y = (dw - mu) * lax.rsqrt(var + EPS) * lg_ref[...] + lb_ref[...]
    y = y * _sigmoid(y)
    o_ref[0] = ga_ref[0] * _dg(y.astype(BF16), wo_ref[...], 1, 0)
    tail = win_ref[t:t + HALO, :]
    win_ref[0:HALO, :] = tail


def _conv_branch(u, hist, ga, w_dw, b_dw, ln_g, ln_b, w_out_b, t):
    bx, s, _ = u.shape
    d = ga.shape[-1]
    wdw = jnp.pad(w_dw, ((0, HALO - CONV_WIDTH), (0, 0)))
    vec = lambda a: a.reshape(1, CONV_DIM)
    return pl.pallas_call(
        functools.partial(_conv_kernel, t=t),
        grid=(bx, s // t),
        in_specs=[pl.BlockSpec((1, t, CONV_DIM), lambda b, i: (b, i, 0)),
                  pl.BlockSpec((1, HALO, CONV_DIM), lambda b, i: (b, 0, 0)),
                  pl.BlockSpec((1, t, d), lambda b, i: (b, i, 0)),
                  _const_spec((HALO, CONV_DIM)), _const_spec((1, CONV_DIM)), _const_spec((1, CONV_DIM)),
                  _const_spec((1, CONV_DIM)), _const_spec((CONV_DIM, d))],
        out_specs=pl.BlockSpec((1, t, d), lambda b, i: (b, i, 0)),
        out_shape=jax.ShapeDtypeStruct((bx, s, d), F32),
        scratch_shapes=[pltpu.VMEM((t + HALO, CONV_DIM), F32)],
        compiler_params=_params(("parallel", "arbitrary")),
        name="conv_branch",
    )(u, hist, ga, wdw, vec(b_dw), vec(ln_g), vec(ln_b), w_out_b)


def _select_threshold(count_fn, n_sel, rows, pos_bits, cut_ref):
    def bit_body(it, lo):
        cand = lo + lax.shift_left(jnp.int32(1), 31 - it)
        cnt = count_fn(lambda k, p, c: k >= c, cand)
        return jnp.where(cnt >= n_sel, cand, lo)

    tau = lax.fori_loop(0, 32, bit_body, jnp.full((rows, 1), INT_MIN, I32))
    surplus = count_fn(lambda k, p, c: k >= c, tau) - n_sel
    cut_ref[...] = jnp.full(cut_ref.shape, INT_MAX, I32)

    @pl.when(jnp.max(surplus.astype(F32)) > 0.0)
    def _():
        need = n_sel - count_fn(lambda k, p, c: k > c, tau)

        def pos_body(it, x):
            cand = x + lax.shift_left(jnp.int32(1), pos_bits - 1 - it)
            cnt = count_fn(lambda k, p, c, d: (k == c) & (p < d), tau, cand)
            return jnp.where(cnt < need, cand, x)

        cut = lax.fori_loop(0, pos_bits, pos_body, jnp.zeros((rows, 1), I32))
        cut_ref[...] = jnp.broadcast_to(cut, cut_ref.shape)

    return tau


def _hi_lo_f32(x):
    hi = x.astype(BF16).astype(F32)
    return hi, x - hi


def _dsa_prompt_kernel(iq_ref, ikwq_ref, q_ref, ikw_ref, kb_ref, vb_ref, o_ref,
                       key_ref, ik3_ref, a3_ref, w_ref, q2_ref, cut_ref, s_ref,
                       *, n_sel, qb, kc, s_len):
    i = pl.program_id(1)
    n_ch = (i * qb + qb + kc - 1) // kc
    n_sub = kc // LANES
    lane = lax.broadcasted_iota(I32, (1, LANES), 1)
    low_half = lane < HEAD_DIM

    @pl.when(i == 0)
    def _():
        def body(c, carry):
            blk = ikw_ref[0, pl.ds(pl.multiple_of(c * kc, kc), kc), :]
            hi, lo = _hi_lo_f32(blk)
            first = jnp.where(low_half, hi, pltpu.roll(lo, HEAD_DIM, 1))
            second = jnp.where(low_half, hi, 0.0)
            ik3_ref[c] = jnp.concatenate([first, second], axis=1).astype(BF16)
            return carry
        lax.fori_loop(0, s_len // kc, body, 0)

    iq = iq_ref[0]
    ikwq = ikwq_ref[0]
    q = q_ref[0] * (HEAD_DIM ** -0.5)
    for h in range(IDX_HEADS):
        pair = slice((h // 2) * LANES, (h // 2 + 1) * LANES)
        x = iq[:, pair] if h % 2 == 0 else pltpu.roll(iq[:, pair], HEAD_DIM, 1)
        hi, lo = _hi_lo_f32(x)
        first = jnp.where(low_half, hi, pltpu.roll(hi, HEAD_DIM, 1))
        second = jnp.where(low_half, lo, 0.0)
        a3_ref[h] = jnp.concatenate([first, second], axis=1).astype(BF16)
        w_ref[h] = jnp.broadcast_to(ikwq[:, IDX_DIM + h:IDX_DIM + h + 1], (qb, LANES))
        keep = low_half if h % 2 == 0 else jnp.logical_not(low_half)
        q2_ref[h] = jnp.where(keep, q[:, pair], 0.0).astype(BF16)

    qpos = i * qb + lax.broadcasted_iota(I32, (qb, 1), 0)
    sub_lane = lax.broadcasted_iota(I32, (1, kc), 1)

    def score_body(c, carry):
        kk = ik3_ref[c]
        tot = jnp.zeros((qb, kc), F32)
        for h in range(IDX_HEADS):
            s = _dg(a3_ref[h], kk, 1, 1)
            tot = tot + jnp.maximum(s, 0.0) * _tile_lanes(w_ref[h], kc)
        causal = (c * kc + sub_lane) <= qpos
        key_ref[c] = _sort_key(jnp.where(causal, tot, -jnp.inf))
        return carry
    lax.fori_loop(0, n_ch, score_body, 0)

    def count_fn(pred, *thr):
        counts = []
        for r0 in range(0, qb, COUNT_ROWS):
            rs = slice(r0, min(r0 + COUNT_ROWS, qb))
            nrows = rs.stop - rs.start
            thr_b = [jnp.broadcast_to(x[rs], (nrows, LANES)) for x in thr]

            def body(c, acc):
                kblk = key_ref[c, rs, :]
                for j in range(n_sub):
                    pos = c * kc + j * LANES + lane
                    acc = acc + pred(kblk[:, j * LANES:(j + 1) * LANES], pos, *thr_b).astype(I32)
                return acc
            acc = lax.fori_loop(0, n_ch, body, jnp.zeros((nrows, LANES), I32))
            counts.append(jnp.sum(acc, axis=1, keepdims=True))
        return jnp.concatenate(counts, axis=0) if len(counts) > 1 else counts[0]

    tau = _select_threshold(count_fn, n_sel, qb, max(1, (s_len - 1).bit_length()) + 1, cut_ref)

    for r0 in range(0, qb, COUNT_ROWS):
        rs = slice(r0, min(r0 + COUNT_ROWS, qb))
        shape = (rs.stop - rs.start, LANES)
        tau_b = jnp.broadcast_to(tau[rs], shape)
        qpos_b = jnp.broadcast_to(qpos[rs], shape)
        cut_b = cut_ref[rs, :]

        def bias_body(c, carry):
            kblk = key_ref[c, rs, :]
            for j in range(n_sub):
                k = kblk[:, j * LANES:(j + 1) * LANES]
                pos = c * kc + j * LANES + lane
                sel = ((k > tau_b) | ((k == tau_b) & (pos <= cut_b))) & (pos <= qpos_b)
                key_ref[c, rs, j * LANES:(j + 1) * LANES] = lax.bitcast_convert_type(jnp.where(sel, 0.0, NEG), I32)
            return carry
        lax.fori_loop(0, n_ch, bias_body, 0)

    def fold(x, op):
        out = x[:, :LANES]
        for j in range(1, n_sub):
            out = op(out, x[:, j * LANES:(j + 1) * LANES])
        return out

    def attn_body(c, carry):
        ms, ls, accs = carry
        rows = pl.ds(pl.multiple_of(c * kc, kc), kc)
        pair_cols = lambda h: slice((h // 2) * LANES, (h // 2 + 1) * LANES)

        def issue_logits(h):
            s_ref[h % 2] = _dg(q2_ref[h], kb_ref[0, rows, pair_cols(h)], 1, 1)

        issue_logits(0)
        new_m, new_l, new_acc = [], [], []
        for hp in range(N_HEADS // 2):
            vpair = vb_ref[0, rows, pair_cols(2 * hp)]
            pvs, alphas = [], []
            for h in (2 * hp, 2 * hp + 1):
                if h + 1 < N_HEADS:
                    issue_logits(h + 1)
                s = s_ref[h % 2] + lax.bitcast_convert_type(key_ref[c], F32)
                m_new = jnp.maximum(ms[h], jnp.max(fold(s, jnp.maximum), axis=1, keepdims=True))
                p = jnp.exp(s - _tile_lanes(m_new, kc))
                alpha = jnp.exp(ms[h] - m_new)
                new_m.append(m_new)
                new_l.append(alpha * ls[h] + fold(p, jnp.add))
                pvs.append(_dg(p.astype(BF16), vpair, 1, 0))
                alphas.append(alpha)
            new_acc.append(jnp.where(low_half, alphas[0], alphas[1]) * accs[hp]
                           + jnp.where(low_half, pvs[0], pvs[1]))
        return tuple(new_m), tuple(new_l), tuple(new_acc)

    zero = jnp.zeros((qb, LANES), F32)
    init = ((jnp.full((qb, LANES), NEG, F32),) * N_HEADS, (zero,) * N_HEADS, (zero,) * (N_HEADS // 2))
    _, ls, accs = lax.fori_loop(0, n_ch, attn_body, init)
    for hp in range(N_HEADS // 2):
        l0 = jnp.sum(ls[2 * hp], axis=1, keepdims=True)
        l1 = jnp.sum(ls[2 * hp + 1], axis=1, keepdims=True)
        o_ref[0, :, hp * LANES:(hp + 1) * LANES] = accs[hp] / jnp.where(low_half, l0, l1)


def _dsa_prompt(iq, ikw, q, kb, vb, qb, kc):
    b, s, _ = q.shape
    n_sel = min(TOPK_MAX, s // 4)
    full = lambda w: pl.BlockSpec((1, s, w), lambda bb, i: (bb, 0, 0), pipeline_mode=pl.Buffered(1))
    blk = lambda w: pl.BlockSpec((1, qb, w), lambda bb, i: (bb, i, 0))
    return pl.pallas_call(
        functools.partial(_dsa_prompt_kernel, n_sel=n_sel, qb=qb, kc=kc, s_len=s),
        grid=(b, s // qb),
        in_specs=[blk(ATTN_DIM), blk(LANES), blk(ATTN_DIM), full(LANES), full(ATTN_DIM), full(ATTN_DIM)],
        out_specs=blk(ATTN_DIM),
        out_shape=jax.ShapeDtypeStruct((b, s, ATTN_DIM), F32),
        scratch_shapes=[pltpu.VMEM((s // kc, qb, kc), I32),
                        pltpu.VMEM((s // kc, kc, 2 * LANES), BF16),
                        pltpu.VMEM((IDX_HEADS, qb, 2 * LANES), BF16),
                        pltpu.VMEM((IDX_HEADS, qb, LANES), F32),
                        pltpu.VMEM((N_HEADS, qb, LANES), BF16),
                        pltpu.VMEM((qb, LANES), I32),
                        pltpu.VMEM((2, qb, kc), F32)],
        compiler_params=_params(("parallel", "arbitrary")),
        name="dsa_prompt",
    )(iq, ikw, q, ikw, kb, vb)


def _sample_select_kernel(pt_ref, qh_ref, ql_ref, w_ref, new_ref, *rest, g, n_grp, n_sel, past, t):
    pages = rest[:g]
    o_ref, key_ref, cur_ref, cut_ref = rest[g:]
    j = pl.program_id(1)
    lane = lax.broadcasted_iota(I32, (1, LANES), 1)
    qh, ql, w = qh_ref[0], ql_ref[0], w_ref[0]

    def scores(ik):
        kh, kl = _split(ik)
        s = _dot3(qh, ql, kh, kl, 1, 1)
        tot = jnp.maximum(s, 0.0) * w
        return jnp.sum(tot.reshape(IDX_HEADS, t, LANES), axis=0)

    @pl.when(j < n_grp)
    def _():
        key_ref[j] = jnp.concatenate([_sort_key(scores(pages[r][0])) for r in range(g)], axis=1)

    @pl.when(j == n_grp)
    def _():
        tpos = lax.broadcasted_iota(I32, (t, 1), 0)
        cur_ref[...] = _sort_key(jnp.where(lane <= tpos, scores(new_ref[0]), -jnp.inf))

        def count_fn(pred, *thr):
            thr_b = [jnp.broadcast_to(x, (t, LANES)) for x in thr]

            def body(c, acc):
                kblk = key_ref[c]
                for r in range(g):
                    pos = (c * g + r) * LANES + lane
                    acc = acc + pred(kblk[:, r * LANES:(r + 1) * LANES], pos, *thr_b).astype(I32)
                return acc
            acc = lax.fori_loop(0, n_grp, body, jnp.zeros((t, LANES), I32))
            acc = acc + pred(cur_ref[...], past + lane, *thr_b).astype(I32)
            return jnp.sum(acc, axis=1, keepdims=True)

        tau = _select_threshold(count_fn, n_sel, t, max(1, (past + LANES - 1).bit_length()) + 1, cut_ref)
        wide_lane = lax.broadcasted_iota(I32, (1, g * LANES), 1)
        cut = cut_ref[...]

        def out_body(c, carry):
            kblk = key_ref[c]
            pos = c * g * LANES + wide_lane
            sel = (kblk > tau) | ((kblk == tau) & (pos <= _tile_lanes(cut, g * LANES)))
            o_ref[0, c] = jnp.where(sel, 0.0, NEG)
            return carry
        lax.fori_loop(0, n_grp, out_body, 0)
        kcur = cur_ref[...]
        pos = past + lane
        sel = ((kcur > tau) | ((kcur == tau) & (pos <= cut))) & (lane <= tpos)
        cur_bias = jnp.where(sel, 0.0, NEG)
        o_ref[0, n_grp] = jnp.concatenate([cur_bias] + [jnp.full((t, LANES), NEG, F32)] * (g - 1), axis=1)


def _page_specs(g, n_grp, layer_base, block_tail, squeeze=False):
    nd = len(block_tail)

    def make(r):
        def index_map(b, j, pt):
            return (layer_base + pt[b, jnp.minimum(j, n_grp - 1) * g + r],) + (0,) * nd
        return pl.BlockSpec(((None if squeeze else 1),) + block_tail, index_map)
    return [make(r) for r in range(g)]


def _sample_select(page_table, qh, ql, w, ik_new, idx_pages, layer_base, g, t):
    db, n_pages = page_table.shape
    n_grp = n_pages // g
    past = n_pages * PAGE
    n_sel = min(TOPK_MAX, (past + t) // 4)
    rows = IDX_HEADS * t
    per_b = lambda shape: pl.BlockSpec((1,) + shape, lambda b, j, pt: (b,) + (0,) * len(shape))
    kern = functools.partial(_sample_select_kernel, g=g, n_grp=n_grp, n_sel=n_sel, past=past, t=t)
    return pl.pallas_call(
        kern,
        grid_spec=pltpu.PrefetchScalarGridSpec(
            num_scalar_prefetch=1,
            grid=(db, n_grp + 1),
            in_specs=[per_b((rows, IDX_DIM)), per_b((rows, IDX_DIM)), per_b((rows, LANES)),
                      per_b((PAGE, IDX_DIM))] + _page_specs(g, n_grp, layer_base, (PAGE, IDX_DIM)),
            out_specs=per_b((n_grp + 1, t, g * LANES)),
            scratch_shapes=[pltpu.VMEM((n_grp, t, g * LANES), I32), pltpu.VMEM((t, LANES), I32),
                            pltpu.VMEM((t, LANES), I32)]),
        out_shape=jax.ShapeDtypeStruct((db, n_grp + 1, t, g * LANES), F32),
        compiler_params=_params(("parallel", "arbitrary")),
        name="sample_select",
    )(page_table, qh, ql, w, ik_new, *([idx_pages] * g))


def _sample_attend_kernel(pt_ref, q_ref, bias_ref, knew_ref, vnew_ref, *rest, g, n_grp, t):
    kpages, vpages = rest[:g], rest[g:2 * g]
    o_ref, m_ref, l_ref, acc_ref = rest[2 * g:]
    j = pl.program_id(1)
    rows = N_HEADS * t

    @pl.when(j == 0)
    def _():
        m_ref[...] = jnp.full(m_ref.shape, NEG, F32)
        l_ref[...] = jnp.zeros(l_ref.shape, F32)
        acc_ref[...] = jnp.zeros(acc_ref.shape, F32)

    def step(kmat, vmat, bias):
        n_sub = kmat.shape[0] // LANES
        s = _dg(q_ref[0], kmat.astype(BF16), 1, 1) + jnp.concatenate([bias] * N_HEADS, axis=0)
        m_old = m_ref[...]
        m_new = jnp.maximum(m_old, jnp.max(s, axis=1, keepdims=True))
        p = jnp.exp(s - _tile_lanes(m_new, kmat.shape[0]))
        alpha = jnp.exp(m_old - m_new)
        psum = p[:, :LANES]
        for r in range(1, n_sub):
            psum = psum + p[:, r * LANES:(r + 1) * LANES]
        l_ref[...] = alpha * l_ref[...] + psum
        m_ref[...] = m_new
        acc_ref[...] = (_tile_lanes(alpha, ATTN_DIM) * acc_ref[...]
                        + _dg(p.astype(BF16), vmat.astype(BF16), 1, 0))

    @pl.when(j < n_grp)
    def _():
        def page_matrix(p_ref):
            head = lambda h: p_ref[pl.ds(h, PAGE, stride=N_HEADS), :]
            pairs = [jnp.concatenate([head(h), head(h + 1)], axis=1).astype(BF16) for h in range(0, N_HEADS, 2)]
            return jnp.concatenate(pairs, axis=1)

        step(jnp.concatenate([page_matrix(kp) for kp in kpages], axis=0),
             jnp.concatenate([page_matrix(vp) for vp in vpages], axis=0), bias_ref[0, 0])

    @pl.when(j == n_grp)
    def _():
        step(knew_ref[0], vnew_ref[0], bias_ref[0, 0][:, :LANES])
        full = acc_ref[...] / jnp.sum(l_ref[...], axis=1, keepdims=True)
        lane = lax.broadcasted_iota(I32, (1, ATTN_DIM), 1)
        out = jnp.zeros((t, ATTN_DIM), F32)
        for h in range(N_HEADS):
            in_head = (lane >= h * HEAD_DIM) & (lane < (h + 1) * HEAD_DIM)
            out = out + jnp.where(in_head, full[h * t:(h + 1) * t, :], 0.0)
        o_ref[0] = out


def _sample_attend(page_table, qrows, bias, k_new, v_new, k_pages, v_pages, layer_base, g, t):
    db, n_pages = page_table.shape
    n_grp = n_pages // g
    rows = N_HEADS * t
    per_b = lambda shape: pl.BlockSpec((1,) + shape, lambda b, j, pt: (b,) + (0,) * len(shape))
    kern = functools.partial(_sample_attend_kernel, g=g, n_grp=n_grp, t=t)
    return pl.pallas_call(
        kern,
        grid_spec=pltpu.PrefetchScalarGridSpec(
            num_scalar_prefetch=1,
            grid=(db, n_grp + 1),
            in_specs=[per_b((rows, ATTN_DIM)),
                      pl.BlockSpec((1, 1, t, g * LANES), lambda b, j, pt: (b, j, 0, 0)),
                      per_b((PAGE, ATTN_DIM)), per_b((PAGE, ATTN_DIM))]
                     + _page_specs(g, n_grp, layer_base, (PAGE * N_HEADS, HEAD_DIM), squeeze=True)
                     + _page_specs(g, n_grp, layer_base, (PAGE * N_HEADS, HEAD_DIM), squeeze=True),
            out_specs=per_b((t, ATTN_DIM)),
            scratch_shapes=[pltpu.VMEM((rows, LANES), F32), pltpu.VMEM((rows, LANES), F32),
                            pltpu.VMEM((rows, ATTN_DIM), F32)]),
        out_shape=jax.ShapeDtypeStruct((db, t, ATTN_DIM), F32),
        compiler_params=_params(("parallel", "arbitrary")),
        name="sample_attend",
    )(page_table, qrows, bias, k_new, v_new, *([k_pages] * g), *([v_pages] * g))


def _mix_kernel(x_ref, ag_ref, gb_ref, at_ref, g1_ref, sc2_ref, sh2_ref, gf_ref,
                wat_ref, wo_ref, wqh_ref, wql_ref, k1h_ref, k1l_ref, k2h_ref, k2l_ref,
                x1_ref, h2_ref, s1_ref, s2_ref):
    bb = _dg(at_ref[0].astype(BF16), wat_ref[...], 1, 0)
    mixed = ag_ref[0] + gb_ref[0] * bb
    x1 = x_ref[0] + g1_ref[0] * _dg(mixed.astype(BF16), wo_ref[...], 1, 0)
    x1_ref[0] = x1
    h2 = x1 * lax.rsqrt(jnp.mean(x1 * x1, axis=-1, keepdims=True) + EPS) * gf_ref[...]
    h2 = h2 * (1.0 + sc2_ref[0]) + sh2_ref[0]
    h2_ref[0] = h2.astype(BF16)
    hh, hl = _split(h2)
    qp = _dot3(hh, hl, wqh_ref[...], wql_ref[...], 1, 0)
    for h in range(PEER_HEADS):
        ah, al = _split(qp[:, h * PEER_QDIM:h * PEER_QDIM + PEER_HALF])
        bh, bl = _split(qp[:, h * PEER_QDIM + PEER_HALF:(h + 1) * PEER_QDIM])
        s1_ref[h] = _dot3(k1h_ref[...], k1l_ref[...], ah, al, 1, 1)
        s2_ref[h] = _dot3(k2h_ref[...], k2l_ref[...], bh, bl, 1, 1)


def _mix(x, ag, gb, attn, gate1, scale2, shift2, g_ffn, wat_b, wo_b, wq_h, wq_l, k1, k2, tm):
    bx, s, d = x.shape
    n = bx * s
    row = lambda w: pl.BlockSpec((1, tm, w), lambda b, i: (b, i, 0))
    tr = pl.BlockSpec((PEER_HEADS, PEER_KEYS, tm), lambda b, i: (0, 0, b * (s // tm) + i))
    k1h, k1l = _split(k1)
    k2h, k2l = _split(k2)
    return pl.pallas_call(
        _mix_kernel,
        grid=(bx, s // tm),
        in_specs=[row(d), row(d), row(d), row(ATTN_DIM), _mod_spec(gate1, tm), _mod_spec(scale2, tm),
                  _mod_spec(shift2, tm), _const_spec((1, d)), _const_spec(wat_b.shape), _const_spec(wo_b.shape),
                  _const_spec(wq_h.shape), _const_spec(wq_l.shape)] + [_const_spec((PEER_KEYS, PEER_HALF))] * 4,
        out_specs=[row(d), row(d), tr, tr],
        out_shape=[jax.ShapeDtypeStruct((bx, s, d), F32), jax.ShapeDtypeStruct((bx, s, d), BF16),
                   jax.ShapeDtypeStruct((PEER_HEADS, PEER_KEYS, n), F32),
                   jax.ShapeDtypeStruct((PEER_HEADS, PEER_KEYS, n), F32)],
        compiler_params=_params(("parallel", "parallel")),
        name="mix_peer_query",
    )(x, ag, gb, attn, gate1, scale2, shift2, g_ffn.reshape(1, d), wat_b, wo_b, wq_h, wq_l, k1h, k1l, k2h, k2l)


A_PER_STEP = 8
PEER_COLS = 256
N_CAND = 80


def _top16(s, break_ties):
    rows = lax.broadcasted_iota(I32, s.shape, 0).astype(F32)
    rank = jnp.full(s.shape, NO_RANK, F32)
    vals = []
    for it in range(PEER_TOPK):
        m = jnp.max(s, axis=0, keepdims=True)
        pick = s == m
        if break_ties:
            pick = rows == jnp.min(jnp.where(pick, rows, float(PEER_KEYS)), axis=0, keepdims=True)
        rank = jnp.where(pick, float(it), rank)
        s = jnp.where(pick, -jnp.inf, s)
        vals.append(m)
    return jnp.concatenate(vals, axis=0), rank


def _peer_select_kernel(s1_ref, s2_ref, e1_ref, jr_ref, e2_ref, r2_ref):
    tn = s1_ref.shape[-1]
    r = lax.broadcasted_iota(I32, (N_CAND, 1), 0)
    mid_i = lax.shift_right_arithmetic(r - 16, 3) + 1
    mid_j = (r - 16) & 7
    flat = jnp.where(r < 16, r, jnp.where(r < 72, mid_i * 16 + mid_j, (r - 64) * 16)).astype(F32)
    limit = jnp.where(mid_i == 1, 8, jnp.where(mid_i == 2, 5, jnp.where(mid_i == 3, 4, jnp.where(mid_i == 4, 3, 2))))
    cell_ok = (r < 16) | (r >= 72) | (mid_j < limit)
    row8 = lax.broadcasted_iota(I32, (8, 1), 0)

    def select_head(h, break_ties):
        s1, s2 = s1_ref[h], s2_ref[h]
        v1, rank1 = _top16(s1, break_ties)
        v2, rank2 = _top16(s2, break_ties)
        groups = [v1[0:1] + v2[0:16]] + [v1[i:i + 1] + v2[0:8] for i in range(1, 8)] + [v1[8:16] + v2[0:1]]
        cand = jnp.where(cell_ok, jnp.concatenate(groups, axis=0), -jnp.inf)
        chosen = jnp.zeros((N_CAND, tn), F32)
        m0 = v1[0:1] + v2[0:1]
        zsum = jnp.zeros((1, tn), F32)
        for it in range(PEER_TOPK):
            m = jnp.max(cand, axis=0, keepdims=True)
            first = jnp.min(jnp.where(cand == m, flat, 4096.0), axis=0, keepdims=True)
            pick = flat == first
            chosen = jnp.where(pick, 1.0, chosen)
            cand = jnp.where(pick, -jnp.inf, cand)
            zsum = zsum + jnp.exp(m - m0)
        j_low = jnp.zeros((8, tn), F32)
        j_low = jnp.where(row8 == 0, jnp.sum(chosen[0:16], axis=0, keepdims=True), j_low)
        for i in range(1, 8):
            j_low = jnp.where(row8 == i, jnp.sum(chosen[8 + 8 * i:16 + 8 * i], axis=0, keepdims=True), j_low)
        j_high = chosen[72:80]
        jr = jnp.zeros((PEER_KEYS, tn), F32)
        for i in range(PEER_TOPK):
            ji = j_low[i:i + 1] if i < 8 else j_high[i - 8:i - 7]
            jr = jnp.where(rank1 == float(i), ji, jr)
        jr_ref[h] = jr
        e1_ref[h] = jnp.where(rank1 < NO_RANK, jnp.exp(s1 - v1[0:1]), 0.0) * (0.5 / zsum)
        e2_ref[h] = jnp.where(rank2 < NO_RANK, jnp.exp(s2 - v2[0:1]), 0.0).astype(BF16)
        r2_ref[h] = rank2.astype(BF16)
        ranked = (rank1 < NO_RANK).astype(F32) + (rank2 < NO_RANK).astype(F32)
        return jnp.max(jnp.sum(ranked, axis=0, keepdims=True))

    def head_body(h, carry):
        n_ranked = select_head(h, False)

        @pl.when(n_ranked > 2.0 * PEER_TOPK)
        def _():
            select_head(h, True)
        return carry
    lax.fori_loop(0, PEER_HEADS, head_body, 0)


def _peer_select(s1t, s2t, tn):
    _, _, n = s1t.shape
    spec = pl.BlockSpec((PEER_HEADS, PEER_KEYS, tn), lambda i: (0, 0, i))
    shape = jax.ShapeDtypeStruct(s1t.shape, F32)
    shape_b = jax.ShapeDtypeStruct(s1t.shape, BF16)
    return pl.pallas_call(
        _peer_select_kernel,
        grid=(n // tn,),
        in_specs=[spec, spec],
        out_specs=[spec] * 4,
        out_shape=[shape, shape, shape_b, shape_b],
        compiler_params=_params(("parallel",)),
        name="peer_select",
    )(s1t, s2t)


def _peer_expert_kernel(h2_ref, x1_ref, g2_ref, gfin_ref, u_ref, vt_ref, e1_ref, jr_ref, e2_ref, r2_ref,
                        y_ref, acc_ref, w_ref, z_ref, p_ref, *, a_per, final_norm):
    j = pl.program_id(2)
    tn = h2_ref.shape[1]
    cols = min(PEER_COLS, tn)
    tiles = [slice(c, c + cols) for c in range(0, tn, cols)]

    @pl.when(j == 0)
    def _():
        acc_ref[...] = jnp.zeros(acc_ref.shape, F32)

    def build_weights(cs):
        for al in range(a_per):
            wsum = jnp.zeros((PEER_KEYS, cols), BF16)
            for h in range(PEER_HEADS):
                jr = jr_ref[h, al:al + 1, cs].astype(BF16)
                e1 = e1_ref[h, al:al + 1, cs].astype(BF16)
                wsum = wsum + jnp.where(r2_ref[h, :, cs] < jr, e2_ref[h, :, cs] * e1, jnp.zeros((), BF16))
            w_ref[al * PEER_KEYS:(al + 1) * PEER_KEYS, cs] = wsum

    def expert_logits(cs):
        z_ref[:, cs] = _dg(u_ref[...], h2_ref[0, cs, :], 1, 1)

    def gate(cs):
        zt = z_ref[:, cs]
        gz = (zt * (1.0 + lax.erf(zt * (2.0 ** -0.5)))).astype(BF16)
        p_ref[:, cs] = w_ref[:, cs] * gz

    def fold(cs):
        acc_ref[:, cs] += _dg(vt_ref[...], p_ref[:, cs], 1, 0)

    expert_logits(tiles[0])
    for k, cs in enumerate(tiles):
        build_weights(cs)
        gate(cs)
        if k + 1 < len(tiles):
            expert_logits(tiles[k + 1])
        fold(cs)

    @pl.when(j == pl.num_programs(2) - 1)
    def _():
        x2 = x1_ref[0] + g2_ref[0] * acc_ref[...].T
        if final_norm:
            x2 = x2 * lax.rsqrt(jnp.mean(x2 * x2, axis=-1, keepdims=True) + EPS) * gfin_ref[...]
        y_ref[0] = x2


def _peer_expert(h2b, x1, gate2, g_final, u_b, vt_b, e1, jr, e2, r2, tn, a_per, final_norm):
    bx, s, d = x1.shape
    n_exp = u_b.shape[0]
    ec = a_per * PEER_KEYS
    tiles = s // tn
    row = lambda w: pl.BlockSpec((1, tn, w), lambda b, i, j: (b, i, 0))
    mod = (pl.BlockSpec((1, 1, d), lambda b, i, j: (b, 0, 0)) if gate2.shape[1] == 1
           else pl.BlockSpec((1, tn, d), lambda b, i, j: (b, i, 0)))
    per_a = pl.BlockSpec((PEER_HEADS, a_per, tn), lambda b, i, j: (0, j, b * tiles + i))
    per_b = pl.BlockSpec((PEER_HEADS, PEER_KEYS, tn), lambda b, i, j: (0, 0, b * tiles + i))
    return pl.pallas_call(
        functools.partial(_peer_expert_kernel, a_per=a_per, final_norm=final_norm),
        grid=(bx, tiles, n_exp // ec),
        in_specs=[row(d), row(d), mod, pl.BlockSpec((1, d), lambda b, i, j: (0, 0)),
                  pl.BlockSpec((ec, d), lambda b, i, j: (j, 0)),
                  pl.BlockSpec((d, ec), lambda b, i, j: (0, j)),
                  per_a, per_a, per_b, per_b],
        out_specs=row(d),
        out_shape=jax.ShapeDtypeStruct((bx, s, d), F32),
        scratch_shapes=[pltpu.VMEM((d, tn), F32), pltpu.VMEM((ec, tn), BF16), pltpu.VMEM((ec, tn), F32),
                        pltpu.VMEM((ec, tn), BF16)],
        compiler_params=_params(("parallel", "parallel", "arbitrary")),
        name="peer_expert",
    )(h2b, x1, gate2, g_final.reshape(1, d), u_b, vt_b, e1, jr, e2, r2)


def _rope_tables(pos):
    rot = HEAD_DIM // 4
    half = rot // 2
    inv_freq = ROPE_THETA ** (-(jnp.arange(half, dtype=F32) * (2.0 / rot)))
    ang = pos.astype(F32)[:, None] * inv_freq[None, :]
    cos, sin = jnp.cos(ang), jnp.sin(ang)
    t = pos.shape[0]
    ones = jnp.ones((t, HEAD_DIM - rot), F32)
    zeros = jnp.zeros((t, HEAD_DIM - rot), F32)
    z8 = jnp.zeros((t, half), F32)
    c = jnp.concatenate([cos, cos, ones], axis=1)
    sa = jnp.concatenate([-sin, z8, zeros], axis=1)
    sb = jnp.concatenate([z8, sin, zeros], axis=1)
    dup = lambda a: jnp.concatenate([a, a], axis=1)
    return dup(c), dup(sa), dup(sb)


def _pack_w_in(w_in):
    d = w_in.shape[0]
    cols = C_IKW - C_GLU
    pad = jnp.zeros((d, LANES - IDX_DIM - IDX_HEADS), w_in.dtype)
    return jnp.concatenate([w_in[:, :cols + IDX_DIM + IDX_HEADS], pad, w_in[:, cols + IDX_DIM + IDX_HEADS:]], axis=1)


def _pick_tile(n, pref):
    t = min(n, pref)
    while n % t:
        t //= 2
    return t


def kernel(x_prompt, x_sample, c_prompt, c_sample, cache_k, cache_v, cache_idx_k, state_conv, page_table, w_ada, b_ada, g_norm_mix, w_in, w_dw, b_dw, ln_conv_g, ln_conv_b, w_conv_out, w_attn_out, w_o, g_norm_ffn, w_peer_q, peer_k1, peer_k2, peer_u, peer_v, g_norm_final):
    b, s, d = x_prompt.shape
    db, t, _ = x_sample.shape
    depth = w_ada.shape[0]
    n_phys = cache_k.shape[1]
    n_pages = page_table.shape[1]
    past = n_pages * PAGE
    ns = db * t
    g_pages = math.gcd(n_pages, 8)

    tabs_p = _rope_tables(jnp.arange(s))
    tabs_s = _rope_tables(jnp.tile(past + jnp.arange(t), db))
    idx_pages = cache_idx_k.reshape(depth * n_phys, PAGE, IDX_DIM)
    k_pages = cache_k.reshape(depth * n_phys, PAGE * N_HEADS, HEAD_DIM)
    v_pages = cache_v.reshape(depth * n_phys, PAGE * N_HEADS, HEAD_DIM)
    c_all = jnp.concatenate([c_prompt, c_sample], axis=0)
    c_rows = -(-c_all.shape[0] // 8) * 8
    c_all = jnp.pad(c_all, ((0, c_rows - c_all.shape[0]), (0, 0)))

    tm_p, tm_s = _pick_tile(s, 256), _pick_tile(ns, 256)
    tc_p = _pick_tile(s, 256)
    qb = _pick_tile(s, 256)
    kc = _pick_tile(s, 512)
    tn_sel_p, tn_sel_s = _pick_tile(b * s, 256), _pick_tile(ns, 256)
    tn_exp_p, tn_exp_s = _pick_tile(s, 512), _pick_tile(ns, 512)

    xp, xs = x_prompt, x_sample.reshape(1, ns, d)
    outs = [[] for _ in range(8)]
    for l in range(depth):
        mod = _ada(c_all, w_ada[l], b_ada[l])
        mod_p = [m.reshape(b, 1, d) for m in jnp.split(mod[:b], 6, axis=-1)]
        mod_s = [jnp.repeat(m, t, axis=0).reshape(1, ns, d) for m in jnp.split(mod[b:b + db], 6, axis=-1)]
        w_pack = _pack_w_in(w_in[l])
        wh = w_pack.astype(BF16)
        wl = (w_pack[:, C_IQ:C_GA] - wh[:, C_IQ:C_GA].astype(F32)).astype(BF16)
        wco_b, wat_b, wo_b = w_conv_out[l].astype(BF16), w_attn_out[l].astype(BF16), w_o[l].astype(BF16)
        wq_h, wq_l = _split(w_peer_q[l])
        u_b = peer_u[l].astype(BF16)
        vt_b = peer_v[l].astype(BF16).T
        conv_w = (w_dw[l], b_dw[l], ln_conv_g[l], ln_conv_b[l], wco_b)

        u, q, k, v, kb, vb, iq, ikw, ga, gb = _inproj(xp, mod_p[1], mod_p[0], g_norm_mix[l], wh, wl, *tabs_p, tm_p)
        hist0 = jnp.zeros((b, HALO, CONV_DIM), F32)
        ag = _conv_branch(u, hist0, ga, *conv_w, tc_p)
        attn = _dsa_prompt(iq, ikw, q, kb, vb, qb, kc)
        x1, h2b, s1t, s2t = _mix(xp, ag, gb, attn, mod_p[2], mod_p[4], mod_p[3], g_norm_ffn[l],
                                 wat_b, wo_b, wq_h, wq_l, peer_k1[l], peer_k2[l], tm_p)
        sel = _peer_select(s1t, s2t, tn_sel_p)
        last = l == depth - 1
        xp = _peer_expert(h2b, x1, mod_p[5], g_norm_final, u_b, vt_b, *sel, tn_exp_p, A_PER_STEP, last)
        outs[0].append(k.reshape(b, s // PAGE, PAGE, N_HEADS, HEAD_DIM))
        outs[1].append(v.reshape(b, s // PAGE, PAGE, N_HEADS, HEAD_DIM))
        outs[2].append(ikw[:, :, :IDX_DIM].reshape(b, s // PAGE, PAGE, IDX_DIM))
        outs[3].append(jnp.concatenate([hist0[:, HALO - (CONV_WIDTH - 1):], u], axis=1)[:, -(CONV_WIDTH - 1):])

        u, q, k, v, kb, vb, iq, ikw, ga, gb = _inproj(xs, mod_s[1], mod_s[0], g_norm_mix[l], wh, wl, *tabs_s, tm_s)
        u3 = u.reshape(db, t, CONV_DIM)
        hist = jnp.pad(state_conv[l], ((0, 0), (HALO - (CONV_WIDTH - 1), 0), (0, 0)))
        ag = _conv_branch(u3, hist, ga.reshape(db, t, d), *conv_w, t).reshape(1, ns, d)
        iq_ht = iq.reshape(db, t, IDX_HEADS, IDX_DIM).transpose(0, 2, 1, 3).reshape(db, IDX_HEADS * t, IDX_DIM)
        iqh, iql = _split(iq_ht)
        iw = ikw.reshape(db, t, LANES)[:, :, IDX_DIM:IDX_DIM + IDX_HEADS]
        w_rows = jnp.broadcast_to(iw.transpose(0, 2, 1).reshape(db, IDX_HEADS * t, 1), (db, IDX_HEADS * t, LANES))
        pad_rows = lambda a: jnp.pad(a.reshape(db, t, -1), ((0, 0), (0, PAGE - t), (0, 0)))
        bias = _sample_select(page_table, iqh, iql, w_rows, pad_rows(ikw[..., :IDX_DIM]), idx_pages,
                              l * n_phys, g_pages, t)
        q4 = q.reshape(db, t, N_HEADS, HEAD_DIM) * (HEAD_DIM ** -0.5)
        eye = jnp.eye(N_HEADS, dtype=F32)
        qrows = jnp.einsum('bthd,hg->bhtgd', q4, eye).reshape(db, N_HEADS * t, ATTN_DIM).astype(BF16)
        attn = _sample_attend(page_table, qrows, bias, pad_rows(k), pad_rows(v), k_pages, v_pages,
                              l * n_phys, g_pages, t).reshape(1, ns, ATTN_DIM)
        x1, h2b, s1t, s2t = _mix(xs, ag, gb, attn, mod_s[2], mod_s[4], mod_s[3], g_norm_ffn[l],
                                 wat_b, wo_b, wq_h, wq_l, peer_k1[l], peer_k2[l], tm_s)
        sel = _peer_select(s1t, s2t, tn_sel_s)
        xs = _peer_expert(h2b, x1, mod_s[5], g_norm_final, u_b, vt_b, *sel, tn_exp_s, A_PER_STEP, last)
        outs[4].append(k.reshape(db, t, N_HEADS, HEAD_DIM))
        outs[5].append(v.reshape(db, t, N_HEADS, HEAD_DIM))
        outs[6].append(ikw.reshape(db, t, LANES)[:, :, :IDX_DIM])
        outs[7].append(jnp.concatenate([state_conv[l], u3], axis=1)[:, -(CONV_WIDTH - 1):])

    stacked = [jnp.stack(o) for o in outs]
    return (xp, xs.reshape(db, t, d), *stacked)
```

```python
import functools
import math

import jax
import jax.numpy as jnp
from jax import lax
from jax.experimental import pallas as pl
from jax.experimental.pallas import tpu as pltpu

F32 = jnp.float32
BF16 = jnp.bfloat16
I32 = jnp.int32

N_HEADS = 8
HEAD_DIM = 64
ATTN_DIM = N_HEADS * HEAD_DIM
ROPE_THETA = 500000.0
IDX_HEADS = 8
IDX_DIM = 64
TOPK_MAX = 256
CONV_DIM = 512
CONV_WIDTH = 31
HALO = 32
PEER_HEADS = 8
PEER_KEYS = 128
PEER_HALF = 128
PEER_QDIM = 256
PEER_TOPK = 16
PAGE = 128
EPS = 1e-6
LANES = 128
INT_MIN = -(2 ** 31)
INT_MAX = 2 ** 31 - 1
NEG = -1e30
NO_RANK = 999.0
COUNT_ROWS = 128
VMEM_LIMIT = 56 * 1024 * 1024

C_GLU, C_Q, C_K, C_V, C_IQ, C_IKW, C_GA, C_GB, C_END = 0, 1024, 1536, 2048, 2560, 3072, 3200, 4224, 5248


def _dg(a, b, ca, cb):
    return lax.dot_general(a, b, (((ca,), (cb,)), ((), ())), preferred_element_type=F32)


def _split(x):
    hi = x.astype(BF16)
    lo = (x - hi.astype(F32)).astype(BF16)
    return hi, lo


def _dot3(ah, al, bh, bl, ca, cb):
    return _dg(ah, bh, ca, cb) + _dg(al, bh, ca, cb) + _dg(ah, bl, ca, cb)


def _sigmoid(x):
    return 1.0 / (1.0 + jnp.exp(-x))


def _const_spec(shape):
    n = len(shape)
    return pl.BlockSpec(shape, lambda *_: (0,) * n, pipeline_mode=pl.Buffered(1))


def _params(sem):
    return pltpu.CompilerParams(dimension_semantics=sem, vmem_limit_bytes=VMEM_LIMIT)


def _sort_key(x):
    bits = lax.bitcast_convert_type(x, I32)
    key = jnp.where(bits < 0, bits ^ jnp.int32(INT_MAX), bits)
    return jnp.where(key == -1, 0, key)


def _ada_kernel(c_ref, w_ref, b_ref, o_ref):
    c = c_ref[...]
    s = c * _sigmoid(c)
    sh, sl = _split(s)
    wh, wl = _split(w_ref[...])
    o_ref[...] = _dot3(sh, sl, wh, wl, 1, 0) + b_ref[...]


def _ada(c, w_ada, b_ada):
    rows, d = c.shape
    n = w_ada.shape[1]
    tn = 1536
    return pl.pallas_call(
        _ada_kernel,
        grid=(n // tn,),
        in_specs=[pl.BlockSpec((rows, d), lambda j: (0, 0)),
                  pl.BlockSpec((d, tn), lambda j: (0, j)),
                  pl.BlockSpec((1, tn), lambda j: (0, j))],
        out_specs=pl.BlockSpec((rows, tn), lambda j: (0, j)),
        out_shape=jax.ShapeDtypeStruct((rows, n), F32),
        compiler_params=_params(("parallel",)),
        name="adaln",
    )(c, w_ada, b_ada.reshape(1, n))


def _tile_lanes(t, width):
    return jnp.concatenate([t] * (width // LANES), axis=1) if width > LANES else t


def _rope(x, c, sa, sb):
    w = x.shape[1]
    return (x * _tile_lanes(c, w) + pltpu.roll(x, w - 8, 1) * _tile_lanes(sa, w)
            + pltpu.roll(x, 8, 1) * _tile_lanes(sb, w))


def _inproj_kernel(x_ref, sc_ref, sh_ref, g_ref, wh_ref, wl_ref, cs_ref, sa_ref, sb_ref,
                   u_ref, q_ref, k_ref, v_ref, kb_ref, vb_ref, iq_ref, ikw_ref, ga_ref, gb_ref):
    x = x_ref[0]
    h = x * lax.rsqrt(jnp.mean(x * x, axis=-1, keepdims=True) + EPS) * g_ref[...]
    h = h * (1.0 + sc_ref[0]) + sh_ref[0]
    hh, hl = _split(h)
    c, sa, sb = cs_ref[...], sa_ref[...], sb_ref[...]

    glu = _dg(hh, wh_ref[:, C_GLU:C_Q], 1, 0)
    u_ref[0] = glu[:, :CONV_DIM] * _sigmoid(glu[:, CONV_DIM:])

    qk = _dg(hh, wh_ref[:, C_Q:C_V], 1, 0)
    q_ref[0] = _rope(qk[:, :ATTN_DIM], c, sa, sb)
    k = _rope(qk[:, ATTN_DIM:], c, sa, sb)
    k_ref[0] = k
    kb_ref[0] = k.astype(BF16)
    v = _dg(hh, wh_ref[:, C_V:C_IQ], 1, 0)
    v_ref[0] = v
    vb_ref[0] = v.astype(BF16)

    wih = wh_ref[:, C_IQ:C_GA]
    idx = _dg(hh, wih, 1, 0) + _dg(hl, wih, 1, 0) + _dg(hh, wl_ref[...], 1, 0)
    iq_ref[0] = _rope(idx[:, :ATTN_DIM], c, sa, sb)
    lane = lax.broadcasted_iota(I32, (1, LANES), 1)
    is_ik = lane < IDX_DIM
    iw_scale = float((IDX_HEADS * IDX_DIM) ** -0.5)
    ikw_ref[0] = _rope(idx[:, ATTN_DIM:], jnp.where(is_ik, c, iw_scale),
                       jnp.where(is_ik, sa, 0.0), jnp.where(is_ik, sb, 0.0))

    ga_ref[0] = _sigmoid(_dg(hh, wh_ref[:, C_GA:C_GB], 1, 0))
    gb_ref[0] = _sigmoid(_dg(hh, wh_ref[:, C_GB:C_END], 1, 0))


def _mod_spec(arr, tm):
    d = arr.shape[-1]
    if arr.shape[1] == 1:
        return pl.BlockSpec((1, 1, d), lambda b, i: (b, 0, 0))
    return pl.BlockSpec((1, tm, d), lambda b, i: (b, i, 0))


def _inproj(x, scale1, shift1, g_mix, wh, wl, cs, sa, sb, tm):
    bx, s, d = x.shape
    row = lambda w: pl.BlockSpec((1, tm, w), lambda b, i: (b, i, 0))
    tab = pl.BlockSpec((tm, LANES), lambda b, i: (i, 0))
    out_w = (CONV_DIM, ATTN_DIM, ATTN_DIM, ATTN_DIM, ATTN_DIM, ATTN_DIM, ATTN_DIM, LANES, d, d)
    out_t = (F32, F32, F32, F32, BF16, BF16, F32, F32, F32, F32)
    return pl.pallas_call(
        _inproj_kernel,
        grid=(bx, s // tm),
        in_specs=[row(d), _mod_spec(scale1, tm), _mod_spec(shift1, tm), _const_spec((1, d)),
                  _const_spec(wh.shape), _const_spec(wl.shape), tab, tab, tab],
        out_specs=[row(w) for w in out_w],
        out_shape=[jax.ShapeDtypeStruct((bx, s, w), t) for w, t in zip(out_w, out_t)],
        compiler_params=_params(("parallel", "parallel")),
        name="inproj",
    )(x, scale1, shift1, g_mix.reshape(1, d), wh, wl, cs, sa, sb)


def _conv_kernel(u_ref, hist_ref, ga_ref, wdw_ref, bdw_ref, lg_ref, lb_ref, wo_ref, o_ref, win_ref, *, t):
    @pl.when(pl.program_id(1) == 0)
    def _():
        win_ref[0:HALO, :] = hist_ref[0]

    win_ref[HALO:HALO + t, :] = u_ref[0]
    off = HALO - (CONV_WIDTH - 1)
    acc = jnp.zeros((t, CONV_DIM), F32)
    for j in range(CONV_WIDTH):
        acc = acc + win_ref[off + j:off + j + t, :] * wdw_ref[j:j + 1, :]
    dw = acc + bdw_ref[...]
    mu = jnp.mean(dw, axis=-1, keepdims=True)
    var = jnp.mean(jnp.square(dw - mu), axis=-1, keepdims=True)
    y = (dw - mu) * lax.rsqrt(var + EPS) * lg_ref[...] + lb_ref[...]
    y = y * _sigmoid(y)
    o_ref[0] = ga_ref[0] * _dg(y.astype(BF16), wo_ref[...], 1, 0)
    tail = win_ref[t:t + HALO, :]
    win_ref[0:HALO, :] = tail


def _conv_branch(u, hist, ga, w_dw, b_dw, ln_g, ln_b, w_out_b, t):
    bx, s, _ = u.shape
    d = ga.shape[-1]
    wdw = jnp.pad(w_dw, ((0, HALO - CONV_WIDTH), (0, 0)))
    vec = lambda a: a.reshape(1, CONV_DIM)
    return pl.pallas_call(
        functools.partial(_conv_kernel, t=t),
        grid=(bx, s // t),
        in_specs=[pl.BlockSpec((1, t, CONV_DIM), lambda b, i: (b, i, 0)),
                  pl.BlockSpec((1, HALO, CONV_DIM), lambda b, i: (b, 0, 0)),
                  pl.BlockSpec((1, t, d), lambda b, i: (b, i, 0)),
                  _const_spec((HALO, CONV_DIM)), _const_spec((1, CONV_DIM)), _const_spec((1, CONV_DIM)),
                  _const_spec((1, CONV_DIM)), _const_spec((CONV_DIM, d))],
        out_specs=pl.BlockSpec((1, t, d), lambda b, i: (b, i, 0)),
        out_shape=jax.ShapeDtypeStruct((bx, s, d), F32),
        scratch_shapes=[pltpu.VMEM((t + HALO, CONV_DIM), F32)],
        compiler_params=_params(("parallel", "arbitrary")),
        name="conv_branch",
    )(u, hist, ga, wdw, vec(b_dw), vec(ln_g), vec(ln_b), w_out_b)


def _select_threshold(count_fn, n_sel, rows, pos_bits, cut_ref):
    def bit_body(it, lo):
        cand = lo + lax.shift_left(jnp.int32(1), 31 - it)
        cnt = count_fn(lambda k, p, c: k >= c, cand)
        return jnp.where(cnt >= n_sel, cand, lo)

    tau = lax.fori_loop(0, 32, bit_body, jnp.full((rows, 1), INT_MIN, I32))
    surplus = count_fn(lambda k, p, c: k >= c, tau) - n_sel
    cut_ref[...] = jnp.full(cut_ref.shape, INT_MAX, I32)

    @pl.when(jnp.max(surplus.astype(F32)) > 0.0)
    def _():
        need = n_sel - count_fn(lambda k, p, c: k > c, tau)

        def pos_body(it, x):
            cand = x + lax.shift_left(jnp.int32(1), pos_bits - 1 - it)
            cnt = count_fn(lambda k, p, c, d: (k == c) & (p < d), tau, cand)
            return jnp.where(cnt < need, cand, x)

        cut = lax.fori_loop(0, pos_bits, pos_body, jnp.zeros((rows, 1), I32))
        cut_ref[...] = jnp.broadcast_to(cut, cut_ref.shape)

    return tau


def _hi_lo_f32(x):
    hi = x.astype(BF16).astype(F32)
    return hi, x - hi


def _dsa_prompt_kernel(iq_ref, ikwq_ref, q_ref, ikw_ref, kb_ref, vb_ref, o_ref,
                       key_ref, ik3_ref, a3_ref, w_ref, q2_ref, cut_ref, s_ref,
                       *, n_sel, qb, kc, s_len):
    i = pl.program_id(1)
    n_ch = (i * qb + qb + kc - 1) // kc
    n_sub = kc // LANES
    lane = lax.broadcasted_iota(I32, (1, LANES), 1)
    low_half = lane < HEAD_DIM

    @pl.when(i == 0)
    def _():
        def body(c, carry):
            blk = ikw_ref[0, pl.ds(pl.multiple_of(c * kc, kc), kc), :]
            hi, lo = _hi_lo_f32(blk)
            first = jnp.where(low_half, hi, pltpu.roll(lo, HEAD_DIM, 1))
            second = jnp.where(low_half, hi, 0.0)
            ik3_ref[c] = jnp.concatenate([first, second], axis=1).astype(BF16)
            return carry
        lax.fori_loop(0, s_len // kc, body, 0)

    iq = iq_ref[0]
    ikwq = ikwq_ref[0]
    q = q_ref[0] * (HEAD_DIM ** -0.5)
    for h in range(IDX_HEADS):
        pair = slice((h // 2) * LANES, (h // 2 + 1) * LANES)
        x = iq[:, pair] if h % 2 == 0 else pltpu.roll(iq[:, pair], HEAD_DIM, 1)
        hi, lo = _hi_lo_f32(x)
        first = jnp.where(low_half, hi, pltpu.roll(hi, HEAD_DIM, 1))
        second = jnp.where(low_half, lo, 0.0)
        a3_ref[h] = jnp.concatenate([first, second], axis=1).astype(BF16)
        w_ref[h] = jnp.broadcast_to(ikwq[:, IDX_DIM + h:IDX_DIM + h + 1], (qb, LANES))
        keep = low_half if h % 2 == 0 else jnp.logical_not(low_half)
        q2_ref[h] = jnp.where(keep, q[:, pair], 0.0).astype(BF16)

    qpos = i * qb + lax.broadcasted_iota(I32, (qb, 1), 0)
    sub_lane = lax.broadcasted_iota(I32, (1, kc), 1)

    def score_body(c, carry):
        kk = ik3_ref[c]
        tot = jnp.zeros((qb, kc), F32)
        for h in range(IDX_HEADS):
            s = _dg(a3_ref[h], kk, 1, 1)
            tot = tot + jnp.maximum(s, 0.0) * _tile_lanes(w_ref[h], kc)
        causal = (c * kc + sub_lane) <= qpos
        key_ref[c] = _sort_key(jnp.where(causal, tot, -jnp.inf))
        return carry
    lax.fori_loop(0, n_ch, score_body, 0)

    def count_fn(pred, *thr):
        counts = []
        for r0 in range(0, qb, COUNT_ROWS):
            rs = slice(r0, min(r0 + COUNT_ROWS, qb))
            nrows = rs.stop - rs.start
            thr_b = [jnp.broadcast_to(x[rs], (nrows, LANES)) for x in thr]

            def body(c, acc):
                kblk = key_ref[c, rs, :]
                for j in range(n_sub):
                    pos = c * kc + j * LANES + lane
                    acc = acc + pred(kblk[:, j * LANES:(j + 1) * LANES], pos, *thr_b).astype(I32)
                return acc
            acc = lax.fori_loop(0, n_ch, body, jnp.zeros((nrows, LANES), I32))
            counts.append(jnp.sum(acc, axis=1, keepdims=True))
        return jnp.concatenate(counts, axis=0) if len(counts) > 1 else counts[0]

    tau = _select_threshold(count_fn, n_sel, qb, max(1, (s_len - 1).bit_length()) + 1, cut_ref)

    for r0 in range(0, qb, COUNT_ROWS):
        rs = slice(r0, min(r0 + COUNT_ROWS, qb))
        shape = (rs.stop - rs.start, LANES)
        tau_b = jnp.broadcast_to(tau[rs], shape)
        qpos_b = jnp.broadcast_to(qpos[rs], shape)
        cut_b = cut_ref[rs, :]

        def bias_body(c, carry):
            kblk = key_ref[c, rs, :]
            for j in range(n_sub):
                k = kblk[:, j * LANES:(j + 1) * LANES]
                pos = c * kc + j * LANES + lane
                sel = ((k > tau_b) | ((k == tau_b) & (pos <= cut_b))) & (pos <= qpos_b)
                key_ref[c, rs, j * LANES:(j + 1) * LANES] = lax.bitcast_convert_type(jnp.where(sel, 0.0, NEG), I32)
            return carry
        lax.fori_loop(0, n_ch, bias_body, 0)

    def fold(x, op):
        out = x[:, :LANES]
        for j in range(1, n_sub):
            out = op(out, x[:, j * LANES:(j + 1) * LANES])
        return out

    def attn_body(c, carry):
        ms, ls, accs = carry
        rows = pl.ds(pl.multiple_of(c * kc, kc), kc)
        pair_cols = lambda h: slice((h // 2) * LANES, (h // 2 + 1) * LANES)

        def issue_logits(h):
            s_ref[h % 2] = _dg(q2_ref[h], kb_ref[0, rows, pair_cols(h)], 1, 1)

        issue_logits(0)
        new_m, new_l, new_acc = [], [], []
        for hp in range(N_HEADS // 2):
            vpair = vb_ref[0, rows, pair_cols(2 * hp)]
            pvs, alphas = [], []
            for h in (2 * hp, 2 * hp + 1):
                if h + 1 < N_HEADS:
                    issue_logits(h + 1)
                s = s_ref[h % 2] + lax.bitcast_convert_type(key_ref[c], F32)
                m_new = jnp.maximum(ms[h], jnp.max(fold(s, jnp.maximum), axis=1, keepdims=True))
                p = jnp.exp(s - _tile_lanes(m_new, kc))
                alpha = jnp.exp(ms[h] - m_new)
                new_m.append(m_new)
                new_l.append(alpha * ls[h] + fold(p, jnp.add))
                pvs.append(_dg(p.astype(BF16), vpair, 1, 0))
                alphas.append(alpha)
            new_acc.append(jnp.where(low_half, alphas[0], alphas[1]) * accs[hp]
                           + jnp.where(low_half, pvs[0], pvs[1]))
        return tuple(new_m), tuple(new_l), tuple(new_acc)

    zero = jnp.zeros((qb, LANES), F32)
    init = ((jnp.full((qb, LANES), NEG, F32),) * N_HEADS, (zero,) * N_HEADS, (zero,) * (N_HEADS // 2))
    _, ls, accs = lax.fori_loop(0, n_ch, attn_body, init)
    for hp in range(N_HEADS // 2):
        l0 = jnp.sum(ls[2 * hp], axis=1, keepdims=True)
        l1 = jnp.sum(ls[2 * hp + 1], axis=1, keepdims=True)
        o_ref[0, :, hp * LANES:(hp + 1) * LANES] = accs[hp] / jnp.where(low_half, l0, l1)


def _dsa_prompt(iq, ikw, q, kb, vb, qb, kc):
    b, s, _ = q.shape
    n_sel = min(TOPK_MAX, s // 4)
    full = lambda w: pl.BlockSpec((1, s, w), lambda bb, i: (bb, 0, 0), pipeline_mode=pl.Buffered(1))
    blk = lambda w: pl.BlockSpec((1, qb, w), lambda bb, i: (bb, i, 0))
    return pl.pallas_call(
        functools.partial(_dsa_prompt_kernel, n_sel=n_sel, qb=qb, kc=kc, s_len=s),
        grid=(b, s // qb),
        in_specs=[blk(ATTN_DIM), blk(LANES), blk(ATTN_DIM), full(LANES), full(ATTN_DIM), full(ATTN_DIM)],
        out_specs=blk(ATTN_DIM),
        out_shape=jax.ShapeDtypeStruct((b, s, ATTN_DIM), F32),
        scratch_shapes=[pltpu.VMEM((s // kc, qb, kc), I32),
                        pltpu.VMEM((s // kc, kc, 2 * LANES), BF16),
                        pltpu.VMEM((IDX_HEADS, qb, 2 * LANES), BF16),
                        pltpu.VMEM((IDX_HEADS, qb, LANES), F32),
                        pltpu.VMEM((N_HEADS, qb, LANES), BF16),
                        pltpu.VMEM((qb, LANES), I32),
                        pltpu.VMEM((2, qb, kc), F32)],
        compiler_params=_params(("parallel", "arbitrary")),
        name="dsa_prompt",
    )(iq, ikw, q, ikw, kb, vb)


def _sample_select_kernel(pt_ref, qh_ref, ql_ref, w_ref, new_ref, *rest, g, n_grp, n_sel, past, t):
    pages = rest[:g]
    o_ref, key_ref, cur_ref, cut_ref = rest[g:]
    j = pl.program_id(1)
    lane = lax.broadcasted_iota(I32, (1, LANES), 1)
    qh, ql, w = qh_ref[0], ql_ref[0], w_ref[0]

    def scores(ik):
        kh, kl = _split(ik)
        s = _dot3(qh, ql, kh, kl, 1, 1)
        tot = jnp.maximum(s, 0.0) * w
        return jnp.sum(tot.reshape(IDX_HEADS, t, LANES), axis=0)

    @pl.when(j < n_grp)
    def _():
        key_ref[j] = jnp.concatenate([_sort_key(scores(pages[r][0])) for r in range(g)], axis=1)

    @pl.when(j == n_grp)
    def _():
        tpos = lax.broadcasted_iota(I32, (t, 1), 0)
        cur_ref[...] = _sort_key(jnp.where(lane <= tpos, scores(new_ref[0]), -jnp.inf))

        def count_fn(pred, *thr):
            thr_b = [jnp.broadcast_to(x, (t, LANES)) for x in thr]

            def body(c, acc):
                kblk = key_ref[c]
                for r in range(g):
                    pos = (c * g + r) * LANES + lane
                    acc = acc + pred(kblk[:, r * LANES:(r + 1) * LANES], pos, *thr_b).astype(I32)
                return acc
            acc = lax.fori_loop(0, n_grp, body, jnp.zeros((t, LANES), I32))
            acc = acc + pred(cur_ref[...], past + lane, *thr_b).astype(I32)
            return jnp.sum(acc, axis=1, keepdims=True)

        tau = _select_threshold(count_fn, n_sel, t, max(1, (past + LANES - 1).bit_length()) + 1, cut_ref)
        wide_lane = lax.broadcasted_iota(I32, (1, g * LANES), 1)
        cut = cut_ref[...]

        def out_body(c, carry):
            kblk = key_ref[c]
            pos = c * g * LANES + wide_lane
            sel = (kblk > tau) | ((kblk == tau) & (pos <= _tile_lanes(cut, g * LANES)))
            o_ref[0, c] = jnp.where(sel, 0.0, NEG)
            return carry
        lax.fori_loop(0, n_grp, out_body, 0)
        kcur = cur_ref[...]
        pos = past + lane
        sel = ((kcur > tau) | ((kcur == tau) & (pos <= cut))) & (lane <= tpos)
        cur_bias = jnp.where(sel, 0.0, NEG)
        o_ref[0, n_grp] = jnp.concatenate([cur_bias] + [jnp.full((t, LANES), NEG, F32)] * (g - 1), axis=1)


def _page_specs(g, n_grp, layer_base, block_tail):
    nd = len(block_tail)

    def make(r):
        def index_map(b, j, pt):
            return (layer_base + pt[b, jnp.minimum(j, n_grp - 1) * g + r],) + (0,) * nd
        return pl.BlockSpec((1,) + block_tail, index_map)
    return [make(r) for r in range(g)]


def _cache_page_specs(g, n_grp, layer):
    def make(r):
        def index_map(b, j, pt):
            return (layer, pt[b, jnp.minimum(j, n_grp - 1) * g + r], 0, 0, 0)
        return pl.BlockSpec((None, None, PAGE, N_HEADS, HEAD_DIM), index_map)
    return [make(r) for r in range(g)]


def _sample_select(page_table, qh, ql, w, ik_new, idx_pages, layer_base, g, t):
    db, n_pages = page_table.shape
    n_grp = n_pages // g
    past = n_pages * PAGE
    n_sel = min(TOPK_MAX, (past + t) // 4)
    rows = IDX_HEADS * t
    per_b = lambda shape: pl.BlockSpec((1,) + shape, lambda b, j, pt: (b,) + (0,) * len(shape))
    kern = functools.partial(_sample_select_kernel, g=g, n_grp=n_grp, n_sel=n_sel, past=past, t=t)
    return pl.pallas_call(
        kern,
        grid_spec=pltpu.PrefetchScalarGridSpec(
            num_scalar_prefetch=1,
            grid=(db, n_grp + 1),
            in_specs=[per_b((rows, IDX_DIM)), per_b((rows, IDX_DIM)), per_b((rows, LANES)),
                      per_b((PAGE, IDX_DIM))] + _page_specs(g, n_grp, layer_base, (PAGE, IDX_DIM)),
            out_specs=per_b((n_grp + 1, t, g * LANES)),
            scratch_shapes=[pltpu.VMEM((n_grp, t, g * LANES), I32), pltpu.VMEM((t, LANES), I32),
                            pltpu.VMEM((t, LANES), I32)]),
        out_shape=jax.ShapeDtypeStruct((db, n_grp + 1, t, g * LANES), F32),
        compiler_params=_params(("parallel", "arbitrary")),
        name="sample_select",
    )(page_table, qh, ql, w, ik_new, *([idx_pages] * g))


def _sample_attend_kernel(pt_ref, q_ref, bias_ref, knew_ref, vnew_ref, *rest, g, n_grp, t):
    kpages, vpages = rest[:g], rest[g:2 * g]
    o_ref, m_ref, l_ref, acc_ref = rest[2 * g:]
    j = pl.program_id(1)
    rows = N_HEADS * t

    @pl.when(j == 0)
    def _():
        m_ref[...] = jnp.full(m_ref.shape, NEG, F32)
        l_ref[...] = jnp.zeros(l_ref.shape, F32)
        acc_ref[...] = jnp.zeros(acc_ref.shape, F32)

    def step(kmat, vmat, bias):
        n_sub = kmat.shape[0] // LANES
        s = _dg(q_ref[0], kmat.astype(BF16), 1, 1) + jnp.concatenate([bias] * N_HEADS, axis=0)
        m_old = m_ref[...]
        m_new = jnp.maximum(m_old, jnp.max(s, axis=1, keepdims=True))
        p = jnp.exp(s - _tile_lanes(m_new, kmat.shape[0]))
        alpha = jnp.exp(m_old - m_new)
        psum = p[:, :LANES]
        for r in range(1, n_sub):
            psum = psum + p[:, r * LANES:(r + 1) * LANES]
        l_ref[...] = alpha * l_ref[...] + psum
        m_ref[...] = m_new
        acc_ref[...] = (_tile_lanes(alpha, ATTN_DIM) * acc_ref[...]
                        + _dg(p.astype(BF16), vmat.astype(BF16), 1, 0))

    @pl.when(j < n_grp)
    def _():
        def page_matrix(p_ref):
            rows = p_ref.reshape(PAGE * N_HEADS, HEAD_DIM)
            head = lambda h: rows[pl.ds(h, PAGE, stride=N_HEADS), :]
            pairs = [jnp.concatenate([head(h), head(h + 1)], axis=1).astype(BF16) for h in range(0, N_HEADS, 2)]
            return jnp.concatenate(pairs, axis=1)

        step(jnp.concatenate([page_matrix(kp) for kp in kpages], axis=0),
             jnp.concatenate([page_matrix(vp) for vp in vpages], axis=0), bias_ref[0, 0])

    @pl.when(j == n_grp)
    def _():
        step(knew_ref[0], vnew_ref[0], bias_ref[0, 0][:, :LANES])
        full = acc_ref[...] / jnp.sum(l_ref[...], axis=1, keepdims=True)
        lane = lax.broadcasted_iota(I32, (1, ATTN_DIM), 1)
        out = jnp.zeros((t, ATTN_DIM), F32)
        for h in range(N_HEADS):
            in_head = (lane >= h * HEAD_DIM) & (lane < (h + 1) * HEAD_DIM)
            out = out + jnp.where(in_head, full[h * t:(h + 1) * t, :], 0.0)
        o_ref[0] = out


def _sample_attend(page_table, qrows, bias, k_new, v_new, k_pages, v_pages, layer, g, t):
    db, n_pages = page_table.shape
    n_grp = n_pages // g
    rows = N_HEADS * t
    per_b = lambda shape: pl.BlockSpec((1,) + shape, lambda b, j, pt: (b,) + (0,) * len(shape))
    kern = functools.partial(_sample_attend_kernel, g=g, n_grp=n_grp, t=t)
    return pl.pallas_call(
        kern,
        grid_spec=pltpu.PrefetchScalarGridSpec(
            num_scalar_prefetch=1,
            grid=(db, n_grp + 1),
            in_specs=[per_b((rows, ATTN_DIM)),
                      pl.BlockSpec((1, 1, t, g * LANES), lambda b, j, pt: (b, j, 0, 0)),
                      per_b((PAGE, ATTN_DIM)), per_b((PAGE, ATTN_DIM))]
                     + _cache_page_specs(g, n_grp, layer) + _cache_page_specs(g, n_grp, layer),
            out_specs=per_b((t, ATTN_DIM)),
            scratch_shapes=[pltpu.VMEM((rows, LANES), F32), pltpu.VMEM((rows, LANES), F32),
                            pltpu.VMEM((rows, ATTN_DIM), F32)]),
        out_shape=jax.ShapeDtypeStruct((db, t, ATTN_DIM), F32),
        compiler_params=_params(("parallel", "arbitrary")),
        name="sample_attend",
    )(page_table, qrows, bias, k_new, v_new, *([k_pages] * g), *([v_pages] * g))


def _mix_kernel(x_ref, ag_ref, gb_ref, at_ref, g1_ref, sc2_ref, sh2_ref, gf_ref,
                wat_ref, wo_ref, wqh_ref, wql_ref, k1h_ref, k1l_ref, k2h_ref, k2l_ref,
                x1_ref, h2_ref, s1_ref, s2_ref):
    bb = _dg(at_ref[0].astype(BF16), wat_ref[...], 1, 0)
    mixed = ag_ref[0] + gb_ref[0] * bb
    x1 = x_ref[0] + g1_ref[0] * _dg(mixed.astype(BF16), wo_ref[...], 1, 0)
    x1_ref[0] = x1
    h2 = x1 * lax.rsqrt(jnp.mean(x1 * x1, axis=-1, keepdims=True) + EPS) * gf_ref[...]
    h2 = h2 * (1.0 + sc2_ref[0]) + sh2_ref[0]
    h2_ref[0] = h2.astype(BF16)
    hh, hl = _split(h2)
    qp = _dot3(hh, hl, wqh_ref[...], wql_ref[...], 1, 0)
    for h in range(PEER_HEADS):
        ah, al = _split(qp[:, h * PEER_QDIM:h * PEER_QDIM + PEER_HALF])
        bh, bl = _split(qp[:, h * PEER_QDIM + PEER_HALF:(h + 1) * PEER_QDIM])
        s1_ref[h] = _dot3(k1h_ref[...], k1l_ref[...], ah, al, 1, 1)
        s2_ref[h] = _dot3(k2h_ref[...], k2l_ref[...], bh, bl, 1, 1)


def _mix(x, ag, gb, attn, gate1, scale2, shift2, g_ffn, wat_b, wo_b, wq_h, wq_l, k1, k2, tm):
    bx, s, d = x.shape
    n = bx * s
    row = lambda w: pl.BlockSpec((1, tm, w), lambda b, i: (b, i, 0))
    tr = pl.BlockSpec((PEER_HEADS, PEER_KEYS, tm), lambda b, i: (0, 0, b * (s // tm) + i))
    k1h, k1l = _split(k1)
    k2h, k2l = _split(k2)
    return pl.pallas_call(
        _mix_kernel,
        grid=(bx, s // tm),
        in_specs=[row(d), row(d), row(d), row(ATTN_DIM), _mod_spec(gate1, tm), _mod_spec(scale2, tm),
                  _mod_spec(shift2, tm), _const_spec((1, d)), _const_spec(wat_b.shape), _const_spec(wo_b.shape),
                  _const_spec(wq_h.shape), _const_spec(wq_l.shape)] + [_const_spec((PEER_KEYS, PEER_HALF))] * 4,
        out_specs=[row(d), row(d), tr, tr],
        out_shape=[jax.ShapeDtypeStruct((bx, s, d), F32), jax.ShapeDtypeStruct((bx, s, d), BF16),
                   jax.ShapeDtypeStruct((PEER_HEADS, PEER_KEYS, n), F32),
                   jax.ShapeDtypeStruct((PEER_HEADS, PEER_KEYS, n), F32)],
        compiler_params=_params(("parallel", "parallel")),
        name="mix_peer_query",
    )(x, ag, gb, attn, gate1, scale2, shift2, g_ffn.reshape(1, d), wat_b, wo_b, wq_h, wq_l, k1h, k1l, k2h, k2l)


A_PER_STEP = 8
PEER_COLS = 256
N_CAND = 80


def _top16(s, break_ties):
    rows = lax.broadcasted_iota(I32, s.shape, 0).astype(F32)
    rank = jnp.full(s.shape, NO_RANK, F32)
    vals = []
    for it in range(PEER_TOPK):
        m = jnp.max(s, axis=0, keepdims=True)
        pick = s == m
        if break_ties:
            pick = rows == jnp.min(jnp.where(pick, rows, float(PEER_KEYS)), axis=0, keepdims=True)
        rank = jnp.where(pick, float(it), rank)
        s = jnp.where(pick, -jnp.inf, s)
        vals.append(m)
    return jnp.concatenate(vals, axis=0), rank


def _peer_select_kernel(s1_ref, s2_ref, e1_ref, jr_ref, e2_ref, r2_ref):
    tn = s1_ref.shape[-1]
    r = lax.broadcasted_iota(I32, (N_CAND, 1), 0)
    mid_i = lax.shift_right_arithmetic(r - 16, 3) + 1
    mid_j = (r - 16) & 7
    flat = jnp.where(r < 16, r, jnp.where(r < 72, mid_i * 16 + mid_j, (r - 64) * 16)).astype(F32)
    limit = jnp.where(mid_i == 1, 8, jnp.where(mid_i == 2, 5, jnp.where(mid_i == 3, 4, jnp.where(mid_i == 4, 3, 2))))
    cell_ok = (r < 16) | (r >= 72) | (mid_j < limit)
    row8 = lax.broadcasted_iota(I32, (8, 1), 0)

    def select_head(h, break_ties):
        s1, s2 = s1_ref[h], s2_ref[h]
        v1, rank1 = _top16(s1, break_ties)
        v2, rank2 = _top16(s2, break_ties)
        groups = [v1[0:1] + v2[0:16]] + [v1[i:i + 1] + v2[0:8] for i in range(1, 8)] + [v1[8:16] + v2[0:1]]
        cand = jnp.where(cell_ok, jnp.concatenate(groups, axis=0), -jnp.inf)
        chosen = jnp.zeros((N_CAND, tn), F32)
        m0 = v1[0:1] + v2[0:1]
        zsum = jnp.zeros((1, tn), F32)
        for it in range(PEER_TOPK):
            m = jnp.max(cand, axis=0, keepdims=True)
            first = jnp.min(jnp.where(cand == m, flat, 4096.0), axis=0, keepdims=True)
            pick = flat == first
            chosen = jnp.where(pick, 1.0, chosen)
            cand = jnp.where(pick, -jnp.inf, cand)
            zsum = zsum + jnp.exp(m - m0)
        j_low = jnp.zeros((8, tn), F32)
        j_low = jnp.where(row8 == 0, jnp.sum(chosen[0:16], axis=0, keepdims=True), j_low)
        for i in range(1, 8):
            j_low = jnp.where(row8 == i, jnp.sum(chosen[8 + 8 * i:16 + 8 * i], axis=0, keepdims=True), j_low)
        j_high = chosen[72:80]
        jr = jnp.zeros((PEER_KEYS, tn), F32)
        for i in range(PEER_TOPK):
            ji = j_low[i:i + 1] if i < 8 else j_high[i - 8:i - 7]
            jr = jnp.where(rank1 == float(i), ji, jr)
        jr_ref[h] = jr
        e1_ref[h] = jnp.where(rank1 < NO_RANK, jnp.exp(s1 - v1[0:1]), 0.0) * (0.5 / zsum)
        e2_ref[h] = jnp.where(rank2 < NO_RANK, jnp.exp(s2 - v2[0:1]), 0.0).astype(BF16)
        r2_ref[h] = rank2.astype(BF16)
        ranked = (rank1 < NO_RANK).astype(F32) + (rank2 < NO_RANK).astype(F32)
        return jnp.max(jnp.sum(ranked, axis=0, keepdims=True))

    def head_body(h, carry):
        n_ranked = select_head(h, False)

        @pl.when(n_ranked > 2.0 * PEER_TOPK)
        def _():
            select_head(h, True)
        return carry
    lax.fori_loop(0, PEER_HEADS, head_body, 0)


def _peer_select(s1t, s2t, tn):
    _, _, n = s1t.shape
    spec = pl.BlockSpec((PEER_HEADS, PEER_KEYS, tn), lambda i: (0, 0, i))
    shape = jax.ShapeDtypeStruct(s1t.shape, F32)
    shape_b = jax.ShapeDtypeStruct(s1t.shape, BF16)
    return pl.pallas_call(
        _peer_select_kernel,
        grid=(n // tn,),
        in_specs=[spec, spec],
        out_specs=[spec] * 4,
        out_shape=[shape, shape, shape_b, shape_b],
        compiler_params=_params(("parallel",)),
        name="peer_select",
    )(s1t, s2t)


def _peer_expert_kernel(h2_ref, x1_ref, g2_ref, gfin_ref, u_ref, vt_ref, e1_ref, jr_ref, e2_ref, r2_ref,
                        y_ref, acc_ref, w_ref, z_ref, p_ref, *, a_per, final_norm):
    j = pl.program_id(2)
    tn = h2_ref.shape[1]
    cols = min(PEER_COLS, tn)
    tiles = [slice(c, c + cols) for c in range(0, tn, cols)]

    @pl.when(j == 0)
    def _():
        acc_ref[...] = jnp.zeros(acc_ref.shape, F32)

    def build_weights(cs):
        for al in range(a_per):
            wsum = jnp.zeros((PEER_KEYS, cols), BF16)
            for h in range(PEER_HEADS):
                jr = jr_ref[h, al:al + 1, cs].astype(BF16)
                e1 = e1_ref[h, al:al + 1, cs].astype(BF16)
                wsum = wsum + jnp.where(r2_ref[h, :, cs] < jr, e2_ref[h, :, cs] * e1, jnp.zeros((), BF16))
            w_ref[al * PEER_KEYS:(al + 1) * PEER_KEYS, cs] = wsum

    def expert_logits(cs):
        z_ref[:, cs] = _dg(u_ref[...], h2_ref[0, cs, :], 1, 1)

    def gate(cs):
        zt = z_ref[:, cs]
        gz = (zt * (1.0 + lax.erf(zt * (2.0 ** -0.5)))).astype(BF16)
        p_ref[:, cs] = w_ref[:, cs] * gz

    def fold(cs):
        acc_ref[:, cs] += _dg(vt_ref[...], p_ref[:, cs], 1, 0)

    expert_logits(tiles[0])
    for k, cs in enumerate(tiles):
        build_weights(cs)
        gate(cs)
        if k + 1 < len(tiles):
            expert_logits(tiles[k + 1])
        fold(cs)

    @pl.when(j == pl.num_programs(2) - 1)
    def _():
        x2 = x1_ref[0] + g2_ref[0] * acc_ref[...].T
        if final_norm:
            x2 = x2 * lax.rsqrt(jnp.mean(x2 * x2, axis=-1, keepdims=True) + EPS) * gfin_ref[...]
        y_ref[0] = x2


def _peer_expert(h2b, x1, gate2, g_final, u_b, vt_b, e1, jr, e2, r2, tn, a_per, final_norm):
    bx, s, d = x1.shape
    n_exp = u_b.shape[0]
    ec = a_per * PEER_KEYS
    tiles = s // tn
    row = lambda w: pl.BlockSpec((1, tn, w), lambda b, i, j: (b, i, 0))
    mod = (pl.BlockSpec((1, 1, d), lambda b, i, j: (b, 0, 0)) if gate2.shape[1] == 1
           else pl.BlockSpec((1, tn, d), lambda b, i, j: (b, i, 0)))
    per_a = pl.BlockSpec((PEER_HEADS, a_per, tn), lambda b, i, j: (0, j, b * tiles + i))
    per_b = pl.BlockSpec((PEER_HEADS, PEER_KEYS, tn), lambda b, i, j: (0, 0, b * tiles + i))
    return pl.pallas_call(
        functools.partial(_peer_expert_kernel, a_per=a_per, final_norm=final_norm),
        grid=(bx, tiles, n_exp // ec),
        in_specs=[row(d), row(d), mod, pl.BlockSpec((1, d), lambda b, i, j: (0, 0)),
                  pl.BlockSpec((ec, d), lambda b, i, j: (j, 0)),
                  pl.BlockSpec((d, ec), lambda b, i, j: (0, j)),
                  per_a, per_a, per_b, per_b],
        out_specs=row(d),
        out_shape=jax.ShapeDtypeStruct((bx, s, d), F32),
        scratch_shapes=[pltpu.VMEM((d, tn), F32), pltpu.VMEM((ec, tn), BF16), pltpu.VMEM((ec, tn), F32),
                        pltpu.VMEM((ec, tn), BF16)],
        compiler_params=_params(("parallel", "parallel", "arbitrary")),
        name="peer_expert",
    )(h2b, x1, gate2, g_final.reshape(1, d), u_b, vt_b, e1, jr, e2, r2)


def _rope_tables(pos):
    rot = HEAD_DIM // 4
    half = rot // 2
    inv_freq = ROPE_THETA ** (-(jnp.arange(half, dtype=F32) * (2.0 / rot)))
    ang = pos.astype(F32)[:, None] * inv_freq[None, :]
    cos, sin = jnp.cos(ang), jnp.sin(ang)
    t = pos.shape[0]
    ones = jnp.ones((t, HEAD_DIM - rot), F32)
    zeros = jnp.zeros((t, HEAD_DIM - rot), F32)
    z8 = jnp.zeros((t, half), F32)
    c = jnp.concatenate([cos, cos, ones], axis=1)
    sa = jnp.concatenate([-sin, z8, zeros], axis=1)
    sb = jnp.concatenate([z8, sin, zeros], axis=1)
    dup = lambda a: jnp.concatenate([a, a], axis=1)
    return dup(c), dup(sa), dup(sb)


def _pack_w_in(w_in):
    d = w_in.shape[0]
    cols = C_IKW - C_GLU
    pad = jnp.zeros((d, LANES - IDX_DIM - IDX_HEADS), w_in.dtype)
    return jnp.concatenate([w_in[:, :cols + IDX_DIM + IDX_HEADS], pad, w_in[:, cols + IDX_DIM + IDX_HEADS:]], axis=1)


def _pick_tile(n, pref):
    t = min(n, pref)
    while n % t:
        t //= 2
    return t


def kernel(x_prompt, x_sample, c_prompt, c_sample, cache_k, cache_v, cache_idx_k, state_conv, page_table, w_ada, b_ada, g_norm_mix, w_in, w_dw, b_dw, ln_conv_g, ln_conv_b, w_conv_out, w_attn_out, w_o, g_norm_ffn, w_peer_q, peer_k1, peer_k2, peer_u, peer_v, g_norm_final):
    b, s, d = x_prompt.shape
    db, t, _ = x_sample.shape
    depth = w_ada.shape[0]
    n_phys = cache_k.shape[1]
    n_pages = page_table.shape[1]
    past = n_pages * PAGE
    ns = db * t
    g_pages = math.gcd(n_pages, 8)

    tabs_p = _rope_tables(jnp.arange(s))
    tabs_s = _rope_tables(jnp.tile(past + jnp.arange(t), db))
    idx_pages = cache_idx_k.reshape(depth * n_phys, PAGE, IDX_DIM)
    c_all = jnp.concatenate([c_prompt, c_sample], axis=0)
    c_rows = -(-c_all.shape[0] // 8) * 8
    c_all = jnp.pad(c_all, ((0, c_rows - c_all.shape[0]), (0, 0)))

    tm_p, tm_s = _pick_tile(s, 256), _pick_tile(ns, 256)
    tc_p = _pick_tile(s, 256)
    qb = _pick_tile(s, 256)
    kc = _pick_tile(s, 512)
    tn_sel_p, tn_sel_s = _pick_tile(b * s, 256), _pick_tile(ns, 256)
    tn_exp_p, tn_exp_s = _pick_tile(s, 512), _pick_tile(ns, 512)

    xp, xs = x_prompt, x_sample.reshape(1, ns, d)
    outs = [[] for _ in range(8)]
    for l in range(depth):
        mod = _ada(c_all, w_ada[l], b_ada[l])
        mod_p = [m.reshape(b, 1, d) for m in jnp.split(mod[:b], 6, axis=-1)]
        mod_s = [jnp.repeat(m, t, axis=0).reshape(1, ns, d) for m in jnp.split(mod[b:b + db], 6, axis=-1)]
        w_pack = _pack_w_in(w_in[l])
        wh = w_pack.astype(BF16)
        wl = (w_pack[:, C_IQ:C_GA] - wh[:, C_IQ:C_GA].astype(F32)).astype(BF16)
        wco_b, wat_b, wo_b = w_conv_out[l].astype(BF16), w_attn_out[l].astype(BF16), w_o[l].astype(BF16)
        wq_h, wq_l = _split(w_peer_q[l])
        u_b = peer_u[l].astype(BF16)
        vt_b = peer_v[l].astype(BF16).T
        conv_w = (w_dw[l], b_dw[l], ln_conv_g[l], ln_conv_b[l], wco_b)

        u, q, k, v, kb, vb, iq, ikw, ga, gb = _inproj(xp, mod_p[1], mod_p[0], g_norm_mix[l], wh, wl, *tabs_p, tm_p)
        hist0 = jnp.zeros((b, HALO, CONV_DIM), F32)
        ag = _conv_branch(u, hist0, ga, *conv_w, tc_p)
        attn = _dsa_prompt(iq, ikw, q, kb, vb, qb, kc)
        x1, h2b, s1t, s2t = _mix(xp, ag, gb, attn, mod_p[2], mod_p[4], mod_p[3], g_norm_ffn[l],
                                 wat_b, wo_b, wq_h, wq_l, peer_k1[l], peer_k2[l], tm_p)
        sel = _peer_select(s1t, s2t, tn_sel_p)
        last = l == depth - 1
        xp = _peer_expert(h2b, x1, mod_p[5], g_norm_final, u_b, vt_b, *sel, tn_exp_p, A_PER_STEP, last)
        outs[0].append(k.reshape(b, s // PAGE, PAGE, N_HEADS, HEAD_DIM))
        outs[1].append(v.reshape(b, s // PAGE, PAGE, N_HEADS, HEAD_DIM))
        outs[2].append(ikw[:, :, :IDX_DIM].reshape(b, s // PAGE, PAGE, IDX_DIM))
        outs[3].append(jnp.concatenate([hist0[:, HALO - (CONV_WIDTH - 1):], u], axis=1)[:, -(CONV_WIDTH - 1):])

        u, q, k, v, kb, vb, iq, ikw, ga, gb = _inproj(xs, mod_s[1], mod_s[0], g_norm_mix[l], wh, wl, *tabs_s, tm_s)
        u3 = u.reshape(db, t, CONV_DIM)
        hist = jnp.pad(state_conv[l], ((0, 0), (HALO - (CONV_WIDTH - 1), 0), (0, 0)))
        ag = _conv_branch(u3, hist, ga.reshape(db, t, d), *conv_w, t).reshape(1, ns, d)
        iq_ht = iq.reshape(db, t, IDX_HEADS, IDX_DIM).transpose(0, 2, 1, 3).reshape(db, IDX_HEADS * t, IDX_DIM)
        iqh, iql = _split(iq_ht)
        iw = ikw.reshape(db, t, LANES)[:, :, IDX_DIM:IDX_DIM + IDX_HEADS]
        w_rows = jnp.broadcast_to(iw.transpose(0, 2, 1).reshape(db, IDX_HEADS * t, 1), (db, IDX_HEADS * t, LANES))
        pad_rows = lambda a: jnp.pad(a.reshape(db, t, -1), ((0, 0), (0, PAGE - t), (0, 0)))
        bias = _sample_select(page_table, iqh, iql, w_rows, pad_rows(ikw[..., :IDX_DIM]), idx_pages,
                              l * n_phys, g_pages, t)
        q4 = q.reshape(db, t, N_HEADS, HEAD_DIM) * (HEAD_DIM ** -0.5)
        eye = jnp.eye(N_HEADS, dtype=F32)
        qrows = jnp.einsum('bthd,hg->bhtgd', q4, eye).reshape(db, N_HEADS * t, ATTN_DIM).astype(BF16)
        attn = _sample_attend(page_table, qrows, bias, pad_rows(k), pad_rows(v), cache_k, cache_v,
                              l, g_pages, t).reshape(1, ns, ATTN_DIM)
        x1, h2b, s1t, s2t = _mix(xs, ag, gb, attn, mod_s[2], mod_s[4], mod_s[3], g_norm_ffn[l],
                                 wat_b, wo_b, wq_h, wq_l, peer_k1[l], peer_k2[l], tm_s)
        sel = _peer_select(s1t, s2t, tn_sel_s)
        xs = _peer_expert(h2b, x1, mod_s[5], g_norm_final, u_b, vt_b, *sel, tn_exp_s, A_PER_STEP, last)
        outs[4].append(k.reshape(db, t, N_HEADS, HEAD_DIM))
        outs[5].append(v.reshape(db, t, N_HEADS, HEAD_DIM))
        outs[6].append(ikw.reshape(db, t, LANES)[:, :, :IDX_DIM])
        outs[7].append(jnp.concatenate([state_conv[l], u3], axis=1)[:, -(CONV_WIDTH - 1):])

    stacked = [jnp.stack(o) for o in outs]
    return (xp, xs.reshape(db, t, d), *stacked)
```

```python
import functools
import math

import jax
import jax.numpy as jnp
from jax import lax
from jax.experimental import pallas as pl
from jax.experimental.pallas import tpu as pltpu

F32 = jnp.float32
BF16 = jnp.bfloat16
I32 = jnp.int32

N_HEADS = 8
HEAD_DIM = 64
ATTN_DIM = N_HEADS * HEAD_DIM
ROPE_THETA = 500000.0
IDX_HEADS = 8
IDX_DIM = 64
TOPK_MAX = 256
CONV_DIM = 512
CONV_WIDTH = 31
HALO = 32
PEER_HEADS = 8
PEER_KEYS = 128
PEER_HALF = 128
PEER_QDIM = 256
PEER_TOPK = 16
PAGE = 128
EPS = 1e-6
LANES = 128
INT_MIN = -(2 ** 31)
INT_MAX = 2 ** 31 - 1
NEG = -1e30
LOG2_E = 1.4426950408889634
NO_RANK = 999.0
COUNT_ROWS = 128
VMEM_LIMIT = 56 * 1024 * 1024

C_GLU, C_Q, C_K, C_V, C_IQ, C_IKW, C_GA, C_GB, C_END = 0, 1024, 1536, 2048, 2560, 3072, 3200, 4224, 5248


def _dg(a, b, ca, cb):
    return lax.dot_general(a, b, (((ca,), (cb,)), ((), ())), preferred_element_type=F32)


def _split(x):
    hi = x.astype(BF16)
    lo = (x - hi.astype(F32)).astype(BF16)
    return hi, lo


def _dot3(ah, al, bh, bl, ca, cb):
    return _dg(ah, bh, ca, cb) + _dg(al, bh, ca, cb) + _dg(ah, bl, ca, cb)


def _sigmoid(x):
    return 1.0 / (1.0 + jnp.exp(-x))


def _const_spec(shape):
    n = len(shape)
    return pl.BlockSpec(shape, lambda *_: (0,) * n, pipeline_mode=pl.Buffered(1))


def _params(sem):
    return pltpu.CompilerParams(dimension_semantics=sem, vmem_limit_bytes=VMEM_LIMIT)


def _sort_key(x):
    bits = lax.bitcast_convert_type(x, I32)
    key = jnp.where(bits < 0, bits ^ jnp.int32(INT_MAX), bits)
    return jnp.where(key == -1, 0, key)


def _ada_kernel(c_ref, w_ref, b_ref, o_ref):
    c = c_ref[...]
    s = c * _sigmoid(c)
    sh, sl = _split(s)
    wh, wl = _split(w_ref[...])
    o_ref[...] = _dot3(sh, sl, wh, wl, 1, 0) + b_ref[...]


def _ada(c, w_ada, b_ada):
    rows, d = c.shape
    n = w_ada.shape[1]
    tn = 1536
    return pl.pallas_call(
        _ada_kernel,
        grid=(n // tn,),
        in_specs=[pl.BlockSpec((rows, d), lambda j: (0, 0)),
                  pl.BlockSpec((d, tn), lambda j: (0, j)),
                  pl.BlockSpec((1, tn), lambda j: (0, j))],
        out_specs=pl.BlockSpec((rows, tn), lambda j: (0, j)),
        out_shape=jax.ShapeDtypeStruct((rows, n), F32),
        compiler_params=_params(("parallel",)),
        name="adaln",
    )(c, w_ada, b_ada.reshape(1, n))


def _tile_lanes(t, width):
    return jnp.concatenate([t] * (width // LANES), axis=1) if width > LANES else t


def _rope(x, c, sa, sb):
    w = x.shape[1]
    return (x * _tile_lanes(c, w) + pltpu.roll(x, w - 8, 1) * _tile_lanes(sa, w)
            + pltpu.roll(x, 8, 1) * _tile_lanes(sb, w))


def _inproj_kernel(x_ref, sc_ref, sh_ref, g_ref, wh_ref, wl_ref, cs_ref, sa_ref, sb_ref,
                   u_ref, q_ref, k_ref, v_ref, kb_ref, vb_ref, iq_ref, ikw_ref, ga_ref, gb_ref):
    x = x_ref[0]
    h = x * lax.rsqrt(jnp.mean(x * x, axis=-1, keepdims=True) + EPS) * g_ref[...]
    h = h * (1.0 + sc_ref[0]) + sh_ref[0]
    hh, hl = _split(h)
    c, sa, sb = cs_ref[...], sa_ref[...], sb_ref[...]

    glu = _dg(hh, wh_ref[:, C_GLU:C_Q], 1, 0)
    u_ref[0] = glu[:, :CONV_DIM] * _sigmoid(glu[:, CONV_DIM:])

    qk = _dg(hh, wh_ref[:, C_Q:C_V], 1, 0)
    q_ref[0] = _rope(qk[:, :ATTN_DIM], c, sa, sb)
    k = _rope(qk[:, ATTN_DIM:], c, sa, sb)
    k_ref[0] = k
    kb_ref[0] = k.astype(BF16)
    v = _dg(hh, wh_ref[:, C_V:C_IQ], 1, 0)
    v_ref[0] = v
    vb_ref[0] = v.astype(BF16)

    wih = wh_ref[:, C_IQ:C_GA]
    idx = _dg(hh, wih, 1, 0) + _dg(hl, wih, 1, 0) + _dg(hh, wl_ref[...], 1, 0)
    iq_ref[0] = _rope(idx[:, :ATTN_DIM], c, sa, sb)
    lane = lax.broadcasted_iota(I32, (1, LANES), 1)
    is_ik = lane < IDX_DIM
    iw_scale = float((IDX_HEADS * IDX_DIM) ** -0.5)
    ikw_ref[0] = _rope(idx[:, ATTN_DIM:], jnp.where(is_ik, c, iw_scale),
                       jnp.where(is_ik, sa, 0.0), jnp.where(is_ik, sb, 0.0))

    ga_ref[0] = _sigmoid(_dg(hh, wh_ref[:, C_GA:C_GB], 1, 0))
    gb_ref[0] = _sigmoid(_dg(hh, wh_ref[:, C_GB:C_END], 1, 0))


def _mod_spec(arr, tm):
    d = arr.shape[-1]
    if arr.shape[1] == 1:
        return pl.BlockSpec((1, 1, d), lambda b, i: (b, 0, 0))
    return pl.BlockSpec((1, tm, d), lambda b, i: (b, i, 0))


def _inproj(x, scale1, shift1, g_mix, wh, wl, cs, sa, sb, tm):
    bx, s, d = x.shape
    row = lambda w: pl.BlockSpec((1, tm, w), lambda b, i: (b, i, 0))
    tab = pl.BlockSpec((tm, LANES), lambda b, i: (i, 0))
    out_w = (CONV_DIM, ATTN_DIM, ATTN_DIM, ATTN_DIM, ATTN_DIM, ATTN_DIM, ATTN_DIM, LANES, d, d)
    out_t = (F32, F32, F32, F32, BF16, BF16, F32, F32, F32, F32)
    return pl.pallas_call(
        _inproj_kernel,
        grid=(bx, s // tm),
        in_specs=[row(d), _mod_spec(scale1, tm), _mod_spec(shift1, tm), _const_spec((1, d)),
                  _const_spec(wh.shape), _const_spec(wl.shape), tab, tab, tab],
        out_specs=[row(w) for w in out_w],
        out_shape=[jax.ShapeDtypeStruct((bx, s, w), t) for w, t in zip(out_w, out_t)],
        compiler_params=_params(("parallel", "parallel")),
        name="inproj",
    )(x, scale1, shift1, g_mix.reshape(1, d), wh, wl, cs, sa, sb)


def _conv_kernel(u_ref, hist_ref, ga_ref, wdw_ref, bdw_ref, lg_ref, lb_ref, wo_ref, o_ref, win_ref, *, t):
    @pl.when(pl.program_id(1) == 0)
    def _():
        win_ref[0:HALO, :] = hist_ref[0]

    win_ref[HALO:HALO + t, :] = u_ref[0]
    off = HALO - (CONV_WIDTH - 1)
    acc = jnp.zeros((t, CONV_DIM), F32)
    for j in range(CONV_WIDTH):
        acc = acc + win_ref[off + j:off + j + t, :] * wdw_ref[j:j + 1, :]
    dw = acc + bdw_ref[...]
    mu = jnp.mean(dw, axis=-1, keepdims=True)
    var = jnp.mean(jnp.square(dw - mu), axis=-1, keepdims=True)
    y = (dw - mu) * lax.rsqrt(var + EPS) * lg_ref[...] + lb_ref[...]
    y = y * _sigmoid(y)
    o_ref[0] = ga_ref[0] * _dg(y.astype(BF16), wo_ref[...], 1, 0)
    tail = win_ref[t:t + HALO, :]
    win_ref[0:HALO, :] = tail


def _conv_branch(u, hist, ga, w_dw, b_dw, ln_g, ln_b, w_out_b, t):
    bx, s, _ = u.shape
    d = ga.shape[-1]
    wdw = jnp.pad(w_dw, ((0, HALO - CONV_WIDTH), (0, 0)))
    vec = lambda a: a.reshape(1, CONV_DIM)
    return pl.pallas_call(
        functools.partial(_conv_kernel, t=t),
        grid=(bx, s // t),
        in_specs=[pl.BlockSpec((1, t, CONV_DIM), lambda b, i: (b, i, 0)),
                  pl.BlockSpec((1, HALO, CONV_DIM), lambda b, i: (b, 0, 0)),
                  pl.BlockSpec((1, t, d), lambda b, i: (b, i, 0)),
                  _const_spec((HALO, CONV_DIM)), _const_spec((1, CONV_DIM)), _const_spec((1, CONV_DIM)),
                  _const_spec((1, CONV_DIM)), _const_spec((CONV_DIM, d))],
        out_specs=pl.BlockSpec((1, t, d), lambda b, i: (b, i, 0)),
        out_shape=jax.ShapeDtypeStruct((bx, s, d), F32),
        scratch_shapes=[pltpu.VMEM((t + HALO, CONV_DIM), F32)],
        compiler_params=_params(("parallel", "arbitrary")),
        name="conv_branch",
    )(u, hist, ga, wdw, vec(b_dw), vec(ln_g), vec(ln_b), w_out_b)


def _select_threshold(count_fn, n_sel, rows, pos_bits, cut_ref):
    def bit_body(it, lo):
        cand = lo + lax.shift_left(jnp.int32(1), 31 - it)
        cnt = count_fn(lambda k, p, c: k >= c, cand)
        return jnp.where(cnt >= n_sel, cand, lo)

    tau = lax.fori_loop(0, 32, bit_body, jnp.full((rows, 1), INT_MIN, I32))
    surplus = count_fn(lambda k, p, c: k >= c, tau) - n_sel
    cut_ref[...] = jnp.full(cut_ref.shape, INT_MAX, I32)

    @pl.when(jnp.max(surplus.astype(F32)) > 0.0)
    def _():
        need = n_sel - count_fn(lambda k, p, c: k > c, tau)

        def pos_body(it, x):
            cand = x + lax.shift_left(jnp.int32(1), pos_bits - 1 - it)
            cnt = count_fn(lambda k, p, c, d: (k == c) & (p < d), tau, cand)
            return jnp.where(cnt < need, cand, x)

        cut = lax.fori_loop(0, pos_bits, pos_body, jnp.zeros((rows, 1), I32))
        cut_ref[...] = jnp.broadcast_to(cut, cut_ref.shape)

    return tau


def _hi_lo_f32(x):
    hi = x.astype(BF16).astype(F32)
    return hi, x - hi


def _dsa_prompt_kernel(iq_ref, ikwq_ref, q_ref, ikw_ref, kb_ref, vb_ref, o_ref,
                       key_ref, ik3_ref, a3_ref, w_ref, q2_ref, cut_ref, s_ref,
                       *, n_sel, qb, kc, s_len):
    i = pl.program_id(1)
    n_ch = (i * qb + qb + kc - 1) // kc
    n_sub = kc // LANES
    lane = lax.broadcasted_iota(I32, (1, LANES), 1)
    low_half = lane < HEAD_DIM

    @pl.when(i == 0)
    def _():
        def body(c, carry):
            blk = ikw_ref[0, pl.ds(pl.multiple_of(c * kc, kc), kc), :]
            hi, lo = _hi_lo_f32(blk)
            first = jnp.where(low_half, hi, pltpu.roll(lo, HEAD_DIM, 1))
            second = jnp.where(low_half, hi, 0.0)
            ik3_ref[c] = jnp.concatenate([first, second], axis=1).astype(BF16)
            return carry
        lax.fori_loop(0, s_len // kc, body, 0)

    iq = iq_ref[0]
    ikwq = ikwq_ref[0]
    q = q_ref[0] * (HEAD_DIM ** -0.5 * LOG2_E)
    for h in range(IDX_HEADS):
        pair = slice((h // 2) * LANES, (h // 2 + 1) * LANES)
        x = iq[:, pair] if h % 2 == 0 else pltpu.roll(iq[:, pair], HEAD_DIM, 1)
        hi, lo = _hi_lo_f32(x)
        first = jnp.where(low_half, hi, pltpu.roll(hi, HEAD_DIM, 1))
        second = jnp.where(low_half, lo, 0.0)
        a3_ref[h] = jnp.concatenate([first, second], axis=1).astype(BF16)
        w_ref[h] = jnp.broadcast_to(ikwq[:, IDX_DIM + h:IDX_DIM + h + 1], (qb, LANES))
        keep = low_half if h % 2 == 0 else jnp.logical_not(low_half)
        q2_ref[h] = jnp.where(keep, q[:, pair], 0.0).astype(BF16)

    qpos = i * qb + lax.broadcasted_iota(I32, (qb, 1), 0)
    sub_lane = lax.broadcasted_iota(I32, (1, kc), 1)

    def score_body(c, carry):
        kk = ik3_ref[c]
        tot = jnp.zeros((qb, kc), F32)
        for h in range(IDX_HEADS):
            s = _dg(a3_ref[h], kk, 1, 1)
            tot = tot + jnp.maximum(s, 0.0) * _tile_lanes(w_ref[h], kc)
        causal = (c * kc + sub_lane) <= qpos
        key_ref[c] = _sort_key(jnp.where(causal, tot, -jnp.inf))
        return carry
    lax.fori_loop(0, n_ch, score_body, 0)

    def count_fn(pred, *thr):
        counts = []
        for r0 in range(0, qb, COUNT_ROWS):
            rs = slice(r0, min(r0 + COUNT_ROWS, qb))
            nrows = rs.stop - rs.start
            thr_b = [jnp.broadcast_to(x[rs], (nrows, LANES)) for x in thr]

            def body(c, acc):
                kblk = key_ref[c, rs, :]
                for j in range(n_sub):
                    pos = c * kc + j * LANES + lane
                    acc = acc + pred(kblk[:, j * LANES:(j + 1) * LANES], pos, *thr_b).astype(I32)
                return acc
            acc = lax.fori_loop(0, n_ch, body, jnp.zeros((nrows, LANES), I32))
            counts.append(jnp.sum(acc, axis=1, keepdims=True))
        return jnp.concatenate(counts, axis=0) if len(counts) > 1 else counts[0]

    tau = _select_threshold(count_fn, n_sel, qb, max(1, (s_len - 1).bit_length()) + 1, cut_ref)

    for r0 in range(0, qb, COUNT_ROWS):
        rs = slice(r0, min(r0 + COUNT_ROWS, qb))
        shape = (rs.stop - rs.start, LANES)
        tau_b = jnp.broadcast_to(tau[rs], shape)
        qpos_b = jnp.broadcast_to(qpos[rs], shape)
        cut_b = cut_ref[rs, :]

        def bias_body(c, carry):
            kblk = key_ref[c, rs, :]
            for j in range(n_sub):
                k = kblk[:, j * LANES:(j + 1) * LANES]
                pos = c * kc + j * LANES + lane
                sel = ((k > tau_b) | ((k == tau_b) & (pos <= cut_b))) & (pos <= qpos_b)
                key_ref[c, rs, j * LANES:(j + 1) * LANES] = lax.bitcast_convert_type(jnp.where(sel, 0.0, NEG), I32)
            return carry
        lax.fori_loop(0, n_ch, bias_body, 0)

    def fold(x, op):
        out = x[:, :LANES]
        for j in range(1, n_sub):
            out = op(out, x[:, j * LANES:(j + 1) * LANES])
        return out

    def attn_body(c, carry):
        ms, ls, accs = carry
        rows = pl.ds(pl.multiple_of(c * kc, kc), kc)
        pair_cols = lambda h: slice((h // 2) * LANES, (h // 2 + 1) * LANES)

        def issue_logits(h):
            s_ref[h % 2] = _dg(q2_ref[h], kb_ref[0, rows, pair_cols(h)], 1, 1)

        issue_logits(0)
        new_m, new_l, new_acc = [], [], []
        for hp in range(N_HEADS // 2):
            vpair = vb_ref[0, rows, pair_cols(2 * hp)]
            pvs, alphas = [], []
            for h in (2 * hp, 2 * hp + 1):
                if h + 1 < N_HEADS:
                    issue_logits(h + 1)
                s = s_ref[h % 2] + lax.bitcast_convert_type(key_ref[c], F32)
                m_new = jnp.maximum(ms[h], jnp.max(fold(s, jnp.maximum), axis=1, keepdims=True))
                p = jnp.exp2(s - _tile_lanes(m_new, kc))
                alpha = jnp.exp2(ms[h] - m_new)
                new_m.append(m_new)
                new_l.append(alpha * ls[h] + fold(p, jnp.add))
                pvs.append(_dg(p.astype(BF16), vpair, 1, 0))
                alphas.append(alpha)
            new_acc.append(jnp.where(low_half, alphas[0], alphas[1]) * accs[hp]
                           + jnp.where(low_half, pvs[0], pvs[1]))
        return tuple(new_m), tuple(new_l), tuple(new_acc)

    zero = jnp.zeros((qb, LANES), F32)
    init = ((jnp.full((qb, LANES), NEG, F32),) * N_HEADS, (zero,) * N_HEADS, (zero,) * (N_HEADS // 2))
    _, ls, accs = lax.fori_loop(0, n_ch, attn_body, init)
    for hp in range(N_HEADS // 2):
        l0 = jnp.sum(ls[2 * hp], axis=1, keepdims=True)
        l1 = jnp.sum(ls[2 * hp + 1], axis=1, keepdims=True)
        o_ref[0, :, hp * LANES:(hp + 1) * LANES] = accs[hp] / jnp.where(low_half, l0, l1)


def _dsa_prompt(iq, ikw, q, kb, vb, qb, kc):
    b, s, _ = q.shape
    n_sel = min(TOPK_MAX, s // 4)
    full = lambda w: pl.BlockSpec((1, s, w), lambda bb, i: (bb, 0, 0), pipeline_mode=pl.Buffered(1))
    blk = lambda w: pl.BlockSpec((1, qb, w), lambda bb, i: (bb, i, 0))
    return pl.pallas_call(
        functools.partial(_dsa_prompt_kernel, n_sel=n_sel, qb=qb, kc=kc, s_len=s),
        grid=(b, s // qb),
        in_specs=[blk(ATTN_DIM), blk(LANES), blk(ATTN_DIM), full(LANES), full(ATTN_DIM), full(ATTN_DIM)],
        out_specs=blk(ATTN_DIM),
        out_shape=jax.ShapeDtypeStruct((b, s, ATTN_DIM), F32),
        scratch_shapes=[pltpu.VMEM((s // kc, qb, kc), I32),
                        pltpu.VMEM((s // kc, kc, 2 * LANES), BF16),
                        pltpu.VMEM((IDX_HEADS, qb, 2 * LANES), BF16),
                        pltpu.VMEM((IDX_HEADS, qb, LANES), F32),
                        pltpu.VMEM((N_HEADS, qb, LANES), BF16),
                        pltpu.VMEM((qb, LANES), I32),
                        pltpu.VMEM((2, qb, kc), F32)],
        compiler_params=_params(("parallel", "arbitrary")),
        name="dsa_prompt",
    )(iq, ikw, q, ikw, kb, vb)


def _sample_select_kernel(pt_ref, qh_ref, ql_ref, w_ref, new_ref, *rest, g, n_grp, n_sel, past, t):
    pages = rest[:g]
    o_ref, key_ref, cur_ref, cut_ref = rest[g:]
    j = pl.program_id(1)
    lane = lax.broadcasted_iota(I32, (1, LANES), 1)
    qh, ql, w = qh_ref[0], ql_ref[0], w_ref[0]

    def scores(ik):
        kh, kl = _split(ik)
        s = _dot3(qh, ql, kh, kl, 1, 1)
        tot = jnp.maximum(s, 0.0) * w
        return jnp.sum(tot.reshape(IDX_HEADS, t, LANES), axis=0)

    @pl.when(j < n_grp)
    def _():
        key_ref[j] = jnp.concatenate([_sort_key(scores(pages[r][0])) for r in range(g)], axis=1)

    @pl.when(j == n_grp)
    def _():
        tpos = lax.broadcasted_iota(I32, (t, 1), 0)
        cur_ref[...] = _sort_key(jnp.where(lane <= tpos, scores(new_ref[0]), -jnp.inf))

        def count_fn(pred, *thr):
            thr_b = [jnp.broadcast_to(x, (t, LANES)) for x in thr]

            def body(c, acc):
                kblk = key_ref[c]
                for r in range(g):
                    pos = (c * g + r) * LANES + lane
                    acc = acc + pred(kblk[:, r * LANES:(r + 1) * LANES], pos, *thr_b).astype(I32)
                return acc
            acc = lax.fori_loop(0, n_grp, body, jnp.zeros((t, LANES), I32))
            acc = acc + pred(cur_ref[...], past + lane, *thr_b).astype(I32)
            return jnp.sum(acc, axis=1, keepdims=True)

        tau = _select_threshold(count_fn, n_sel, t, max(1, (past + LANES - 1).bit_length()) + 1, cut_ref)
        wide_lane = lax.broadcasted_iota(I32, (1, g * LANES), 1)
        cut = cut_ref[...]

        def out_body(c, carry):
            kblk = key_ref[c]
            pos = c * g * LANES + wide_lane
            sel = (kblk > tau) | ((kblk == tau) & (pos <= _tile_lanes(cut, g * LANES)))
            o_ref[0, c] = jnp.where(sel, 0.0, NEG)
            return carry
        lax.fori_loop(0, n_grp, out_body, 0)
        kcur = cur_ref[...]
        pos = past + lane
        sel = ((kcur > tau) | ((kcur == tau) & (pos <= cut))) & (lane <= tpos)
        cur_bias = jnp.where(sel, 0.0, NEG)
        o_ref[0, n_grp] = jnp.concatenate([cur_bias] + [jnp.full((t, LANES), NEG, F32)] * (g - 1), axis=1)


def _page_specs(g, n_grp, layer_base, block_tail):
    nd = len(block_tail)

    def make(r):
        def index_map(b, j, pt):
            return (layer_base + pt[b, jnp.minimum(j, n_grp - 1) * g + r],) + (0,) * nd
        return pl.BlockSpec((1,) + block_tail, index_map)
    return [make(r) for r in range(g)]


def _sample_select(page_table, qh, ql, w, ik_new, idx_pages, layer_base, g, t):
    db, n_pages = page_table.shape
    n_grp = n_pages // g
    past = n_pages * PAGE
    n_sel = min(TOPK_MAX, (past + t) // 4)
    rows = IDX_HEADS * t
    per_b = lambda shape: pl.BlockSpec((1,) + shape, lambda b, j, pt: (b,) + (0,) * len(shape))
    kern = functools.partial(_sample_select_kernel, g=g, n_grp=n_grp, n_sel=n_sel, past=past, t=t)
    return pl.pallas_call(
        kern,
        grid_spec=pltpu.PrefetchScalarGridSpec(
            num_scalar_prefetch=1,
            grid=(db, n_grp + 1),
            in_specs=[per_b((rows, IDX_DIM)), per_b((rows, IDX_DIM)), per_b((rows, LANES)),
                      per_b((PAGE, IDX_DIM))] + _page_specs(g, n_grp, layer_base, (PAGE, IDX_DIM)),
            out_specs=per_b((n_grp + 1, t, g * LANES)),
            scratch_shapes=[pltpu.VMEM((n_grp, t, g * LANES), I32), pltpu.VMEM((t, LANES), I32),
                            pltpu.VMEM((t, LANES), I32)]),
        out_shape=jax.ShapeDtypeStruct((db, n_grp + 1, t, g * LANES), F32),
        compiler_params=_params(("parallel", "arbitrary")),
        name="sample_select",
    )(page_table, qh, ql, w, ik_new, *([idx_pages] * g))


def _sample_attend_kernel(pt_ref, q_ref, bias_ref, knew_ref, vnew_ref, *rest, g, n_grp, t):
    kpages, vpages = rest[:g], rest[g:2 * g]
    o_ref, m_ref, l_ref, acc_ref = rest[2 * g:]
    j = pl.program_id(1)
    rows = N_HEADS * t

    @pl.when(j == 0)
    def _():
        m_ref[...] = jnp.full(m_ref.shape, NEG, F32)
        l_ref[...] = jnp.zeros(l_ref.shape, F32)
        acc_ref[...] = jnp.zeros(acc_ref.shape, F32)

    def step(kmat, vmat, bias):
        n_sub = kmat.shape[0] // LANES
        s = _dg(q_ref[0], kmat.astype(BF16), 1, 1) + jnp.concatenate([bias] * N_HEADS, axis=0)
        m_old = m_ref[...]
        m_new = jnp.maximum(m_old, jnp.max(s, axis=1, keepdims=True))
        p = jnp.exp(s - _tile_lanes(m_new, kmat.shape[0]))
        alpha = jnp.exp(m_old - m_new)
        psum = p[:, :LANES]
        for r in range(1, n_sub):
            psum = psum + p[:, r * LANES:(r + 1) * LANES]
        l_ref[...] = alpha * l_ref[...] + psum
        m_ref[...] = m_new
        acc_ref[...] = (_tile_lanes(alpha, ATTN_DIM) * acc_ref[...]
                        + _dg(p.astype(BF16), vmat.astype(BF16), 1, 0))

    @pl.when(j < n_grp)
    def _():
        step(jnp.concatenate([kp[0] for kp in kpages], axis=0),
             jnp.concatenate([vp[0] for vp in vpages], axis=0), bias_ref[0, 0])

    @pl.when(j == n_grp)
    def _():
        step(knew_ref[0], vnew_ref[0], bias_ref[0, 0][:, :LANES])
        full = acc_ref[...] / jnp.sum(l_ref[...], axis=1, keepdims=True)
        lane = lax.broadcasted_iota(I32, (1, ATTN_DIM), 1)
        out = jnp.zeros((t, ATTN_DIM), F32)
        for h in range(N_HEADS):
            in_head = (lane >= h * HEAD_DIM) & (lane < (h + 1) * HEAD_DIM)
            out = out + jnp.where(in_head, full[h * t:(h + 1) * t, :], 0.0)
        o_ref[0] = out


def _sample_attend(page_table, qrows, bias, k_new, v_new, k_pages, v_pages, layer_base, g, t):
    db, n_pages = page_table.shape
    n_grp = n_pages // g
    rows = N_HEADS * t
    per_b = lambda shape: pl.BlockSpec((1,) + shape, lambda b, j, pt: (b,) + (0,) * len(shape))
    kern = functools.partial(_sample_attend_kernel, g=g, n_grp=n_grp, t=t)
    return pl.pallas_call(
        kern,
        grid_spec=pltpu.PrefetchScalarGridSpec(
            num_scalar_prefetch=1,
            grid=(db, n_grp + 1),
            in_specs=[per_b((rows, ATTN_DIM)),
                      pl.BlockSpec((1, 1, t, g * LANES), lambda b, j, pt: (b, j, 0, 0)),
                      per_b((PAGE, ATTN_DIM)), per_b((PAGE, ATTN_DIM))]
                     + _page_specs(g, n_grp, layer_base, (PAGE, ATTN_DIM))
                     + _page_specs(g, n_grp, layer_base, (PAGE, ATTN_DIM)),
            out_specs=per_b((t, ATTN_DIM)),
            scratch_shapes=[pltpu.VMEM((rows, LANES), F32), pltpu.VMEM((rows, LANES), F32),
                            pltpu.VMEM((rows, ATTN_DIM), F32)]),
        out_shape=jax.ShapeDtypeStruct((db, t, ATTN_DIM), F32),
        compiler_params=_params(("parallel", "arbitrary")),
        name="sample_attend",
    )(page_table, qrows, bias, k_new, v_new, *([k_pages] * g), *([v_pages] * g))


def _mix_kernel(x_ref, ag_ref, gb_ref, at_ref, g1_ref, sc2_ref, sh2_ref, gf_ref,
                wat_ref, wo_ref, wqh_ref, wql_ref, k1h_ref, k1l_ref, k2h_ref, k2l_ref,
                x1_ref, h2_ref, s1_ref, s2_ref):
    bb = _dg(at_ref[0].astype(BF16), wat_ref[...], 1, 0)
    mixed = ag_ref[0] + gb_ref[0] * bb
    x1 = x_ref[0] + g1_ref[0] * _dg(mixed.astype(BF16), wo_ref[...], 1, 0)
    x1_ref[0] = x1
    h2 = x1 * lax.rsqrt(jnp.mean(x1 * x1, axis=-1, keepdims=True) + EPS) * gf_ref[...]
    h2 = h2 * (1.0 + sc2_ref[0]) + sh2_ref[0]
    h2_ref[0] = h2.astype(BF16)
    hh, hl = _split(h2)
    qp = _dot3(hh, hl, wqh_ref[...], wql_ref[...], 1, 0)
    for h in range(PEER_HEADS):
        ah, al = _split(qp[:, h * PEER_QDIM:h * PEER_QDIM + PEER_HALF])
        bh, bl = _split(qp[:, h * PEER_QDIM + PEER_HALF:(h + 1) * PEER_QDIM])
        s1_ref[h] = _dot3(k1h_ref[...], k1l_ref[...], ah, al, 1, 1)
        s2_ref[h] = _dot3(k2h_ref[...], k2l_ref[...], bh, bl, 1, 1)


def _mix(x, ag, gb, attn, gate1, scale2, shift2, g_ffn, wat_b, wo_b, wq_h, wq_l, k1, k2, tm):
    bx, s, d = x.shape
    n = bx * s
    row = lambda w: pl.BlockSpec((1, tm, w), lambda b, i: (b, i, 0))
    tr = pl.BlockSpec((PEER_HEADS, PEER_KEYS, tm), lambda b, i: (0, 0, b * (s // tm) + i))
    k1h, k1l = _split(k1)
    k2h, k2l = _split(k2)
    return pl.pallas_call(
        _mix_kernel,
        grid=(bx, s // tm),
        in_specs=[row(d), row(d), row(d), row(ATTN_DIM), _mod_spec(gate1, tm), _mod_spec(scale2, tm),
                  _mod_spec(shift2, tm), _const_spec((1, d)), _const_spec(wat_b.shape), _const_spec(wo_b.shape),
                  _const_spec(wq_h.shape), _const_spec(wq_l.shape)] + [_const_spec((PEER_KEYS, PEER_HALF))] * 4,
        out_specs=[row(d), row(d), tr, tr],
        out_shape=[jax.ShapeDtypeStruct((bx, s, d), F32), jax.ShapeDtypeStruct((bx, s, d), BF16),
                   jax.ShapeDtypeStruct((PEER_HEADS, PEER_KEYS, n), F32),
                   jax.ShapeDtypeStruct((PEER_HEADS, PEER_KEYS, n), F32)],
        compiler_params=_params(("parallel", "parallel")),
        name="mix_peer_query",
    )(x, ag, gb, attn, gate1, scale2, shift2, g_ffn.reshape(1, d), wat_b, wo_b, wq_h, wq_l, k1h, k1l, k2h, k2l)


A_PER_STEP = 8
PEER_COLS = 256
N_CAND = 80


def _top16(s, break_ties):
    rows = lax.broadcasted_iota(I32, s.shape, 0).astype(F32)
    rank = jnp.full(s.shape, NO_RANK, F32)
    vals = []
    for it in range(PEER_TOPK):
        m = jnp.max(s, axis=0, keepdims=True)
        pick = s == m
        if break_ties:
            pick = rows == jnp.min(jnp.where(pick, rows, float(PEER_KEYS)), axis=0, keepdims=True)
        rank = jnp.where(pick, float(it), rank)
        s = jnp.where(pick, -jnp.inf, s)
        vals.append(m)
    return jnp.concatenate(vals, axis=0), rank


def _peer_select_kernel(s1_ref, s2_ref, e1_ref, jr_ref, e2_ref, r2_ref):
    tn = s1_ref.shape[-1]
    r = lax.broadcasted_iota(I32, (N_CAND, 1), 0)
    mid_i = lax.shift_right_arithmetic(r - 16, 3) + 1
    mid_j = (r - 16) & 7
    flat = jnp.where(r < 16, r, jnp.where(r < 72, mid_i * 16 + mid_j, (r - 64) * 16)).astype(F32)
    limit = jnp.where(mid_i == 1, 8, jnp.where(mid_i == 2, 5, jnp.where(mid_i == 3, 4, jnp.where(mid_i == 4, 3, 2))))
    cell_ok = (r < 16) | (r >= 72) | (mid_j < limit)
    row8 = lax.broadcasted_iota(I32, (8, 1), 0)

    def select_head(h, break_ties):
        s1, s2 = s1_ref[h], s2_ref[h]
        v1, rank1 = _top16(s1, break_ties)
        v2, rank2 = _top16(s2, break_ties)
        groups = [v1[0:1] + v2[0:16]] + [v1[i:i + 1] + v2[0:8] for i in range(1, 8)] + [v1[8:16] + v2[0:1]]
        cand = jnp.where(cell_ok, jnp.concatenate(groups, axis=0), -jnp.inf)
        chosen = jnp.zeros((N_CAND, tn), F32)
        m0 = v1[0:1] + v2[0:1]
        zsum = jnp.zeros((1, tn), F32)
        for it in range(PEER_TOPK):
            m = jnp.max(cand, axis=0, keepdims=True)
            first = jnp.min(jnp.where(cand == m, flat, 4096.0), axis=0, keepdims=True)
            pick = flat == first
            chosen = jnp.where(pick, 1.0, chosen)
            cand = jnp.where(pick, -jnp.inf, cand)
            zsum = zsum + jnp.exp(m - m0)
        j_low = jnp.zeros((8, tn), F32)
        j_low = jnp.where(row8 == 0, jnp.sum(chosen[0:16], axis=0, keepdims=True), j_low)
        for i in range(1, 8):
            j_low = jnp.where(row8 == i, jnp.sum(chosen[8 + 8 * i:16 + 8 * i], axis=0, keepdims=True), j_low)
        j_high = chosen[72:80]
        jr = jnp.zeros((PEER_KEYS, tn), F32)
        for i in range(PEER_TOPK):
            ji = j_low[i:i + 1] if i < 8 else j_high[i - 8:i - 7]
            jr = jnp.where(rank1 == float(i), ji, jr)
        jr_ref[h] = jr
        e1_ref[h] = jnp.where(rank1 < NO_RANK, jnp.exp(s1 - v1[0:1]), 0.0) * (0.5 / zsum)
        e2_ref[h] = jnp.where(rank2 < NO_RANK, jnp.exp(s2 - v2[0:1]), 0.0).astype(BF16)
        r2_ref[h] = rank2.astype(BF16)
        ranked = (rank1 < NO_RANK).astype(F32) + (rank2 < NO_RANK).astype(F32)
        return jnp.max(jnp.sum(ranked, axis=0, keepdims=True))

    def head_body(h, carry):
        n_ranked = select_head(h, False)

        @pl.when(n_ranked > 2.0 * PEER_TOPK)
        def _():
            select_head(h, True)
        return carry
    lax.fori_loop(0, PEER_HEADS, head_body, 0)


def _peer_select(s1t, s2t, tn):
    _, _, n = s1t.shape
    spec = pl.BlockSpec((PEER_HEADS, PEER_KEYS, tn), lambda i: (0, 0, i))
    shape = jax.ShapeDtypeStruct(s1t.shape, F32)
    shape_b = jax.ShapeDtypeStruct(s1t.shape, BF16)
    return pl.pallas_call(
        _peer_select_kernel,
        grid=(n // tn,),
        in_specs=[spec, spec],
        out_specs=[spec] * 4,
        out_shape=[shape, shape, shape_b, shape_b],
        compiler_params=_params(("parallel",)),
        name="peer_select",
    )(s1t, s2t)


def _peer_expert_kernel(h2_ref, x1_ref, g2_ref, gfin_ref, u_ref, vt_ref, e1_ref, jr_ref, e2_ref, r2_ref,
                        y_ref, acc_ref, w_ref, z_ref, p_ref, *, a_per, final_norm):
    j = pl.program_id(2)
    tn = h2_ref.shape[1]
    cols = min(PEER_COLS, tn)
    tiles = [slice(c, c + cols) for c in range(0, tn, cols)]

    @pl.when(j == 0)
    def _():
        acc_ref[...] = jnp.zeros(acc_ref.shape, F32)

    def build_weights(cs):
        for al in range(a_per):
            wsum = jnp.zeros((PEER_KEYS, cols), BF16)
            for h in range(PEER_HEADS):
                jr = jr_ref[h, al:al + 1, cs].astype(BF16)
                e1 = e1_ref[h, al:al + 1, cs].astype(BF16)
                wsum = wsum + jnp.where(r2_ref[h, :, cs] < jr, e2_ref[h, :, cs] * e1, jnp.zeros((), BF16))
            w_ref[al * PEER_KEYS:(al + 1) * PEER_KEYS, cs] = wsum

    def expert_logits(cs):
        z_ref[:, cs] = _dg(u_ref[...], h2_ref[0, cs, :], 1, 1)

    def gate(cs):
        zt = z_ref[:, cs]
        gz = (zt * (1.0 + lax.erf(zt * (2.0 ** -0.5)))).astype(BF16)
        p_ref[:, cs] = w_ref[:, cs] * gz

    def fold(cs):
        acc_ref[:, cs] += _dg(vt_ref[...], p_ref[:, cs], 1, 0)

    expert_logits(tiles[0])
    for k, cs in enumerate(tiles):
        build_weights(cs)
        gate(cs)
        if k + 1 < len(tiles):
            expert_logits(tiles[k + 1])
        fold(cs)

    @pl.when(j == pl.num_programs(2) - 1)
    def _():
        x2 = x1_ref[0] + g2_ref[0] * acc_ref[...].T
        if final_norm:
            x2 = x2 * lax.rsqrt(jnp.mean(x2 * x2, axis=-1, keepdims=True) + EPS) * gfin_ref[...]
        y_ref[0] = x2


def _peer_expert(h2b, x1, gate2, g_final, u_b, vt_b, e1, jr, e2, r2, tn, a_per, final_norm):
    bx, s, d = x1.shape
    n_exp = u_b.shape[0]
    ec = a_per * PEER_KEYS
    tiles = s // tn
    row = lambda w: pl.BlockSpec((1, tn, w), lambda b, i, j: (b, i, 0))
    mod = (pl.BlockSpec((1, 1, d), lambda b, i, j: (b, 0, 0)) if gate2.shape[1] == 1
           else pl.BlockSpec((1, tn, d), lambda b, i, j: (b, i, 0)))
    per_a = pl.BlockSpec((PEER_HEADS, a_per, tn), lambda b, i, j: (0, j, b * tiles + i))
    per_b = pl.BlockSpec((PEER_HEADS, PEER_KEYS, tn), lambda b, i, j: (0, 0, b * tiles + i))
    return pl.pallas_call(
        functools.partial(_peer_expert_kernel, a_per=a_per, final_norm=final_norm),
        grid=(bx, tiles, n_exp // ec),
        in_specs=[row(d), row(d), mod, pl.BlockSpec((1, d), lambda b, i, j: (0, 0)),
                  pl.BlockSpec((ec, d), lambda b, i, j: (j, 0)),
                  pl.BlockSpec((d, ec), lambda b, i, j: (0, j)),
                  per_a, per_a, per_b, per_b],
        out_specs=row(d),
        out_shape=jax.ShapeDtypeStruct((bx, s, d), F32),
        scratch_shapes=[pltpu.VMEM((d, tn), F32), pltpu.VMEM((ec, tn), BF16), pltpu.VMEM((ec, tn), F32),
                        pltpu.VMEM((ec, tn), BF16)],
        compiler_params=_params(("parallel", "parallel", "arbitrary")),
        name="peer_expert",
    )(h2b, x1, gate2, g_final.reshape(1, d), u_b, vt_b, e1, jr, e2, r2)


def _rope_tables(pos):
    rot = HEAD_DIM // 4
    half = rot // 2
    inv_freq = ROPE_THETA ** (-(jnp.arange(half, dtype=F32) * (2.0 / rot)))
    ang = pos.astype(F32)[:, None] * inv_freq[None, :]
    cos, sin = jnp.cos(ang), jnp.sin(ang)
    t = pos.shape[0]
    ones = jnp.ones((t, HEAD_DIM - rot), F32)
    zeros = jnp.zeros((t, HEAD_DIM - rot), F32)
    z8 = jnp.zeros((t, half), F32)
    c = jnp.concatenate([cos, cos, ones], axis=1)
    sa = jnp.concatenate([-sin, z8, zeros], axis=1)
    sb = jnp.concatenate([z8, sin, zeros], axis=1)
    dup = lambda a: jnp.concatenate([a, a], axis=1)
    return dup(c), dup(sa), dup(sb)


def _pack_w_in(w_in):
    d = w_in.shape[0]
    cols = C_IKW - C_GLU
    pad = jnp.zeros((d, LANES - IDX_DIM - IDX_HEADS), w_in.dtype)
    return jnp.concatenate([w_in[:, :cols + IDX_DIM + IDX_HEADS], pad, w_in[:, cols + IDX_DIM + IDX_HEADS:]], axis=1)


def _pick_tile(n, pref):
    t = min(n, pref)
    while n % t:
        t //= 2
    return t


def kernel(x_prompt, x_sample, c_prompt, c_sample, cache_k, cache_v, cache_idx_k, state_conv, page_table, w_ada, b_ada, g_norm_mix, w_in, w_dw, b_dw, ln_conv_g, ln_conv_b, w_conv_out, w_attn_out, w_o, g_norm_ffn, w_peer_q, peer_k1, peer_k2, peer_u, peer_v, g_norm_final):
    b, s, d = x_prompt.shape
    db, t, _ = x_sample.shape
    depth = w_ada.shape[0]
    n_phys = cache_k.shape[1]
    n_pages = page_table.shape[1]
    past = n_pages * PAGE
    ns = db * t
    g_pages = math.gcd(n_pages, 8)

    tabs_p = _rope_tables(jnp.arange(s))
    tabs_s = _rope_tables(jnp.tile(past + jnp.arange(t), db))
    idx_pages = cache_idx_k.reshape(depth * n_phys, PAGE, IDX_DIM)
    k_pages = cache_k.reshape(depth * n_phys, PAGE, ATTN_DIM)
    v_pages = cache_v.reshape(depth * n_phys, PAGE, ATTN_DIM)
    c_all = jnp.concatenate([c_prompt, c_sample], axis=0)
    c_rows = -(-c_all.shape[0] // 8) * 8
    c_all = jnp.pad(c_all, ((0, c_rows - c_all.shape[0]), (0, 0)))

    tm_p, tm_s = _pick_tile(s, 256), _pick_tile(ns, 256)
    tc_p = _pick_tile(s, 256)
    qb = _pick_tile(s, 256)
    kc = _pick_tile(s, 512)
    tn_sel_p, tn_sel_s = _pick_tile(b * s, 256), _pick_tile(ns, 256)
    tn_exp_p, tn_exp_s = _pick_tile(s, 512), _pick_tile(ns, 512)

    xp, xs = x_prompt, x_sample.reshape(1, ns, d)
    outs = [[] for _ in range(8)]
    for l in range(depth):
        mod = _ada(c_all, w_ada[l], b_ada[l])
        mod_p = [m.reshape(b, 1, d) for m in jnp.split(mod[:b], 6, axis=-1)]
        mod_s = [jnp.repeat(m, t, axis=0).reshape(1, ns, d) for m in jnp.split(mod[b:b + db], 6, axis=-1)]
        w_pack = _pack_w_in(w_in[l])
        wh = w_pack.astype(BF16)
        wl = (w_pack[:, C_IQ:C_GA] - wh[:, C_IQ:C_GA].astype(F32)).astype(BF16)
        wco_b, wat_b, wo_b = w_conv_out[l].astype(BF16), w_attn_out[l].astype(BF16), w_o[l].astype(BF16)
        wq_h, wq_l = _split(w_peer_q[l])
        u_b = peer_u[l].astype(BF16)
        vt_b = peer_v[l].astype(BF16).T
        conv_w = (w_dw[l], b_dw[l], ln_conv_g[l], ln_conv_b[l], wco_b)

        u, q, k, v, kb, vb, iq, ikw, ga, gb = _inproj(xp, mod_p[1], mod_p[0], g_norm_mix[l], wh, wl, *tabs_p, tm_p)
        hist0 = jnp.zeros((b, HALO, CONV_DIM), F32)
        ag = _conv_branch(u, hist0, ga, *conv_w, tc_p)
        attn = _dsa_prompt(iq, ikw, q, kb, vb, qb, kc)
        x1, h2b, s1t, s2t = _mix(xp, ag, gb, attn, mod_p[2], mod_p[4], mod_p[3], g_norm_ffn[l],
                                 wat_b, wo_b, wq_h, wq_l, peer_k1[l], peer_k2[l], tm_p)
        sel = _peer_select(s1t, s2t, tn_sel_p)
        last = l == depth - 1
        xp = _peer_expert(h2b, x1, mod_p[5], g_norm_final, u_b, vt_b, *sel, tn_exp_p, A_PER_STEP, last)
        outs[0].append(k.reshape(b, s // PAGE, PAGE, N_HEADS, HEAD_DIM))
        outs[1].append(v.reshape(b, s // PAGE, PAGE, N_HEADS, HEAD_DIM))
        outs[2].append(ikw[:, :, :IDX_DIM].reshape(b, s // PAGE, PAGE, IDX_DIM))
        outs[3].append(jnp.concatenate([hist0[:, HALO - (CONV_WIDTH - 1):], u], axis=1)[:, -(CONV_WIDTH - 1):])

        u, q, k, v, kb, vb, iq, ikw, ga, gb = _inproj(xs, mod_s[1], mod_s[0], g_norm_mix[l], wh, wl, *tabs_s, tm_s)
        u3 = u.reshape(db, t, CONV_DIM)
        hist = jnp.pad(state_conv[l], ((0, 0), (HALO - (CONV_WIDTH - 1), 0), (0, 0)))
        ag = _conv_branch(u3, hist, ga.reshape(db, t, d), *conv_w, t).reshape(1, ns, d)
        iq_ht = iq.reshape(db, t, IDX_HEADS, IDX_DIM).transpose(0, 2, 1, 3).reshape(db, IDX_HEADS * t, IDX_DIM)
        iqh, iql = _split(iq_ht)
        iw = ikw.reshape(db, t, LANES)[:, :, IDX_DIM:IDX_DIM + IDX_HEADS]
        w_rows = jnp.broadcast_to(iw.transpose(0, 2, 1).reshape(db, IDX_HEADS * t, 1), (db, IDX_HEADS * t, LANES))
        pad_rows = lambda a: jnp.pad(a.reshape(db, t, -1), ((0, 0), (0, PAGE - t), (0, 0)))
        bias = _sample_select(page_table, iqh, iql, w_rows, pad_rows(ikw[..., :IDX_DIM]), idx_pages,
                              l * n_phys, g_pages, t)
        q4 = q.reshape(db, t, N_HEADS, HEAD_DIM) * (HEAD_DIM ** -0.5)
        eye = jnp.eye(N_HEADS, dtype=F32)
        qrows = jnp.einsum('bthd,hg->bhtgd', q4, eye).reshape(db, N_HEADS * t, ATTN_DIM).astype(BF16)
        attn = _sample_attend(page_table, qrows, bias, pad_rows(k), pad_rows(v), k_pages, v_pages,
                              l * n_phys, g_pages, t).reshape(1, ns, ATTN_DIM)
        x1, h2b, s1t, s2t = _mix(xs, ag, gb, attn, mod_s[2], mod_s[4], mod_s[3], g_norm_ffn[l],
                                 wat_b, wo_b, wq_h, wq_l, peer_k1[l], peer_k2[l], tm_s)
        sel = _peer_select(s1t, s2t, tn_sel_s)
        xs = _peer_expert(h2b, x1, mod_s[5], g_norm_final, u_b, vt_b, *sel, tn_exp_s, A_PER_STEP, last)
        outs[4].append(k.reshape(db, t, N_HEADS, HEAD_DIM))
        outs[5].append(v.reshape(db, t, N_HEADS, HEAD_DIM))
        outs[6].append(ikw.reshape(db, t, LANES)[:, :, :IDX_DIM])
        outs[7].append(jnp.concatenate([state_conv[l], u3], axis=1)[:, -(CONV_WIDTH - 1):])

    stacked = [jnp.stack(o) for o in outs]
    return (xp, xs.reshape(db, t, d), *stacked)
```

```python
import functools
import math

import jax
import jax.numpy as jnp
from jax import lax
from jax.experimental import pallas as pl
from jax.experimental.pallas import tpu as pltpu

F32 = jnp.float32
BF16 = jnp.bfloat16
I32 = jnp.int32

N_HEADS = 8
HEAD_DIM = 64
ATTN_DIM = N_HEADS * HEAD_DIM
ROPE_THETA = 500000.0
IDX_HEADS = 8
IDX_DIM = 64
TOPK_MAX = 256
CONV_DIM = 512
CONV_WIDTH = 31
HALO = 32
PEER_HEADS = 8
PEER_KEYS = 128
PEER_HALF = 128
PEER_QDIM = 256
PEER_TOPK = 16
PAGE = 128
EPS = 1e-6
LANES = 128
INT_MIN = -(2 ** 31)
INT_MAX = 2 ** 31 - 1
NEG = -1e30
LOG2_E = 1.4426950408889634
NO_RANK = 999.0
COUNT_ROWS = 128
VMEM_LIMIT = 56 * 1024 * 1024

C_GLU, C_Q, C_K, C_V, C_IQ, C_IKW, C_GA, C_GB, C_END = 0, 1024, 1536, 2048, 2560, 3072, 3200, 4224, 5248


def _dg(a, b, ca, cb):
    return lax.dot_general(a, b, (((ca,), (cb,)), ((), ())), preferred_element_type=F32)


def _split(x):
    hi = x.astype(BF16)
    lo = (x - hi.astype(F32)).astype(BF16)
    return hi, lo


def _dot3(ah, al, bh, bl, ca, cb):
    return _dg(ah, bh, ca, cb) + _dg(al, bh, ca, cb) + _dg(ah, bl, ca, cb)


def _sigmoid(x):
    return 1.0 / (1.0 + jnp.exp(-x))


def _const_spec(shape):
    n = len(shape)
    return pl.BlockSpec(shape, lambda *_: (0,) * n, pipeline_mode=pl.Buffered(1))


def _params(sem):
    return pltpu.CompilerParams(dimension_semantics=sem, vmem_limit_bytes=VMEM_LIMIT)


def _sort_key(x):
    bits = lax.bitcast_convert_type(x, I32)
    key = jnp.where(bits < 0, bits ^ jnp.int32(INT_MAX), bits)
    return jnp.where(key == -1, 0, key)


def _ada_kernel(c_ref, w_ref, b_ref, o_ref):
    c = c_ref[...]
    s = c * _sigmoid(c)
    sh, sl = _split(s)
    wh, wl = _split(w_ref[...])
    o_ref[...] = _dot3(sh, sl, wh, wl, 1, 0) + b_ref[...]


def _ada(c, w_ada, b_ada):
    rows, d = c.shape
    n = w_ada.shape[1]
    tn = 1536
    return pl.pallas_call(
        _ada_kernel,
        grid=(n // tn,),
        in_specs=[pl.BlockSpec((rows, d), lambda j: (0, 0)),
                  pl.BlockSpec((d, tn), lambda j: (0, j)),
                  pl.BlockSpec((1, tn), lambda j: (0, j))],
        out_specs=pl.BlockSpec((rows, tn), lambda j: (0, j)),
        out_shape=jax.ShapeDtypeStruct((rows, n), F32),
        compiler_params=_params(("parallel",)),
        name="adaln",
    )(c, w_ada, b_ada.reshape(1, n))


def _tile_lanes(t, width):
    return jnp.concatenate([t] * (width // LANES), axis=1) if width > LANES else t


def _rope(x, c, sa, sb):
    w = x.shape[1]
    return (x * _tile_lanes(c, w) + pltpu.roll(x, w - 8, 1) * _tile_lanes(sa, w)
            + pltpu.roll(x, 8, 1) * _tile_lanes(sb, w))


def _inproj_kernel(x_ref, sc_ref, sh_ref, g_ref, wh_ref, wl_ref, cs_ref, sa_ref, sb_ref,
                   u_ref, q_ref, k_ref, v_ref, kb_ref, vb_ref, iq_ref, ikw_ref, ga_ref, gb_ref):
    x = x_ref[0]
    h = x * lax.rsqrt(jnp.mean(x * x, axis=-1, keepdims=True) + EPS) * g_ref[...]
    h = h * (1.0 + sc_ref[0]) + sh_ref[0]
    hh, hl = _split(h)
    c, sa, sb = cs_ref[...], sa_ref[...], sb_ref[...]

    glu = _dg(hh, wh_ref[:, C_GLU:C_Q], 1, 0)
    u_ref[0] = glu[:, :CONV_DIM] * _sigmoid(glu[:, CONV_DIM:])

    qk = _dg(hh, wh_ref[:, C_Q:C_V], 1, 0)
    q_ref[0] = _rope(qk[:, :ATTN_DIM], c, sa, sb)
    k = _rope(qk[:, ATTN_DIM:], c, sa, sb)
    k_ref[0] = k
    kb_ref[0] = k.astype(BF16)
    v = _dg(hh, wh_ref[:, C_V:C_IQ], 1, 0)
    v_ref[0] = v
    vb_ref[0] = v.astype(BF16)

    wih = wh_ref[:, C_IQ:C_GA]
    idx = _dg(hh, wih, 1, 0) + _dg(hl, wih, 1, 0) + _dg(hh, wl_ref[...], 1, 0)
    iq_ref[0] = _rope(idx[:, :ATTN_DIM], c, sa, sb)
    lane = lax.broadcasted_iota(I32, (1, LANES), 1)
    is_ik = lane < IDX_DIM
    iw_scale = float((IDX_HEADS * IDX_DIM) ** -0.5)
    ikw_ref[0] = _rope(idx[:, ATTN_DIM:], jnp.where(is_ik, c, iw_scale),
                       jnp.where(is_ik, sa, 0.0), jnp.where(is_ik, sb, 0.0))

    ga_ref[0] = _sigmoid(_dg(hh, wh_ref[:, C_GA:C_GB], 1, 0))
    gb_ref[0] = _sigmoid(_dg(hh, wh_ref[:, C_GB:C_END], 1, 0))


def _mod_spec(arr, tm):
    d = arr.shape[-1]
    if arr.shape[1] == 1:
        return pl.BlockSpec((1, 1, d), lambda b, i: (b, 0, 0))
    return pl.BlockSpec((1, tm, d), lambda b, i: (b, i, 0))


def _inproj(x, scale1, shift1, g_mix, wh, wl, cs, sa, sb, tm):
    bx, s, d = x.shape
    row = lambda w: pl.BlockSpec((1, tm, w), lambda b, i: (b, i, 0))
    tab = pl.BlockSpec((tm, LANES), lambda b, i: (i, 0))
    out_w = (CONV_DIM, ATTN_DIM, ATTN_DIM, ATTN_DIM, ATTN_DIM, ATTN_DIM, ATTN_DIM, LANES, d, d)
    out_t = (F32, F32, F32, F32, BF16, BF16, F32, F32, F32, F32)
    return pl.pallas_call(
        _inproj_kernel,
        grid=(bx, s // tm),
        in_specs=[row(d), _mod_spec(scale1, tm), _mod_spec(shift1, tm), _const_spec((1, d)),
                  _const_spec(wh.shape), _const_spec(wl.shape), tab, tab, tab],
        out_specs=[row(w) for w in out_w],
        out_shape=[jax.ShapeDtypeStruct((bx, s, w), t) for w, t in zip(out_w, out_t)],
        compiler_params=_params(("parallel", "parallel")),
        name="inproj",
    )(x, scale1, shift1, g_mix.reshape(1, d), wh, wl, cs, sa, sb)


def _conv_kernel(u_ref, hist_ref, ga_ref, wdw_ref, bdw_ref, lg_ref, lb_ref, wo_ref, o_ref, win_ref, *, t):
    @pl.when(pl.program_id(1) == 0)
    def _():
        win_ref[0:HALO, :] = hist_ref[0]

    win_ref[HALO:HALO + t, :] = u_ref[0]
    off = HALO - (CONV_WIDTH - 1)
    acc = jnp.zeros((t, CONV_DIM), F32)
    for j in range(CONV_WIDTH):
        acc = acc + win_ref[off + j:off + j + t, :] * wdw_ref[j:j + 1, :]
    dw = acc + bdw_ref[...]
    mu = jnp.mean(dw, axis=-1, keepdims=True)
    var = jnp.mean(jnp.square(dw - mu), axis=-1, keepdims=True)
    y = (dw - mu) * lax.rsqrt(var + EPS) * lg_ref[...] + lb_ref[...]
    y = y * _sigmoid(y)
    o_ref[0] = ga_ref[0] * _dg(y.astype(BF16), wo_ref[...], 1, 0)
    tail = win_ref[t:t + HALO, :]
    win_ref[0:HALO, :] = tail


def _conv_branch(u, hist, ga, w_dw, b_dw, ln_g, ln_b, w_out_b, t):
    bx, s, _ = u.shape
    d = ga.shape[-1]
    wdw = jnp.pad(w_dw, ((0, HALO - CONV_WIDTH), (0, 0)))
    vec = lambda a: a.reshape(1, CONV_DIM)
    return pl.pallas_call(
        functools.partial(_conv_kernel, t=t),
        grid=(bx, s // t),
        in_specs=[pl.BlockSpec((1, t, CONV_DIM), lambda b, i: (b, i, 0)),
                  pl.BlockSpec((1, HALO, CONV_DIM), lambda b, i: (b, 0, 0)),
                  pl.BlockSpec((1, t, d), lambda b, i: (b, i, 0)),
                  _const_spec((HALO, CONV_DIM)), _const_spec((1, CONV_DIM)), _const_spec((1, CONV_DIM)),
                  _const_spec((1, CONV_DIM)), _const_spec((CONV_DIM, d))],
        out_specs=pl.BlockSpec((1, t, d), lambda b, i: (b, i, 0)),
        out_shape=jax.ShapeDtypeStruct((bx, s, d), F32),
        scratch_shapes=[pltpu.VMEM((t + HALO, CONV_DIM), F32)],
        compiler_params=_params(("parallel", "arbitrary")),
        name="conv_branch",
    )(u, hist, ga, wdw, vec(b_dw), vec(ln_g), vec(ln_b), w_out_b)


def _select_threshold(count_fn, n_sel, rows, pos_bits, cut_ref):
    def bit_body(it, lo):
        cand = lo + lax.shift_left(jnp.int32(1), 31 - it)
        cnt = count_fn(lambda k, p, c: k >= c, cand)
        return jnp.where(cnt >= n_sel, cand, lo)

    tau = lax.fori_loop(0, 32, bit_body, jnp.full((rows, 1), INT_MIN, I32))
    surplus = count_fn(lambda k, p, c: k >= c, tau) - n_sel
    cut_ref[...] = jnp.full(cut_ref.shape, INT_MAX, I32)

    @pl.when(jnp.max(surplus.astype(F32)) > 0.0)
    def _():
        need = n_sel - count_fn(lambda k, p, c: k > c, tau)

        def pos_body(it, x):
            cand = x + lax.shift_left(jnp.int32(1), pos_bits - 1 - it)
            cnt = count_fn(lambda k, p, c, d: (k == c) & (p < d), tau, cand)
            return jnp.where(cnt < need, cand, x)

        cut = lax.fori_loop(0, pos_bits, pos_body, jnp.zeros((rows, 1), I32))
        cut_ref[...] = jnp.broadcast_to(cut, cut_ref.shape)

    return tau


def _hi_lo_f32(x):
    hi = x.astype(BF16).astype(F32)
    return hi, x - hi


def _dsa_prompt_kernel(iq_ref, ikwq_ref, q_ref, ikw_ref, kb_ref, vb_ref, o_ref,
                       key_ref, ik3_ref, a3_ref, w_ref, q2_ref, cut_ref, s_ref,
                       *, n_sel, qb, kc, s_len):
    i = pl.program_id(1)
    n_ch = (i * qb + qb + kc - 1) // kc
    n_sub = kc // LANES
    lane = lax.broadcasted_iota(I32, (1, LANES), 1)
    low_half = lane < HEAD_DIM

    @pl.when(i == 0)
    def _():
        def body(c, carry):
            blk = ikw_ref[0, pl.ds(pl.multiple_of(c * kc, kc), kc), :]
            hi, lo = _hi_lo_f32(blk)
            first = jnp.where(low_half, hi, pltpu.roll(lo, HEAD_DIM, 1))
            second = jnp.where(low_half, hi, 0.0)
            ik3_ref[c] = jnp.concatenate([first, second], axis=1).astype(BF16)
            return carry
        lax.fori_loop(0, s_len // kc, body, 0)

    iq = iq_ref[0]
    ikwq = ikwq_ref[0]
    q = q_ref[0] * (HEAD_DIM ** -0.5 * LOG2_E)
    for h in range(IDX_HEADS):
        pair = slice((h // 2) * LANES, (h // 2 + 1) * LANES)
        x = iq[:, pair] if h % 2 == 0 else pltpu.roll(iq[:, pair], HEAD_DIM, 1)
        hi, lo = _hi_lo_f32(x)
        first = jnp.where(low_half, hi, pltpu.roll(hi, HEAD_DIM, 1))
        second = jnp.where(low_half, lo, 0.0)
        a3_ref[h] = jnp.concatenate([first, second], axis=1).astype(BF16)
        w_ref[h] = jnp.broadcast_to(ikwq[:, IDX_DIM + h:IDX_DIM + h + 1], (qb, LANES))
        keep = low_half if h % 2 == 0 else jnp.logical_not(low_half)
        q2_ref[h] = jnp.where(keep, q[:, pair], 0.0).astype(BF16)

    qpos = i * qb + lax.broadcasted_iota(I32, (qb, 1), 0)
    sub_lane = lax.broadcasted_iota(I32, (1, kc), 1)

    def score_body(c, carry):
        kk = ik3_ref[c]
        tot = jnp.zeros((qb, kc), F32)
        for h in range(IDX_HEADS):
            s = _dg(a3_ref[h], kk, 1, 1)
            tot = tot + jnp.maximum(s, 0.0) * _tile_lanes(w_ref[h], kc)
        causal = (c * kc + sub_lane) <= qpos
        key_ref[c] = _sort_key(jnp.where(causal, tot, -jnp.inf))
        return carry
    lax.fori_loop(0, n_ch, score_body, 0)

    def count_fn(pred, *thr):
        counts = []
        for r0 in range(0, qb, COUNT_ROWS):
            rs = slice(r0, min(r0 + COUNT_ROWS, qb))
            nrows = rs.stop - rs.start
            thr_b = [jnp.broadcast_to(x[rs], (nrows, LANES)) for x in thr]

            def body(c, acc):
                kblk = key_ref[c, rs, :]
                for j in range(n_sub):
                    pos = c * kc + j * LANES + lane
                    acc = acc + pred(kblk[:, j * LANES:(j + 1) * LANES], pos, *thr_b).astype(I32)
                return acc
            acc = lax.fori_loop(0, n_ch, body, jnp.zeros((nrows, LANES), I32))
            counts.append(jnp.sum(acc, axis=1, keepdims=True))
        return jnp.concatenate(counts, axis=0) if len(counts) > 1 else counts[0]

    tau = _select_threshold(count_fn, n_sel, qb, max(1, (s_len - 1).bit_length()) + 1, cut_ref)

    for r0 in range(0, qb, COUNT_ROWS):
        rs = slice(r0, min(r0 + COUNT_ROWS, qb))
        shape = (rs.stop - rs.start, LANES)
        tau_b = jnp.broadcast_to(tau[rs], shape)
        qpos_b = jnp.broadcast_to(qpos[rs], shape)
        cut_b = cut_ref[rs, :]

        def bias_body(c, carry):
            kblk = key_ref[c, rs, :]
            for j in range(n_sub):
                k = kblk[:, j * LANES:(j + 1) * LANES]
                pos = c * kc + j * LANES + lane
                sel = ((k > tau_b) | ((k == tau_b) & (pos <= cut_b))) & (pos <= qpos_b)
                key_ref[c, rs, j * LANES:(j + 1) * LANES] = lax.bitcast_convert_type(jnp.where(sel, 0.0, NEG), I32)
            return carry
        lax.fori_loop(0, n_ch, bias_body, 0)

    def fold(x, op):
        out = x[:, :LANES]
        for j in range(1, n_sub):
            out = op(out, x[:, j * LANES:(j + 1) * LANES])
        return out

    def attn_body(c, carry):
        ms, ls, accs = carry
        rows = pl.ds(pl.multiple_of(c * kc, kc), kc)
        pair_cols = lambda h: slice((h // 2) * LANES, (h // 2 + 1) * LANES)

        def issue_logits(h):
            s_ref[h % 2] = _dg(q2_ref[h], kb_ref[0, rows, pair_cols(h)], 1, 1)

        issue_logits(0)
        new_m, new_l, new_acc = [], [], []
        for hp in range(N_HEADS // 2):
            vpair = vb_ref[0, rows, pair_cols(2 * hp)]
            pvs, alphas = [], []
            for h in (2 * hp, 2 * hp + 1):
                if h + 1 < N_HEADS:
                    issue_logits(h + 1)
                s = s_ref[h % 2] + lax.bitcast_convert_type(key_ref[c], F32)
                m_new = jnp.maximum(ms[h], jnp.max(fold(s, jnp.maximum), axis=1, keepdims=True))
                p = jnp.exp2(s - _tile_lanes(m_new, kc))
                alpha = jnp.exp2(ms[h] - m_new)
                new_m.append(m_new)
                new_l.append(alpha * ls[h] + fold(p, jnp.add))
                pvs.append(_dg(p.astype(BF16), vpair, 1, 0))
                alphas.append(alpha)
            new_acc.append(jnp.where(low_half, alphas[0], alphas[1]) * accs[hp]
                           + jnp.where(low_half, pvs[0], pvs[1]))
        return tuple(new_m), tuple(new_l), tuple(new_acc)

    zero = jnp.zeros((qb, LANES), F32)
    init = ((jnp.full((qb, LANES), NEG, F32),) * N_HEADS, (zero,) * N_HEADS, (zero,) * (N_HEADS // 2))
    _, ls, accs = lax.fori_loop(0, n_ch, attn_body, init)
    for hp in range(N_HEADS // 2):
        l0 = jnp.sum(ls[2 * hp], axis=1, keepdims=True)
        l1 = jnp.sum(ls[2 * hp + 1], axis=1, keepdims=True)
        o_ref[0, :, hp * LANES:(hp + 1) * LANES] = accs[hp] / jnp.where(low_half, l0, l1)


def _dsa_prompt(iq, ikw, q, kb, vb, qb, kc):
    b, s, _ = q.shape
    n_sel = min(TOPK_MAX, s // 4)
    full = lambda w: pl.BlockSpec((1, s, w), lambda bb, i: (bb, 0, 0), pipeline_mode=pl.Buffered(1))
    blk = lambda w: pl.BlockSpec((1, qb, w), lambda bb, i: (bb, i, 0))
    return pl.pallas_call(
        functools.partial(_dsa_prompt_kernel, n_sel=n_sel, qb=qb, kc=kc, s_len=s),
        grid=(b, s // qb),
        in_specs=[blk(ATTN_DIM), blk(LANES), blk(ATTN_DIM), full(LANES), full(ATTN_DIM), full(ATTN_DIM)],
        out_specs=blk(ATTN_DIM),
        out_shape=jax.ShapeDtypeStruct((b, s, ATTN_DIM), F32),
        scratch_shapes=[pltpu.VMEM((s // kc, qb, kc), I32),
                        pltpu.VMEM((s // kc, kc, 2 * LANES), BF16),
                        pltpu.VMEM((IDX_HEADS, qb, 2 * LANES), BF16),
                        pltpu.VMEM((IDX_HEADS, qb, LANES), F32),
                        pltpu.VMEM((N_HEADS, qb, LANES), BF16),
                        pltpu.VMEM((qb, LANES), I32),
                        pltpu.VMEM((2, qb, kc), F32)],
        compiler_params=_params(("parallel", "arbitrary")),
        name="dsa_prompt",
    )(iq, ikw, q, ikw, kb, vb)


def _sample_select_kernel(pt_ref, qh_ref, ql_ref, w_ref, new_ref, *rest, g, n_grp, n_sel, past, t):
    pages = rest[:g]
    o_ref, key_ref, cur_ref, cut_ref = rest[g:]
    j = pl.program_id(1)
    lane = lax.broadcasted_iota(I32, (1, LANES), 1)
    qh, ql, w = qh_ref[0], ql_ref[0], w_ref[0]

    def scores(ik):
        kh, kl = _split(ik)
        s = _dot3(qh, ql, kh, kl, 1, 1)
        tot = jnp.maximum(s, 0.0) * w
        return jnp.sum(tot.reshape(IDX_HEADS, t, LANES), axis=0)

    @pl.when(j < n_grp)
    def _():
        key_ref[j] = jnp.concatenate([_sort_key(scores(pages[r][0])) for r in range(g)], axis=1)

    @pl.when(j == n_grp)
    def _():
        tpos = lax.broadcasted_iota(I32, (t, 1), 0)
        cur_ref[...] = _sort_key(jnp.where(lane <= tpos, scores(new_ref[0]), -jnp.inf))

        def count_fn(pred, *thr):
            thr_b = [jnp.broadcast_to(x, (t, LANES)) for x in thr]

            def body(c, acc):
                kblk = key_ref[c]
                for r in range(g):
                    pos = (c * g + r) * LANES + lane
                    acc = acc + pred(kblk[:, r * LANES:(r + 1) * LANES], pos, *thr_b).astype(I32)
                return acc
            acc = lax.fori_loop(0, n_grp, body, jnp.zeros((t, LANES), I32))
            acc = acc + pred(cur_ref[...], past + lane, *thr_b).astype(I32)
            return jnp.sum(acc, axis=1, keepdims=True)

        tau = _select_threshold(count_fn, n_sel, t, max(1, (past + LANES - 1).bit_length()) + 1, cut_ref)
        wide_lane = lax.broadcasted_iota(I32, (1, g * LANES), 1)
        cut = cut_ref[...]

        def out_body(c, carry):
            kblk = key_ref[c]
            pos = c * g * LANES + wide_lane
            sel = (kblk > tau) | ((kblk == tau) & (pos <= _tile_lanes(cut, g * LANES)))
            o_ref[0, c] = jnp.where(sel, 0.0, NEG)
            return carry
        lax.fori_loop(0, n_grp, out_body, 0)
        kcur = cur_ref[...]
        pos = past + lane
        sel = ((kcur > tau) | ((kcur == tau) & (pos <= cut))) & (lane <= tpos)
        cur_bias = jnp.where(sel, 0.0, NEG)
        o_ref[0, n_grp] = jnp.concatenate([cur_bias] + [jnp.full((t, LANES), NEG, F32)] * (g - 1), axis=1)


def _page_specs(g, n_grp, layer_base, block_tail):
    nd = len(block_tail)

    def make(r):
        def index_map(b, j, pt):
            return (layer_base + pt[b, jnp.minimum(j, n_grp - 1) * g + r],) + (0,) * nd
        return pl.BlockSpec((1,) + block_tail, index_map)
    return [make(r) for r in range(g)]


def _sample_select(page_table, qh, ql, w, ik_new, idx_pages, layer_base, g, t):
    db, n_pages = page_table.shape
    n_grp = n_pages // g
    past = n_pages * PAGE
    n_sel = min(TOPK_MAX, (past + t) // 4)
    rows = IDX_HEADS * t
    per_b = lambda shape: pl.BlockSpec((1,) + shape, lambda b, j, pt: (b,) + (0,) * len(shape))
    kern = functools.partial(_sample_select_kernel, g=g, n_grp=n_grp, n_sel=n_sel, past=past, t=t)
    return pl.pallas_call(
        kern,
        grid_spec=pltpu.PrefetchScalarGridSpec(
            num_scalar_prefetch=1,
            grid=(db, n_grp + 1),
            in_specs=[per_b((rows, IDX_DIM)), per_b((rows, IDX_DIM)), per_b((rows, LANES)),
                      per_b((PAGE, IDX_DIM))] + _page_specs(g, n_grp, layer_base, (PAGE, IDX_DIM)),
            out_specs=per_b((n_grp + 1, t, g * LANES)),
            scratch_shapes=[pltpu.VMEM((n_grp, t, g * LANES), I32), pltpu.VMEM((t, LANES), I32),
                            pltpu.VMEM((t, LANES), I32)]),
        out_shape=jax.ShapeDtypeStruct((db, n_grp + 1, t, g * LANES), F32),
        compiler_params=_params(("parallel", "arbitrary")),
        name="sample_select",
    )(page_table, qh, ql, w, ik_new, *([idx_pages] * g))


def _sample_attend_kernel(pt_ref, q_ref, bias_ref, knew_ref, vnew_ref, *rest, g, n_grp, t):
    kpages, vpages = rest[:g], rest[g:2 * g]
    o_ref, m_ref, l_ref, acc_ref = rest[2 * g:]
    j = pl.program_id(1)
    rows = N_HEADS * t

    @pl.when(j == 0)
    def _():
        m_ref[...] = jnp.full(m_ref.shape, NEG, F32)
        l_ref[...] = jnp.zeros(l_ref.shape, F32)
        acc_ref[...] = jnp.zeros(acc_ref.shape, F32)

    def step(kmat, vmat, bias):
        n_sub = kmat.shape[0] // LANES
        s = _dg(q_ref[0], kmat.astype(BF16), 1, 1) + jnp.concatenate([bias] * N_HEADS, axis=0)
        m_old = m_ref[...]
        m_new = jnp.maximum(m_old, jnp.max(s, axis=1, keepdims=True))
        p = jnp.exp(s - _tile_lanes(m_new, kmat.shape[0]))
        alpha = jnp.exp(m_old - m_new)
        psum = p[:, :LANES]
        for r in range(1, n_sub):
            psum = psum + p[:, r * LANES:(r + 1) * LANES]
        l_ref[...] = alpha * l_ref[...] + psum
        m_ref[...] = m_new
        acc_ref[...] = (_tile_lanes(alpha, ATTN_DIM) * acc_ref[...]
                        + _dg(p.astype(BF16), vmat.astype(BF16), 1, 0))

    @pl.when(j < n_grp)
    def _():
        step(jnp.concatenate([kp[0] for kp in kpages], axis=0),
             jnp.concatenate([vp[0] for vp in vpages], axis=0), bias_ref[0, 0])

    @pl.when(j == n_grp)
    def _():
        step(knew_ref[0], vnew_ref[0], bias_ref[0, 0][:, :LANES])
        full = acc_ref[...] / jnp.sum(l_ref[...], axis=1, keepdims=True)
        lane = lax.broadcasted_iota(I32, (1, ATTN_DIM), 1)
        out = jnp.zeros((t, ATTN_DIM), F32)
        for h in range(N_HEADS):
            in_head = (lane >= h * HEAD_DIM) & (lane < (h + 1) * HEAD_DIM)
            out = out + jnp.where(in_head, full[h * t:(h + 1) * t, :], 0.0)
        o_ref[0] = out


def _sample_attend(page_table, qrows, bias, k_new, v_new, k_pages, v_pages, layer_base, g, t):
    db, n_pages = page_table.shape
    n_grp = n_pages // g
    rows = N_HEADS * t
    per_b = lambda shape: pl.BlockSpec((1,) + shape, lambda b, j, pt: (b,) + (0,) * len(shape))
    kern = functools.partial(_sample_attend_kernel, g=g, n_grp=n_grp, t=t)
    return pl.pallas_call(
        kern,
        grid_spec=pltpu.PrefetchScalarGridSpec(
            num_scalar_prefetch=1,
            grid=(db, n_grp + 1),
            in_specs=[per_b((rows, ATTN_DIM)),
                      pl.BlockSpec((1, 1, t, g * LANES), lambda b, j, pt: (b, j, 0, 0)),
                      per_b((PAGE, ATTN_DIM)), per_b((PAGE, ATTN_DIM))]
                     + _page_specs(g, n_grp, layer_base, (PAGE, ATTN_DIM))
                     + _page_specs(g, n_grp, layer_base, (PAGE, ATTN_DIM)),
            out_specs=per_b((t, ATTN_DIM)),
            scratch_shapes=[pltpu.VMEM((rows, LANES), F32), pltpu.VMEM((rows, LANES), F32),
                            pltpu.VMEM((rows, ATTN_DIM), F32)]),
        out_shape=jax.ShapeDtypeStruct((db, t, ATTN_DIM), F32),
        compiler_params=_params(("parallel", "arbitrary")),
        name="sample_attend",
    )(page_table, qrows, bias, k_new, v_new, *([k_pages] * g), *([v_pages] * g))


def _mix_kernel(x_ref, ag_ref, gb_ref, at_ref, g1_ref, sc2_ref, sh2_ref, gf_ref,
                wat_ref, wo_ref, wqh_ref, wql_ref, k1h_ref, k1l_ref, k2h_ref, k2l_ref,
                x1_ref, h2_ref, s1_ref, s2_ref):
    bb = _dg(at_ref[0].astype(BF16), wat_ref[...], 1, 0)
    mixed = ag_ref[0] + gb_ref[0] * bb
    x1 = x_ref[0] + g1_ref[0] * _dg(mixed.astype(BF16), wo_ref[...], 1, 0)
    x1_ref[0] = x1
    h2 = x1 * lax.rsqrt(jnp.mean(x1 * x1, axis=-1, keepdims=True) + EPS) * gf_ref[...]
    h2 = h2 * (1.0 + sc2_ref[0]) + sh2_ref[0]
    h2_ref[0] = h2.astype(BF16)
    hh, hl = _split(h2)
    qp = _dot3(hh, hl, wqh_ref[...], wql_ref[...], 1, 0)
    for h in range(PEER_HEADS):
        ah, al = _split(qp[:, h * PEER_QDIM:h * PEER_QDIM + PEER_HALF])
        bh, bl = _split(qp[:, h * PEER_QDIM + PEER_HALF:(h + 1) * PEER_QDIM])
        s1_ref[h] = _dot3(k1h_ref[...], k1l_ref[...], ah, al, 1, 1)
        s2_ref[h] = _dot3(k2h_ref[...], k2l_ref[...], bh, bl, 1, 1)


def _mix(x, ag, gb, attn, gate1, scale2, shift2, g_ffn, wat_b, wo_b, wq_h, wq_l, k1, k2, tm):
    bx, s, d = x.shape
    n = bx * s
    row = lambda w: pl.BlockSpec((1, tm, w), lambda b, i: (b, i, 0))
    tr = pl.BlockSpec((PEER_HEADS, PEER_KEYS, tm), lambda b, i: (0, 0, b * (s // tm) + i))
    k1h, k1l = _split(k1)
    k2h, k2l = _split(k2)
    return pl.pallas_call(
        _mix_kernel,
        grid=(bx, s // tm),
        in_specs=[row(d), row(d), row(d), row(ATTN_DIM), _mod_spec(gate1, tm), _mod_spec(scale2, tm),
                  _mod_spec(shift2, tm), _const_spec((1, d)), _const_spec(wat_b.shape), _const_spec(wo_b.shape),
                  _const_spec(wq_h.shape), _const_spec(wq_l.shape)] + [_const_spec((PEER_KEYS, PEER_HALF))] * 4,
        out_specs=[row(d), row(d), tr, tr],
        out_shape=[jax.ShapeDtypeStruct((bx, s, d), F32), jax.ShapeDtypeStruct((bx, s, d), BF16),
                   jax.ShapeDtypeStruct((PEER_HEADS, PEER_KEYS, n), F32),
                   jax.ShapeDtypeStruct((PEER_HEADS, PEER_KEYS, n), F32)],
        compiler_params=_params(("parallel", "parallel")),
        name="mix_peer_query",
    )(x, ag, gb, attn, gate1, scale2, shift2, g_ffn.reshape(1, d), wat_b, wo_b, wq_h, wq_l, k1h, k1l, k2h, k2l)


A_PER_STEP = 16
PAGES_PER_STEP = 16
PEER_COLS = 256
N_CAND = 80


def _top16(s, break_ties):
    rows = lax.broadcasted_iota(I32, s.shape, 0).astype(F32)
    rank = jnp.full(s.shape, NO_RANK, F32)
    vals = []
    for it in range(PEER_TOPK):
        m = jnp.max(s, axis=0, keepdims=True)
        pick = s == m
        if break_ties:
            pick = rows == jnp.min(jnp.where(pick, rows, float(PEER_KEYS)), axis=0, keepdims=True)
        rank = jnp.where(pick, float(it), rank)
        s = jnp.where(pick, -jnp.inf, s)
        vals.append(m)
    return jnp.concatenate(vals, axis=0), rank


def _peer_select_kernel(s1_ref, s2_ref, e1_ref, jr_ref, e2_ref, r2_ref):
    tn = s1_ref.shape[-1]
    r = lax.broadcasted_iota(I32, (N_CAND, 1), 0)
    mid_i = lax.shift_right_arithmetic(r - 16, 3) + 1
    mid_j = (r - 16) & 7
    flat = jnp.where(r < 16, r, jnp.where(r < 72, mid_i * 16 + mid_j, (r - 64) * 16)).astype(F32)
    limit = jnp.where(mid_i == 1, 8, jnp.where(mid_i == 2, 5, jnp.where(mid_i == 3, 4, jnp.where(mid_i == 4, 3, 2))))
    cell_ok = (r < 16) | (r >= 72) | (mid_j < limit)
    row8 = lax.broadcasted_iota(I32, (8, 1), 0)

    def select_head(h, break_ties):
        s1, s2 = s1_ref[h], s2_ref[h]
        v1, rank1 = _top16(s1, break_ties)
        v2, rank2 = _top16(s2, break_ties)
        groups = [v1[0:1] + v2[0:16]] + [v1[i:i + 1] + v2[0:8] for i in range(1, 8)] + [v1[8:16] + v2[0:1]]
        cand = jnp.where(cell_ok, jnp.concatenate(groups, axis=0), -jnp.inf)
        chosen = jnp.zeros((N_CAND, tn), F32)
        m0 = v1[0:1] + v2[0:1]
        zsum = jnp.zeros((1, tn), F32)
        for it in range(PEER_TOPK):
            m = jnp.max(cand, axis=0, keepdims=True)
            first = jnp.min(jnp.where(cand == m, flat, 4096.0), axis=0, keepdims=True)
            pick = flat == first
            chosen = jnp.where(pick, 1.0, chosen)
            cand = jnp.where(pick, -jnp.inf, cand)
            zsum = zsum + jnp.exp(m - m0)
        j_low = jnp.zeros((8, tn), F32)
        j_low = jnp.where(row8 == 0, jnp.sum(chosen[0:16], axis=0, keepdims=True), j_low)
        for i in range(1, 8):
            j_low = jnp.where(row8 == i, jnp.sum(chosen[8 + 8 * i:16 + 8 * i], axis=0, keepdims=True), j_low)
        j_high = chosen[72:80]
        jr = jnp.zeros((PEER_KEYS, tn), F32)
        for i in range(PEER_TOPK):
            ji = j_low[i:i + 1] if i < 8 else j_high[i - 8:i - 7]
            jr = jnp.where(rank1 == float(i), ji, jr)
        jr_ref[h] = jr
        e1_ref[h] = jnp.where(rank1 < NO_RANK, jnp.exp(s1 - v1[0:1]), 0.0) * (0.5 / zsum)
        e2_ref[h] = jnp.where(rank2 < NO_RANK, jnp.exp(s2 - v2[0:1]), 0.0).astype(BF16)
        r2_ref[h] = rank2.astype(BF16)
        ranked = (rank1 < NO_RANK).astype(F32) + (rank2 < NO_RANK).astype(F32)
        return jnp.max(jnp.sum(ranked, axis=0, keepdims=True))

    def head_body(h, carry):
        n_ranked = select_head(h, False)

        @pl.when(n_ranked > 2.0 * PEER_TOPK)
        def _():
            select_head(h, True)
        return carry
    lax.fori_loop(0, PEER_HEADS, head_body, 0)


def _peer_select(s1t, s2t, tn):
    _, _, n = s1t.shape
    spec = pl.BlockSpec((PEER_HEADS, PEER_KEYS, tn), lambda i: (0, 0, i))
    shape = jax.ShapeDtypeStruct(s1t.shape, F32)
    shape_b = jax.ShapeDtypeStruct(s1t.shape, BF16)
    return pl.pallas_call(
        _peer_select_kernel,
        grid=(n // tn,),
        in_specs=[spec, spec],
        out_specs=[spec] * 4,
        out_shape=[shape, shape, shape_b, shape_b],
        compiler_params=_params(("parallel",)),
        name="peer_select",
    )(s1t, s2t)


def _peer_expert_kernel(h2_ref, x1_ref, g2_ref, gfin_ref, u_ref, vt_ref, e1_ref, jr_ref, e2_ref, r2_ref,
                        y_ref, acc_ref, w_ref, z_ref, p_ref, *, a_per, final_norm):
    j = pl.program_id(2)
    tn = h2_ref.shape[1]
    cols = min(PEER_COLS, tn)
    tiles = [slice(c, c + cols) for c in range(0, tn, cols)]

    @pl.when(j == 0)
    def _():
        acc_ref[...] = jnp.zeros(acc_ref.shape, F32)

    def build_weights(cs):
        for al in range(a_per):
            wsum = jnp.zeros((PEER_KEYS, cols), BF16)
            for h in range(PEER_HEADS):
                jr = jr_ref[h, al:al + 1, cs].astype(BF16)
                e1 = e1_ref[h, al:al + 1, cs].astype(BF16)
                wsum = wsum + jnp.where(r2_ref[h, :, cs] < jr, e2_ref[h, :, cs] * e1, jnp.zeros((), BF16))
            w_ref[al * PEER_KEYS:(al + 1) * PEER_KEYS, cs] = wsum

    def expert_logits(cs):
        z_ref[:, cs] = _dg(u_ref[...], h2_ref[0, cs, :], 1, 1)

    def gate(cs):
        zt = z_ref[:, cs]
        gz = (zt * (1.0 + lax.erf(zt * (2.0 ** -0.5)))).astype(BF16)
        p_ref[:, cs] = w_ref[:, cs] * gz

    def fold(cs):
        acc_ref[:, cs] += _dg(vt_ref[...], p_ref[:, cs], 1, 0)

    expert_logits(tiles[0])
    for k, cs in enumerate(tiles):
        build_weights(cs)
        gate(cs)
        if k + 1 < len(tiles):
            expert_logits(tiles[k + 1])
        fold(cs)

    @pl.when(j == pl.num_programs(2) - 1)
    def _():
        x2 = x1_ref[0] + g2_ref[0] * acc_ref[...].T
        if final_norm:
            x2 = x2 * lax.rsqrt(jnp.mean(x2 * x2, axis=-1, keepdims=True) + EPS) * gfin_ref[...]
        y_ref[0] = x2


def _peer_expert(h2b, x1, gate2, g_final, u_b, vt_b, e1, jr, e2, r2, tn, a_per, final_norm):
    bx, s, d = x1.shape
    n_exp = u_b.shape[0]
    ec = a_per * PEER_KEYS
    tiles = s // tn
    row = lambda w: pl.BlockSpec((1, tn, w), lambda b, i, j: (b, i, 0))
    mod = (pl.BlockSpec((1, 1, d), lambda b, i, j: (b, 0, 0)) if gate2.shape[1] == 1
           else pl.BlockSpec((1, tn, d), lambda b, i, j: (b, i, 0)))
    per_a = pl.BlockSpec((PEER_HEADS, a_per, tn), lambda b, i, j: (0, j, b * tiles + i))
    per_b = pl.BlockSpec((PEER_HEADS, PEER_KEYS, tn), lambda b, i, j: (0, 0, b * tiles + i))
    return pl.pallas_call(
        functools.partial(_peer_expert_kernel, a_per=a_per, final_norm=final_norm),
        grid=(bx, tiles, n_exp // ec),
        in_specs=[row(d), row(d), mod, pl.BlockSpec((1, d), lambda b, i, j: (0, 0)),
                  pl.BlockSpec((ec, d), lambda b, i, j: (j, 0)),
                  pl.BlockSpec((d, ec), lambda b, i, j: (0, j)),
                  per_a, per_a, per_b, per_b],
        out_specs=row(d),
        out_shape=jax.ShapeDtypeStruct((bx, s, d), F32),
        scratch_shapes=[pltpu.VMEM((d, tn), F32), pltpu.VMEM((ec, tn), BF16), pltpu.VMEM((ec, tn), F32),
                        pltpu.VMEM((ec, tn), BF16)],
        compiler_params=_params(("parallel", "parallel", "arbitrary")),
        name="peer_expert",
    )(h2b, x1, gate2, g_final.reshape(1, d), u_b, vt_b, e1, jr, e2, r2)


def _rope_tables(pos):
    rot = HEAD_DIM // 4
    half = rot // 2
    inv_freq = ROPE_THETA ** (-(jnp.arange(half, dtype=F32) * (2.0 / rot)))
    ang = pos.astype(F32)[:, None] * inv_freq[None, :]
    cos, sin = jnp.cos(ang), jnp.sin(ang)
    t = pos.shape[0]
    ones = jnp.ones((t, HEAD_DIM - rot), F32)
    zeros = jnp.zeros((t, HEAD_DIM - rot), F32)
    z8 = jnp.zeros((t, half), F32)
    c = jnp.concatenate([cos, cos, ones], axis=1)
    sa = jnp.concatenate([-sin, z8, zeros], axis=1)
    sb = jnp.concatenate([z8, sin, zeros], axis=1)
    dup = lambda a: jnp.concatenate([a, a], axis=1)
    return dup(c), dup(sa), dup(sb)


def _pack_w_in(w_in):
    d = w_in.shape[0]
    cols = C_IKW - C_GLU
    pad = jnp.zeros((d, LANES - IDX_DIM - IDX_HEADS), w_in.dtype)
    return jnp.concatenate([w_in[:, :cols + IDX_DIM + IDX_HEADS], pad, w_in[:, cols + IDX_DIM + IDX_HEADS:]], axis=1)


def _pick_tile(n, pref):
    t = min(n, pref)
    while n % t:
        t //= 2
    return t


def kernel(x_prompt, x_sample, c_prompt, c_sample, cache_k, cache_v, cache_idx_k, state_conv, page_table, w_ada, b_ada, g_norm_mix, w_in, w_dw, b_dw, ln_conv_g, ln_conv_b, w_conv_out, w_attn_out, w_o, g_norm_ffn, w_peer_q, peer_k1, peer_k2, peer_u, peer_v, g_norm_final):
    b, s, d = x_prompt.shape
    db, t, _ = x_sample.shape
    depth = w_ada.shape[0]
    n_phys = cache_k.shape[1]
    n_pages = page_table.shape[1]
    past = n_pages * PAGE
    ns = db * t
    g_pages = math.gcd(n_pages, PAGES_PER_STEP)

    tabs_p = _rope_tables(jnp.arange(s))
    tabs_s = _rope_tables(jnp.tile(past + jnp.arange(t), db))
    idx_pages = cache_idx_k.reshape(depth * n_phys, PAGE, IDX_DIM)
    k_pages = cache_k.reshape(depth * n_phys, PAGE, ATTN_DIM)
    v_pages = cache_v.reshape(depth * n_phys, PAGE, ATTN_DIM)
    c_all = jnp.concatenate([c_prompt, c_sample], axis=0)
    c_rows = -(-c_all.shape[0] // 8) * 8
    c_all = jnp.pad(c_all, ((0, c_rows - c_all.shape[0]), (0, 0)))

    tm_p, tm_s = _pick_tile(s, 256), _pick_tile(ns, 256)
    tc_p = _pick_tile(s, 256)
    qb = _pick_tile(s, 256)
    kc = _pick_tile(s, 512)
    tn_sel_p, tn_sel_s = _pick_tile(b * s, 256), _pick_tile(ns, 256)
    tn_exp_p, tn_exp_s = _pick_tile(s, 512), _pick_tile(ns, 512)

    xp, xs = x_prompt, x_sample.reshape(1, ns, d)
    outs = [[] for _ in range(8)]
    for l in range(depth):
        mod = _ada(c_all, w_ada[l], b_ada[l])
        mod_p = [m.reshape(b, 1, d) for m in jnp.split(mod[:b], 6, axis=-1)]
        mod_s = [jnp.repeat(m, t, axis=0).reshape(1, ns, d) for m in jnp.split(mod[b:b + db], 6, axis=-1)]
        w_pack = _pack_w_in(w_in[l])
        wh = w_pack.astype(BF16)
        wl = (w_pack[:, C_IQ:C_GA] - wh[:, C_IQ:C_GA].astype(F32)).astype(BF16)
        wco_b, wat_b, wo_b = w_conv_out[l].astype(BF16), w_attn_out[l].astype(BF16), w_o[l].astype(BF16)
        wq_h, wq_l = _split(w_peer_q[l])
        u_b = peer_u[l].astype(BF16)
        vt_b = peer_v[l].astype(BF16).T
        conv_w = (w_dw[l], b_dw[l], ln_conv_g[l], ln_conv_b[l], wco_b)

        u, q, k, v, kb, vb, iq, ikw, ga, gb = _inproj(xp, mod_p[1], mod_p[0], g_norm_mix[l], wh, wl, *tabs_p, tm_p)
        hist0 = jnp.zeros((b, HALO, CONV_DIM), F32)
        ag = _conv_branch(u, hist0, ga, *conv_w, tc_p)
        attn = _dsa_prompt(iq, ikw, q, kb, vb, qb, kc)
        x1, h2b, s1t, s2t = _mix(xp, ag, gb, attn, mod_p[2], mod_p[4], mod_p[3], g_norm_ffn[l],
                                 wat_b, wo_b, wq_h, wq_l, peer_k1[l], peer_k2[l], tm_p)
        sel = _peer_select(s1t, s2t, tn_sel_p)
        last = l == depth - 1
        xp = _peer_expert(h2b, x1, mod_p[5], g_norm_final, u_b, vt_b, *sel, tn_exp_p, A_PER_STEP, last)
        outs[0].append(k.reshape(b, s // PAGE, PAGE, N_HEADS, HEAD_DIM))
        outs[1].append(v.reshape(b, s // PAGE, PAGE, N_HEADS, HEAD_DIM))
        outs[2].append(ikw[:, :, :IDX_DIM].reshape(b, s // PAGE, PAGE, IDX_DIM))
        outs[3].append(jnp.concatenate([hist0[:, HALO - (CONV_WIDTH - 1):], u], axis=1)[:, -(CONV_WIDTH - 1):])

        u, q, k, v, kb, vb, iq, ikw, ga, gb = _inproj(xs, mod_s[1], mod_s[0], g_norm_mix[l], wh, wl, *tabs_s, tm_s)
        u3 = u.reshape(db, t, CONV_DIM)
        hist = jnp.pad(state_conv[l], ((0, 0), (HALO - (CONV_WIDTH - 1), 0), (0, 0)))
        ag = _conv_branch(u3, hist, ga.reshape(db, t, d), *conv_w, t).reshape(1, ns, d)
        iq_ht = iq.reshape(db, t, IDX_HEADS, IDX_DIM).transpose(0, 2, 1, 3).reshape(db, IDX_HEADS * t, IDX_DIM)
        iqh, iql = _split(iq_ht)
        iw = ikw.reshape(db, t, LANES)[:, :, IDX_DIM:IDX_DIM + IDX_HEADS]
        w_rows = jnp.broadcast_to(iw.transpose(0, 2, 1).reshape(db, IDX_HEADS * t, 1), (db, IDX_HEADS * t, LANES))
        pad_rows = lambda a: jnp.pad(a.reshape(db, t, -1), ((0, 0), (0, PAGE - t), (0, 0)))
        bias = _sample_select(page_table, iqh, iql, w_rows, pad_rows(ikw[..., :IDX_DIM]), idx_pages,
                              l * n_phys, g_pages, t)
        q4 = q.reshape(db, t, N_HEADS, HEAD_DIM) * (HEAD_DIM ** -0.5)
        eye = jnp.eye(N_HEADS, dtype=F32)
        qrows = jnp.einsum('bthd,hg->bhtgd', q4, eye).reshape(db, N_HEADS * t, ATTN_DIM).astype(BF16)
        attn = _sample_attend(page_table, qrows, bias, pad_rows(k), pad_rows(v), k_pages, v_pages,
                              l * n_phys, g_pages, t).reshape(1, ns, ATTN_DIM)
        x1, h2b, s1t, s2t = _mix(xs, ag, gb, attn, mod_s[2], mod_s[4], mod_s[3], g_norm_ffn[l],
                                 wat_b, wo_b, wq_h, wq_l, peer_k1[l], peer_k2[l], tm_s)
        sel = _peer_select(s1t, s2t, tn_sel_s)
        xs = _peer_expert(h2b, x1, mod_s[5], g_norm_final, u_b, vt_b, *sel, tn_exp_s, A_PER_STEP, last)
        outs[4].append(k.reshape(db, t, N_HEADS, HEAD_DIM))
        outs[5].append(v.reshape(db, t, N_HEADS, HEAD_DIM))
        outs[6].append(ikw.reshape(db, t, LANES)[:, :, :IDX_DIM])
        outs[7].append(jnp.concatenate([state_conv[l], u3], axis=1)[:, -(CONV_WIDTH - 1):])

    stacked = [jnp.stack(o) for o in outs]
    return (xp, xs.reshape(db, t, d), *stacked)
```

```python
import functools
import math

import jax
import jax.numpy as jnp
from jax import lax
from jax.experimental import pallas as pl
from jax.experimental.pallas import tpu as pltpu

F32 = jnp.float32
BF16 = jnp.bfloat16
I32 = jnp.int32

N_HEADS = 8
HEAD_DIM = 64
ATTN_DIM = N_HEADS * HEAD_DIM
ROPE_THETA = 500000.0
IDX_HEADS = 8
IDX_DIM = 64
TOPK_MAX = 256
CONV_DIM = 512
CONV_WIDTH = 31
HALO = 32
PEER_HEADS = 8
PEER_KEYS = 128
PEER_HALF = 128
PEER_QDIM = 256
PEER_TOPK = 16
PAGE = 128
EPS = 1e-6
LANES = 128
INT_MIN = -(2 ** 31)
INT_MAX = 2 ** 31 - 1
NEG = -1e30
LOG2_E = 1.4426950408889634
NO_RANK = 999.0
COUNT_ROWS = 128
VMEM_LIMIT = 56 * 1024 * 1024

C_GLU, C_Q, C_K, C_V, C_IQ, C_IKW, C_GA, C_GB, C_END = 0, 1024, 1536, 2048, 2560, 3072, 3200, 4224, 5248


def _dg(a, b, ca, cb):
    return lax.dot_general(a, b, (((ca,), (cb,)), ((), ())), preferred_element_type=F32)


def _split(x):
    hi = x.astype(BF16)
    lo = (x - hi.astype(F32)).astype(BF16)
    return hi, lo


def _dot3(ah, al, bh, bl, ca, cb):
    return _dg(ah, bh, ca, cb) + _dg(al, bh, ca, cb) + _dg(ah, bl, ca, cb)


def _sigmoid(x):
    return 1.0 / (1.0 + jnp.exp(-x))


def _const_spec(shape):
    n = len(shape)
    return pl.BlockSpec(shape, lambda *_: (0,) * n, pipeline_mode=pl.Buffered(1))


def _params(sem):
    return pltpu.CompilerParams(dimension_semantics=sem, vmem_limit_bytes=VMEM_LIMIT)


def _sort_key(x):
    bits = lax.bitcast_convert_type(x, I32)
    key = jnp.where(bits < 0, bits ^ jnp.int32(INT_MAX), bits)
    return jnp.where(key == -1, 0, key)


def _ada_kernel(c_ref, w_ref, b_ref, o_ref):
    c = c_ref[...]
    s = c * _sigmoid(c)
    sh, sl = _split(s)
    wh, wl = _split(w_ref[...])
    o_ref[...] = _dot3(sh, sl, wh, wl, 1, 0) + b_ref[...]


def _ada(c, w_ada, b_ada):
    rows, d = c.shape
    n = w_ada.shape[1]
    tn = 1536
    return pl.pallas_call(
        _ada_kernel,
        grid=(n // tn,),
        in_specs=[pl.BlockSpec((rows, d), lambda j: (0, 0)),
                  pl.BlockSpec((d, tn), lambda j: (0, j)),
                  pl.BlockSpec((1, tn), lambda j: (0, j))],
        out_specs=pl.BlockSpec((rows, tn), lambda j: (0, j)),
        out_shape=jax.ShapeDtypeStruct((rows, n), F32),
        compiler_params=_params(("parallel",)),
        name="adaln",
    )(c, w_ada, b_ada.reshape(1, n))


def _tile_lanes(t, width):
    return jnp.concatenate([t] * (width // LANES), axis=1) if width > LANES else t


def _rope(x, c, sa, sb):
    w = x.shape[1]
    return (x * _tile_lanes(c, w) + pltpu.roll(x, w - 8, 1) * _tile_lanes(sa, w)
            + pltpu.roll(x, 8, 1) * _tile_lanes(sb, w))


def _inproj_kernel(x_ref, sc_ref, sh_ref, g_ref, wh_ref, wl_ref, cs_ref, sa_ref, sb_ref,
                   u_ref, q_ref, k_ref, v_ref, kb_ref, vb_ref, iq_ref, ikw_ref, ga_ref, gb_ref):
    x = x_ref[0]
    h = x * lax.rsqrt(jnp.mean(x * x, axis=-1, keepdims=True) + EPS) * g_ref[...]
    h = h * (1.0 + sc_ref[0]) + sh_ref[0]
    hh, hl = _split(h)
    c, sa, sb = cs_ref[...], sa_ref[...], sb_ref[...]

    glu = _dg(hh, wh_ref[:, C_GLU:C_Q], 1, 0)
    u_ref[0] = glu[:, :CONV_DIM] * _sigmoid(glu[:, CONV_DIM:])

    qk = _dg(hh, wh_ref[:, C_Q:C_V], 1, 0)
    q_ref[0] = _rope(qk[:, :ATTN_DIM], c, sa, sb)
    k = _rope(qk[:, ATTN_DIM:], c, sa, sb)
    k_ref[0] = k
    kb_ref[0] = k.astype(BF16)
    v = _dg(hh, wh_ref[:, C_V:C_IQ], 1, 0)
    v_ref[0] = v
    vb_ref[0] = v.astype(BF16)

    wih = wh_ref[:, C_IQ:C_GA]
    idx = _dg(hh, wih, 1, 0) + _dg(hl, wih, 1, 0) + _dg(hh, wl_ref[...], 1, 0)
    iq_ref[0] = _rope(idx[:, :ATTN_DIM], c, sa, sb)
    lane = lax.broadcasted_iota(I32, (1, LANES), 1)
    is_ik = lane < IDX_DIM
    iw_scale = float((IDX_HEADS * IDX_DIM) ** -0.5)
    ikw_ref[0] = _rope(idx[:, ATTN_DIM:], jnp.where(is_ik, c, iw_scale),
                       jnp.where(is_ik, sa, 0.0), jnp.where(is_ik, sb, 0.0))

    ga_ref[0] = _sigmoid(_dg(hh, wh_ref[:, C_GA:C_GB], 1, 0))
    gb_ref[0] = _sigmoid(_dg(hh, wh_ref[:, C_GB:C_END], 1, 0))


def _mod_spec(arr, tm):
    d = arr.shape[-1]
    if arr.shape[1] == 1:
        return pl.BlockSpec((1, 1, d), lambda b, i: (b, 0, 0))
    return pl.BlockSpec((1, tm, d), lambda b, i: (b, i, 0))


def _inproj(x, scale1, shift1, g_mix, wh, wl, cs, sa, sb, tm):
    bx, s, d = x.shape
    row = lambda w: pl.BlockSpec((1, tm, w), lambda b, i: (b, i, 0))
    tab = pl.BlockSpec((tm, LANES), lambda b, i: (i, 0))
    out_w = (CONV_DIM, ATTN_DIM, ATTN_DIM, ATTN_DIM, ATTN_DIM, ATTN_DIM, ATTN_DIM, LANES, d, d)
    out_t = (F32, F32, F32, F32, BF16, BF16, F32, F32, F32, F32)
    return pl.pallas_call(
        _inproj_kernel,
        grid=(bx, s // tm),
        in_specs=[row(d), _mod_spec(scale1, tm), _mod_spec(shift1, tm), _const_spec((1, d)),
                  _const_spec(wh.shape), _const_spec(wl.shape), tab, tab, tab],
        out_specs=[row(w) for w in out_w],
        out_shape=[jax.ShapeDtypeStruct((bx, s, w), t) for w, t in zip(out_w, out_t)],
        compiler_params=_params(("parallel", "parallel")),
        name="inproj",
    )(x, scale1, shift1, g_mix.reshape(1, d), wh, wl, cs, sa, sb)


def _conv_kernel(u_ref, hist_ref, ga_ref, wdw_ref, bdw_ref, lg_ref, lb_ref, wo_ref, o_ref, win_ref, *, t):
    @pl.when(pl.program_id(1) == 0)
    def _():
        win_ref[0:HALO, :] = hist_ref[0]

    win_ref[HALO:HALO + t, :] = u_ref[0]
    off = HALO - (CONV_WIDTH - 1)
    acc = jnp.zeros((t, CONV_DIM), F32)
    for j in range(CONV_WIDTH):
        acc = acc + win_ref[off + j:off + j + t, :] * wdw_ref[j:j + 1, :]
    dw = acc + bdw_ref[...]
    mu = jnp.mean(dw, axis=-1, keepdims=True)
    var = jnp.mean(jnp.square(dw - mu), axis=-1, keepdims=True)
    y = (dw - mu) * lax.rsqrt(var + EPS) * lg_ref[...] + lb_ref[...]
    y = y * _sigmoid(y)
    o_ref[0] = ga_ref[0] * _dg(y.astype(BF16), wo_ref[...], 1, 0)
    tail = win_ref[t:t + HALO, :]
    win_ref[0:HALO, :] = tail


def _conv_branch(u, hist, ga, w_dw, b_dw, ln_g, ln_b, w_out_b, t):
    bx, s, _ = u.shape
    d = ga.shape[-1]
    wdw = jnp.pad(w_dw, ((0, HALO - CONV_WIDTH), (0, 0)))
    vec = lambda a: a.reshape(1, CONV_DIM)
    return pl.pallas_call(
        functools.partial(_conv_kernel, t=t),
        grid=(bx, s // t),
        in_specs=[pl.BlockSpec((1, t, CONV_DIM), lambda b, i: (b, i, 0)),
                  pl.BlockSpec((1, HALO, CONV_DIM), lambda b, i: (b, 0, 0)),
                  pl.BlockSpec((1, t, d), lambda b, i: (b, i, 0)),
                  _const_spec((HALO, CONV_DIM)), _const_spec((1, CONV_DIM)), _const_spec((1, CONV_DIM)),
                  _const_spec((1, CONV_DIM)), _const_spec((CONV_DIM, d))],
        out_specs=pl.BlockSpec((1, t, d), lambda b, i: (b, i, 0)),
        out_shape=jax.ShapeDtypeStruct((bx, s, d), F32),
        scratch_shapes=[pltpu.VMEM((t + HALO, CONV_DIM), F32)],
        compiler_params=_params(("parallel", "arbitrary")),
        name="conv_branch",
    )(u, hist, ga, wdw, vec(b_dw), vec(ln_g), vec(ln_b), w_out_b)


def _select_threshold(count_fn, n_sel, rows, pos_bits, cut_ref):
    def bit_body(it, lo):
        cand = lo + lax.shift_left(jnp.int32(1), 31 - it)
        cnt = count_fn(lambda k, p, c: k >= c, cand)
        return jnp.where(cnt >= n_sel, cand, lo)

    tau = lax.fori_loop(0, 32, bit_body, jnp.full((rows, 1), INT_MIN, I32))
    surplus = count_fn(lambda k, p, c: k >= c, tau) - n_sel
    cut_ref[...] = jnp.full(cut_ref.shape, INT_MAX, I32)

    @pl.when(jnp.max(surplus.astype(F32)) > 0.0)
    def _():
        need = n_sel - count_fn(lambda k, p, c: k > c, tau)

        def pos_body(it, x):
            cand = x + lax.shift_left(jnp.int32(1), pos_bits - 1 - it)
            cnt = count_fn(lambda k, p, c, d: (k == c) & (p < d), tau, cand)
            return jnp.where(cnt < need, cand, x)

        cut = lax.fori_loop(0, pos_bits, pos_body, jnp.zeros((rows, 1), I32))
        cut_ref[...] = jnp.broadcast_to(cut, cut_ref.shape)

    return tau


def _hi_lo_f32(x):
    hi = x.astype(BF16).astype(F32)
    return hi, x - hi


def _dsa_prompt_kernel(iq_ref, ikwq_ref, q_ref, ikw_ref, kb_ref, vb_ref, o_ref,
                       key_ref, ik3_ref, a3_ref, w_ref, q2_ref, cut_ref, s_ref,
                       *, n_sel, qb, kc, s_len):
    i = pl.program_id(1)
    n_ch = (i * qb + qb + kc - 1) // kc
    n_sub = kc // LANES
    lane = lax.broadcasted_iota(I32, (1, LANES), 1)
    low_half = lane < HEAD_DIM

    @pl.when(i == 0)
    def _():
        def body(c, carry):
            blk = ikw_ref[0, pl.ds(pl.multiple_of(c * kc, kc), kc), :]
            hi, lo = _hi_lo_f32(blk)
            first = jnp.where(low_half, hi, pltpu.roll(lo, HEAD_DIM, 1))
            second = jnp.where(low_half, hi, 0.0)
            ik3_ref[c] = jnp.concatenate([first, second], axis=1).astype(BF16)
            return carry
        lax.fori_loop(0, s_len // kc, body, 0)

    iq = iq_ref[0]
    ikwq = ikwq_ref[0]
    q = q_ref[0] * (HEAD_DIM ** -0.5 * LOG2_E)
    for h in range(IDX_HEADS):
        pair = slice((h // 2) * LANES, (h // 2 + 1) * LANES)
        x = iq[:, pair] if h % 2 == 0 else pltpu.roll(iq[:, pair], HEAD_DIM, 1)
        hi, lo = _hi_lo_f32(x)
        first = jnp.where(low_half, hi, pltpu.roll(hi, HEAD_DIM, 1))
        second = jnp.where(low_half, lo, 0.0)
        a3_ref[h] = jnp.concatenate([first, second], axis=1).astype(BF16)
        w_ref[h] = jnp.broadcast_to(ikwq[:, IDX_DIM + h:IDX_DIM + h + 1], (qb, LANES))
        keep = low_half if h % 2 == 0 else jnp.logical_not(low_half)
        q2_ref[h] = jnp.where(keep, q[:, pair], 0.0).astype(BF16)

    qpos = i * qb + lax.broadcasted_iota(I32, (qb, 1), 0)
    sub_lane = lax.broadcasted_iota(I32, (1, kc), 1)

    def score_body(c, carry):
        kk = ik3_ref[c]
        tot = jnp.zeros((qb, kc), F32)
        for h in range(IDX_HEADS):
            s = _dg(a3_ref[h], kk, 1, 1)
            tot = tot + jnp.maximum(s, 0.0) * _tile_lanes(w_ref[h], kc)
        causal = (c * kc + sub_lane) <= qpos
        key_ref[c] = _sort_key(jnp.where(causal, tot, -jnp.inf))
        return carry
    lax.fori_loop(0, n_ch, score_body, 0)

    def count_fn(pred, *thr):
        counts = []
        for r0 in range(0, qb, COUNT_ROWS):
            rs = slice(r0, min(r0 + COUNT_ROWS, qb))
            nrows = rs.stop - rs.start
            thr_b = [jnp.broadcast_to(x[rs], (nrows, LANES)) for x in thr]

            def body(c, acc):
                kblk = key_ref[c, rs, :]
                for j in range(n_sub):
                    pos = c * kc + j * LANES + lane
                    acc = acc + pred(kblk[:, j * LANES:(j + 1) * LANES], pos, *thr_b).astype(I32)
                return acc
            acc = lax.fori_loop(0, n_ch, body, jnp.zeros((nrows, LANES), I32))
            counts.append(jnp.sum(acc, axis=1, keepdims=True))
        return jnp.concatenate(counts, axis=0) if len(counts) > 1 else counts[0]

    tau = _select_threshold(count_fn, n_sel, qb, max(1, (s_len - 1).bit_length()) + 1, cut_ref)

    for r0 in range(0, qb, COUNT_ROWS):
        rs = slice(r0, min(r0 + COUNT_ROWS, qb))
        shape = (rs.stop - rs.start, LANES)
        tau_b = jnp.broadcast_to(tau[rs], shape)
        qpos_b = jnp.broadcast_to(qpos[rs], shape)
        cut_b = cut_ref[rs, :]

        def bias_body(c, carry):
            kblk = key_ref[c, rs, :]
            for j in range(n_sub):
                k = kblk[:, j * LANES:(j + 1) * LANES]
                pos = c * kc + j * LANES + lane
                sel = ((k > tau_b) | ((k == tau_b) & (pos <= cut_b))) & (pos <= qpos_b)
                key_ref[c, rs, j * LANES:(j + 1) * LANES] = lax.bitcast_convert_type(jnp.where(sel, 0.0, NEG), I32)
            return carry
        lax.fori_loop(0, n_ch, bias_body, 0)

    def fold(x, op):
        out = x[:, :LANES]
        for j in range(1, n_sub):
            out = op(out, x[:, j * LANES:(j + 1) * LANES])
        return out

    def attn_body(c, carry):
        ms, ls, accs = carry
        rows = pl.ds(pl.multiple_of(c * kc, kc), kc)
        pair_cols = lambda h: slice((h // 2) * LANES, (h // 2 + 1) * LANES)

        def issue_logits(h):
            s_ref[h % 2] = _dg(q2_ref[h], kb_ref[0, rows, pair_cols(h)], 1, 1)

        issue_logits(0)
        new_m, new_l, new_acc = [], [], []
        for hp in range(N_HEADS // 2):
            vpair = vb_ref[0, rows, pair_cols(2 * hp)]
            pvs, alphas = [], []
            for h in (2 * hp, 2 * hp + 1):
                if h + 1 < N_HEADS:
                    issue_logits(h + 1)
                s = s_ref[h % 2] + lax.bitcast_convert_type(key_ref[c], F32)
                m_new = jnp.maximum(ms[h], jnp.max(fold(s, jnp.maximum), axis=1, keepdims=True))
                p = jnp.exp2(s - _tile_lanes(m_new, kc))
                alpha = jnp.exp2(ms[h] - m_new)
                new_m.append(m_new)
                new_l.append(alpha * ls[h] + fold(p, jnp.add))
                pvs.append(_dg(p.astype(BF16), vpair, 1, 0))
                alphas.append(alpha)
            new_acc.append(jnp.where(low_half, alphas[0], alphas[1]) * accs[hp]
                           + jnp.where(low_half, pvs[0], pvs[1]))
        return tuple(new_m), tuple(new_l), tuple(new_acc)

    zero = jnp.zeros((qb, LANES), F32)
    init = ((jnp.full((qb, LANES), NEG, F32),) * N_HEADS, (zero,) * N_HEADS, (zero,) * (N_HEADS // 2))
    _, ls, accs = lax.fori_loop(0, n_ch, attn_body, init)
    for hp in range(N_HEADS // 2):
        l0 = jnp.sum(ls[2 * hp], axis=1, keepdims=True)
        l1 = jnp.sum(ls[2 * hp + 1], axis=1, keepdims=True)
        o_ref[0, :, hp * LANES:(hp + 1) * LANES] = accs[hp] / jnp.where(low_half, l0, l1)


def _dsa_prompt(iq, ikw, q, kb, vb, qb, kc):
    b, s, _ = q.shape
    n_sel = min(TOPK_MAX, s // 4)
    full = lambda w: pl.BlockSpec((1, s, w), lambda bb, i: (bb, 0, 0), pipeline_mode=pl.Buffered(1))
    blk = lambda w: pl.BlockSpec((1, qb, w), lambda bb, i: (bb, i, 0))
    return pl.pallas_call(
        functools.partial(_dsa_prompt_kernel, n_sel=n_sel, qb=qb, kc=kc, s_len=s),
        grid=(b, s // qb),
        in_specs=[blk(ATTN_DIM), blk(LANES), blk(ATTN_DIM), full(LANES), full(ATTN_DIM), full(ATTN_DIM)],
        out_specs=blk(ATTN_DIM),
        out_shape=jax.ShapeDtypeStruct((b, s, ATTN_DIM), F32),
        scratch_shapes=[pltpu.VMEM((s // kc, qb, kc), I32),
                        pltpu.VMEM((s // kc, kc, 2 * LANES), BF16),
                        pltpu.VMEM((IDX_HEADS, qb, 2 * LANES), BF16),
                        pltpu.VMEM((IDX_HEADS, qb, LANES), F32),
                        pltpu.VMEM((N_HEADS, qb, LANES), BF16),
                        pltpu.VMEM((qb, LANES), I32),
                        pltpu.VMEM((2, qb, kc), F32)],
        compiler_params=_params(("parallel", "arbitrary")),
        name="dsa_prompt",
    )(iq, ikw, q, ikw, kb, vb)


def _sample_select_kernel(pt_ref, qh_ref, ql_ref, w_ref, new_ref, *rest, g, n_grp, n_sel, past, t):
    pages = rest[:g]
    o_ref, key_ref, cur_ref, cut_ref = rest[g:]
    j = pl.program_id(1)
    lane = lax.broadcasted_iota(I32, (1, LANES), 1)
    qh, ql, w = qh_ref[0], ql_ref[0], w_ref[0]

    def scores(ik):
        kh, kl = _split(ik)
        s = _dot3(qh, ql, kh, kl, 1, 1)
        tot = jnp.maximum(s, 0.0) * w
        return jnp.sum(tot.reshape(IDX_HEADS, t, LANES), axis=0)

    @pl.when(j < n_grp)
    def _():
        key_ref[j] = jnp.concatenate([_sort_key(scores(pages[r][0])) for r in range(g)], axis=1)

    @pl.when(j == n_grp)
    def _():
        tpos = lax.broadcasted_iota(I32, (t, 1), 0)
        cur_ref[...] = _sort_key(jnp.where(lane <= tpos, scores(new_ref[0]), -jnp.inf))

        def count_fn(pred, *thr):
            thr_b = [jnp.broadcast_to(x, (t, LANES)) for x in thr]

            def body(c, acc):
                kblk = key_ref[c]
                for r in range(g):
                    pos = (c * g + r) * LANES + lane
                    acc = acc + pred(kblk[:, r * LANES:(r + 1) * LANES], pos, *thr_b).astype(I32)
                return acc
            acc = lax.fori_loop(0, n_grp, body, jnp.zeros((t, LANES), I32))
            acc = acc + pred(cur_ref[...], past + lane, *thr_b).astype(I32)
            return jnp.sum(acc, axis=1, keepdims=True)

        tau = _select_threshold(count_fn, n_sel, t, max(1, (past + LANES - 1).bit_length()) + 1, cut_ref)
        wide_lane = lax.broadcasted_iota(I32, (1, g * LANES), 1)
        cut = cut_ref[...]

        def out_body(c, carry):
            kblk = key_ref[c]
            pos = c * g * LANES + wide_lane
            sel = (kblk > tau) | ((kblk == tau) & (pos <= _tile_lanes(cut, g * LANES)))
            o_ref[0, c] = jnp.where(sel, 0.0, NEG)
            return carry
        lax.fori_loop(0, n_grp, out_body, 0)
        kcur = cur_ref[...]
        pos = past + lane
        sel = ((kcur > tau) | ((kcur == tau) & (pos <= cut))) & (lane <= tpos)
        cur_bias = jnp.where(sel, 0.0, NEG)
        o_ref[0, n_grp] = jnp.concatenate([cur_bias] + [jnp.full((t, LANES), NEG, F32)] * (g - 1), axis=1)


def _page_specs(g, n_grp, layer_base, block_tail):
    nd = len(block_tail)

    def make(r):
        def index_map(b, j, pt):
            return (layer_base + pt[b, jnp.minimum(j, n_grp - 1) * g + r],) + (0,) * nd
        return pl.BlockSpec((1,) + block_tail, index_map)
    return [make(r) for r in range(g)]


def _sample_select(page_table, qh, ql, w, ik_new, idx_pages, layer_base, g, t):
    db, n_pages = page_table.shape
    n_grp = n_pages // g
    past = n_pages * PAGE
    n_sel = min(TOPK_MAX, (past + t) // 4)
    rows = IDX_HEADS * t
    per_b = lambda shape: pl.BlockSpec((1,) + shape, lambda b, j, pt: (b,) + (0,) * len(shape))
    kern = functools.partial(_sample_select_kernel, g=g, n_grp=n_grp, n_sel=n_sel, past=past, t=t)
    return pl.pallas_call(
        kern,
        grid_spec=pltpu.PrefetchScalarGridSpec(
            num_scalar_prefetch=1,
            grid=(db, n_grp + 1),
            in_specs=[per_b((rows, IDX_DIM)), per_b((rows, IDX_DIM)), per_b((rows, LANES)),
                      per_b((PAGE, IDX_DIM))] + _page_specs(g, n_grp, layer_base, (PAGE, IDX_DIM)),
            out_specs=per_b((n_grp + 1, t, g * LANES)),
            scratch_shapes=[pltpu.VMEM((n_grp, t, g * LANES), I32), pltpu.VMEM((t, LANES), I32),
                            pltpu.VMEM((t, LANES), I32)]),
        out_shape=jax.ShapeDtypeStruct((db, n_grp + 1, t, g * LANES), F32),
        compiler_params=_params(("parallel", "arbitrary")),
        name="sample_select",
    )(page_table, qh, ql, w, ik_new, *([idx_pages] * g))


def _sample_attend_kernel(pt_ref, q_ref, bias_ref, knew_ref, vnew_ref, *rest, g, n_grp, t):
    kpages, vpages = rest[:g], rest[g:2 * g]
    o_ref, m_ref, l_ref, acc_ref = rest[2 * g:]
    j = pl.program_id(1)
    rows = N_HEADS * t

    @pl.when(j == 0)
    def _():
        m_ref[...] = jnp.full(m_ref.shape, NEG, F32)
        l_ref[...] = jnp.zeros(l_ref.shape, F32)
        acc_ref[...] = jnp.zeros(acc_ref.shape, F32)

    def step(kmat, vmat, bias):
        n_sub = kmat.shape[0] // LANES
        s = _dg(q_ref[0], kmat.astype(BF16), 1, 1) + jnp.concatenate([bias] * N_HEADS, axis=0)
        m_old = m_ref[...]
        m_new = jnp.maximum(m_old, jnp.max(s, axis=1, keepdims=True))
        p = jnp.exp(s - _tile_lanes(m_new, kmat.shape[0]))
        alpha = jnp.exp(m_old - m_new)
        psum = p[:, :LANES]
        for r in range(1, n_sub):
            psum = psum + p[:, r * LANES:(r + 1) * LANES]
        l_ref[...] = alpha * l_ref[...] + psum
        m_ref[...] = m_new
        acc_ref[...] = (_tile_lanes(alpha, ATTN_DIM) * acc_ref[...]
                        + _dg(p.astype(BF16), vmat.astype(BF16), 1, 0))

    @pl.when(j < n_grp)
    def _():
        step(jnp.concatenate([kp[0] for kp in kpages], axis=0),
             jnp.concatenate([vp[0] for vp in vpages], axis=0), bias_ref[0, 0])

    @pl.when(j == n_grp)
    def _():
        step(knew_ref[0], vnew_ref[0], bias_ref[0, 0][:, :LANES])
        full = acc_ref[...] / jnp.sum(l_ref[...], axis=1, keepdims=True)
        lane = lax.broadcasted_iota(I32, (1, ATTN_DIM), 1)
        out = jnp.zeros((t, ATTN_DIM), F32)
        for h in range(N_HEADS):
            in_head = (lane >= h * HEAD_DIM) & (lane < (h + 1) * HEAD_DIM)
            out = out + jnp.where(in_head, full[h * t:(h + 1) * t, :], 0.0)
        o_ref[0] = out


def _sample_attend(page_table, qrows, bias, k_new, v_new, k_pages, v_pages, layer_base, g, t):
    db, n_pages = page_table.shape
    n_grp = n_pages // g
    rows = N_HEADS * t
    per_b = lambda shape: pl.BlockSpec((1,) + shape, lambda b, j, pt: (b,) + (0,) * len(shape))
    kern = functools.partial(_sample_attend_kernel, g=g, n_grp=n_grp, t=t)
    return pl.pallas_call(
        kern,
        grid_spec=pltpu.PrefetchScalarGridSpec(
            num_scalar_prefetch=1,
            grid=(db, n_grp + 1),
            in_specs=[per_b((rows, ATTN_DIM)),
                      pl.BlockSpec((1, 1, t, g * LANES), lambda b, j, pt: (b, j, 0, 0)),
                      per_b((PAGE, ATTN_DIM)), per_b((PAGE, ATTN_DIM))]
                     + _page_specs(g, n_grp, layer_base, (PAGE, ATTN_DIM))
                     + _page_specs(g, n_grp, layer_base, (PAGE, ATTN_DIM)),
            out_specs=per_b((t, ATTN_DIM)),
            scratch_shapes=[pltpu.VMEM((rows, LANES), F32), pltpu.VMEM((rows, LANES), F32),
                            pltpu.VMEM((rows, ATTN_DIM), F32)]),
        out_shape=jax.ShapeDtypeStruct((db, t, ATTN_DIM), F32),
        compiler_params=_params(("parallel", "arbitrary")),
        name="sample_attend",
    )(page_table, qrows, bias, k_new, v_new, *([k_pages] * g), *([v_pages] * g))


def _mix_kernel(x_ref, ag_ref, gb_ref, at_ref, g1_ref, sc2_ref, sh2_ref, gf_ref,
                wat_ref, wo_ref, wqh_ref, wql_ref, k1h_ref, k1l_ref, k2h_ref, k2l_ref,
                x1_ref, h2_ref, s1_ref, s2_ref):
    bb = _dg(at_ref[0].astype(BF16), wat_ref[...], 1, 0)
    mixed = ag_ref[0] + gb_ref[0] * bb
    x1 = x_ref[0] + g1_ref[0] * _dg(mixed.astype(BF16), wo_ref[...], 1, 0)
    x1_ref[0] = x1
    h2 = x1 * lax.rsqrt(jnp.mean(x1 * x1, axis=-1, keepdims=True) + EPS) * gf_ref[...]
    h2 = h2 * (1.0 + sc2_ref[0]) + sh2_ref[0]
    h2_ref[0] = h2.astype(BF16)
    hh, hl = _split(h2)
    qp = _dot3(hh, hl, wqh_ref[...], wql_ref[...], 1, 0)
    for h in range(PEER_HEADS):
        ah, al = _split(qp[:, h * PEER_QDIM:h * PEER_QDIM + PEER_HALF])
        bh, bl = _split(qp[:, h * PEER_QDIM + PEER_HALF:(h + 1) * PEER_QDIM])
        s1_ref[h] = _dot3(k1h_ref[...], k1l_ref[...], ah, al, 1, 1)
        s2_ref[h] = _dot3(k2h_ref[...], k2l_ref[...], bh, bl, 1, 1)


def _mix(x, ag, gb, attn, gate1, scale2, shift2, g_ffn, wat_b, wo_b, wq_h, wq_l, k1, k2, tm):
    bx, s, d = x.shape
    n = bx * s
    row = lambda w: pl.BlockSpec((1, tm, w), lambda b, i: (b, i, 0))
    tr = pl.BlockSpec((PEER_HEADS, PEER_KEYS, tm), lambda b, i: (0, 0, b * (s // tm) + i))
    k1h, k1l = _split(k1)
    k2h, k2l = _split(k2)
    return pl.pallas_call(
        _mix_kernel,
        grid=(bx, s // tm),
        in_specs=[row(d), row(d), row(d), row(ATTN_DIM), _mod_spec(gate1, tm), _mod_spec(scale2, tm),
                  _mod_spec(shift2, tm), _const_spec((1, d)), _const_spec(wat_b.shape), _const_spec(wo_b.shape),
                  _const_spec(wq_h.shape), _const_spec(wq_l.shape)] + [_const_spec((PEER_KEYS, PEER_HALF))] * 4,
        out_specs=[row(d), row(d), tr, tr],
        out_shape=[jax.ShapeDtypeStruct((bx, s, d), F32), jax.ShapeDtypeStruct((bx, s, d), BF16),
                   jax.ShapeDtypeStruct((PEER_HEADS, PEER_KEYS, n), F32),
                   jax.ShapeDtypeStruct((PEER_HEADS, PEER_KEYS, n), F32)],
        compiler_params=_params(("parallel", "parallel")),
        name="mix_peer_query",
    )(x, ag, gb, attn, gate1, scale2, shift2, g_ffn.reshape(1, d), wat_b, wo_b, wq_h, wq_l, k1h, k1l, k2h, k2l)


A_PER_STEP = 16
PAGES_PER_STEP = 16
PEER_COLS = 256
N_CAND = 80


def _top16(s, break_ties):
    rows = lax.broadcasted_iota(I32, s.shape, 0).astype(F32)
    rank = jnp.full(s.shape, NO_RANK, F32)
    vals = []
    for it in range(PEER_TOPK):
        m = jnp.max(s, axis=0, keepdims=True)
        pick = s == m
        if break_ties:
            pick = rows == jnp.min(jnp.where(pick, rows, float(PEER_KEYS)), axis=0, keepdims=True)
        rank = jnp.where(pick, float(it), rank)
        s = jnp.where(pick, -jnp.inf, s)
        vals.append(m)
    return jnp.concatenate(vals, axis=0), rank


def _peer_select_kernel(s1_ref, s2_ref, e1_ref, jr_ref, e2_ref, r2_ref):
    tn = s1_ref.shape[-1]
    r = lax.broadcasted_iota(I32, (N_CAND, 1), 0)
    mid_i = lax.shift_right_arithmetic(r - 16, 3) + 1
    mid_j = (r - 16) & 7
    flat = jnp.where(r < 16, r, jnp.where(r < 72, mid_i * 16 + mid_j, (r - 64) * 16)).astype(F32)
    limit = jnp.where(mid_i == 1, 8, jnp.where(mid_i == 2, 5, jnp.where(mid_i == 3, 4, jnp.where(mid_i == 4, 3, 2))))
    cell_ok = (r < 16) | (r >= 72) | (mid_j < limit)
    row8 = lax.broadcasted_iota(I32, (8, 1), 0)

    def select_head(h, break_ties):
        s1, s2 = s1_ref[h], s2_ref[h]
        v1, rank1 = _top16(s1, break_ties)
        v2, rank2 = _top16(s2, break_ties)
        groups = [v1[0:1] + v2[0:16]] + [v1[i:i + 1] + v2[0:8] for i in range(1, 8)] + [v1[8:16] + v2[0:1]]
        cand = jnp.where(cell_ok, jnp.concatenate(groups, axis=0), -jnp.inf)
        chosen = jnp.zeros((N_CAND, tn), F32)
        m0 = v1[0:1] + v2[0:1]
        zsum = jnp.zeros((1, tn), F32)
        for it in range(PEER_TOPK):
            m = jnp.max(cand, axis=0, keepdims=True)
            first = jnp.min(jnp.where(cand == m, flat, 4096.0), axis=0, keepdims=True)
            pick = flat == first
            chosen = jnp.where(pick, 1.0, chosen)
            cand = jnp.where(pick, -jnp.inf, cand)
            zsum = zsum + jnp.exp(m - m0)
        j_low = jnp.zeros((8, tn), F32)
        j_low = jnp.where(row8 == 0, jnp.sum(chosen[0:16], axis=0, keepdims=True), j_low)
        for i in range(1, 8):
            j_low = jnp.where(row8 == i, jnp.sum(chosen[8 + 8 * i:16 + 8 * i], axis=0, keepdims=True), j_low)
        j_high = chosen[72:80]
        jr = jnp.zeros((PEER_KEYS, tn), F32)
        for i in range(PEER_TOPK):
            ji = j_low[i:i + 1] if i < 8 else j_high[i - 8:i - 7]
            jr = jnp.where(rank1 == float(i), ji, jr)
        jr_ref[h] = jr
        e1_ref[h] = jnp.where(rank1 < NO_RANK, jnp.exp(s1 - v1[0:1]), 0.0) * (0.5 / zsum)
        e2_ref[h] = jnp.where(rank2 < NO_RANK, jnp.exp(s2 - v2[0:1]), 0.0).astype(BF16)
        r2_ref[h] = rank2.astype(BF16)
        ranked = (rank1 < NO_RANK).astype(F32) + (rank2 < NO_RANK).astype(F32)
        return jnp.max(jnp.sum(ranked, axis=0, keepdims=True))

    def head_body(h, carry):
        n_ranked = select_head(h, False)

        @pl.when(n_ranked > 2.0 * PEER_TOPK)
        def _():
            select_head(h, True)
        return carry
    lax.fori_loop(0, PEER_HEADS, head_body, 0)


def _peer_select(s1t, s2t, tn):
    _, _, n = s1t.shape
    spec = pl.BlockSpec((PEER_HEADS, PEER_KEYS, tn), lambda i: (0, 0, i))
    shape = jax.ShapeDtypeStruct(s1t.shape, F32)
    shape_b = jax.ShapeDtypeStruct(s1t.shape, BF16)
    return pl.pallas_call(
        _peer_select_kernel,
        grid=(n // tn,),
        in_specs=[spec, spec],
        out_specs=[spec] * 4,
        out_shape=[shape, shape, shape_b, shape_b],
        compiler_params=_params(("parallel",)),
        name="peer_select",
    )(s1t, s2t)


def _peer_expert_kernel(h2_ref, x1_ref, g2_ref, gfin_ref, u_ref, vt_ref, e1_ref, jr_ref, e2_ref, r2_ref,
                        y_ref, acc_ref, w_ref, z_ref, p_ref, *, a_per, final_norm):
    j = pl.program_id(2)
    tn = h2_ref.shape[1]
    cols = min(PEER_COLS, tn)
    tiles = [slice(c, c + cols) for c in range(0, tn, cols)]

    @pl.when(j == 0)
    def _():
        acc_ref[...] = jnp.zeros(acc_ref.shape, F32)

    def build_weights(cs):
        for al in range(a_per):
            wsum = jnp.zeros((PEER_KEYS, cols), BF16)
            for h in range(PEER_HEADS):
                jr = jr_ref[h, al:al + 1, cs].astype(BF16)
                e1 = e1_ref[h, al:al + 1, cs].astype(BF16)
                wsum = wsum + jnp.where(r2_ref[h, :, cs] < jr, e2_ref[h, :, cs] * e1, jnp.zeros((), BF16))
            w_ref[al * PEER_KEYS:(al + 1) * PEER_KEYS, cs] = wsum

    def expert_logits(cs):
        z_ref[:, cs] = _dg(u_ref[...], h2_ref[0, cs, :], 1, 1)

    def gate(cs):
        zt = z_ref[:, cs]
        gz = (zt * (1.0 + lax.erf(zt * (2.0 ** -0.5)))).astype(BF16)
        p_ref[:, cs] = w_ref[:, cs] * gz

    def fold(cs):
        acc_ref[:, cs] += _dg(vt_ref[...], p_ref[:, cs], 1, 0)

    expert_logits(tiles[0])
    for k, cs in enumerate(tiles):
        build_weights(cs)
        gate(cs)
        if k + 1 < len(tiles):
            expert_logits(tiles[k + 1])
        fold(cs)

    @pl.when(j == pl.num_programs(2) - 1)
    def _():
        x2 = x1_ref[0] + g2_ref[0] * acc_ref[...].T
        if final_norm:
            x2 = x2 * lax.rsqrt(jnp.mean(x2 * x2, axis=-1, keepdims=True) + EPS) * gfin_ref[...]
        y_ref[0] = x2


def _peer_expert(h2b, x1, gate2, g_final, u_b, vt_b, e1, jr, e2, r2, tn, a_per, final_norm):
    bx, s, d = x1.shape
    n_exp = u_b.shape[0]
    ec = a_per * PEER_KEYS
    tiles = s // tn
    row = lambda w: pl.BlockSpec((1, tn, w), lambda b, i, j: (b, i, 0))
    mod = (pl.BlockSpec((1, 1, d), lambda b, i, j: (b, 0, 0)) if gate2.shape[1] == 1
           else pl.BlockSpec((1, tn, d), lambda b, i, j: (b, i, 0)))
    per_a = pl.BlockSpec((PEER_HEADS, a_per, tn), lambda b, i, j: (0, j, b * tiles + i))
    per_b = pl.BlockSpec((PEER_HEADS, PEER_KEYS, tn), lambda b, i, j: (0, 0, b * tiles + i))
    return pl.pallas_call(
        functools.partial(_peer_expert_kernel, a_per=a_per, final_norm=final_norm),
        grid=(bx, tiles, n_exp // ec),
        in_specs=[row(d), row(d), mod, pl.BlockSpec((1, d), lambda b, i, j: (0, 0)),
                  pl.BlockSpec((ec, d), lambda b, i, j: (j, 0)),
                  pl.BlockSpec((d, ec), lambda b, i, j: (0, j)),
                  per_a, per_a, per_b, per_b],
        out_specs=row(d),
        out_shape=jax.ShapeDtypeStruct((bx, s, d), F32),
        scratch_shapes=[pltpu.VMEM((d, tn), F32), pltpu.VMEM((ec, tn), BF16), pltpu.VMEM((ec, tn), F32),
                        pltpu.VMEM((ec, tn), BF16)],
        compiler_params=_params(("parallel", "parallel", "arbitrary")),
        name="peer_expert",
    )(h2b, x1, gate2, g_final.reshape(1, d), u_b, vt_b, e1, jr, e2, r2)


def _rope_tables(pos):
    rot = HEAD_DIM // 4
    half = rot // 2
    inv_freq = ROPE_THETA ** (-(jnp.arange(half, dtype=F32) * (2.0 / rot)))
    ang = pos.astype(F32)[:, None] * inv_freq[None, :]
    cos, sin = jnp.cos(ang), jnp.sin(ang)
    t = pos.shape[0]
    ones = jnp.ones((t, HEAD_DIM - rot), F32)
    zeros = jnp.zeros((t, HEAD_DIM - rot), F32)
    z8 = jnp.zeros((t, half), F32)
    c = jnp.concatenate([cos, cos, ones], axis=1)
    sa = jnp.concatenate([-sin, z8, zeros], axis=1)
    sb = jnp.concatenate([z8, sin, zeros], axis=1)
    dup = lambda a: jnp.concatenate([a, a], axis=1)
    return dup(c), dup(sa), dup(sb)


def _pack_w_in(w_in):
    d = w_in.shape[0]
    cols = C_IKW - C_GLU
    pad = jnp.zeros((d, LANES - IDX_DIM - IDX_HEADS), w_in.dtype)
    return jnp.concatenate([w_in[:, :cols + IDX_DIM + IDX_HEADS], pad, w_in[:, cols + IDX_DIM + IDX_HEADS:]], axis=1)


def _pick_tile(n, pref):
    t = min(n, pref)
    while n % t:
        t //= 2
    return t


def kernel(x_prompt, x_sample, c_prompt, c_sample, cache_k, cache_v, cache_idx_k, state_conv, page_table, w_ada, b_ada, g_norm_mix, w_in, w_dw, b_dw, ln_conv_g, ln_conv_b, w_conv_out, w_attn_out, w_o, g_norm_ffn, w_peer_q, peer_k1, peer_k2, peer_u, peer_v, g_norm_final):
    b, s, d = x_prompt.shape
    db, t, _ = x_sample.shape
    depth = w_ada.shape[0]
    n_phys = cache_k.shape[1]
    n_pages = page_table.shape[1]
    past = n_pages * PAGE
    ns = db * t
    g_pages = math.gcd(n_pages, PAGES_PER_STEP)

    tabs_p = _rope_tables(jnp.arange(s))
    tabs_s = _rope_tables(jnp.tile(past + jnp.arange(t), db))
    idx_pages = cache_idx_k.reshape(depth * n_phys, PAGE, IDX_DIM)
    k_pages = cache_k.reshape(depth * n_phys, PAGE, ATTN_DIM)
    v_pages = cache_v.reshape(depth * n_phys, PAGE, ATTN_DIM)
    c_all = jnp.concatenate([c_prompt, c_sample], axis=0)
    c_rows = -(-c_all.shape[0] // 8) * 8
    c_all = jnp.pad(c_all, ((0, c_rows - c_all.shape[0]), (0, 0)))

    tm_p, tm_s = _pick_tile(s, 256), _pick_tile(ns, 256)
    tc_p = _pick_tile(s, 256)
    qb = _pick_tile(s, 256)
    kc = _pick_tile(s, 512)
    tn_sel_p, tn_sel_s = _pick_tile(b * s, 512), _pick_tile(ns, 512)
    tn_exp_p, tn_exp_s = _pick_tile(s, 512), _pick_tile(ns, 512)

    xp, xs = x_prompt, x_sample.reshape(1, ns, d)
    outs = [[] for _ in range(8)]
    for l in range(depth):
        mod = _ada(c_all, w_ada[l], b_ada[l])
        mod_p = [m.reshape(b, 1, d) for m in jnp.split(mod[:b], 6, axis=-1)]
        mod_s = [jnp.repeat(m, t, axis=0).reshape(1, ns, d) for m in jnp.split(mod[b:b + db], 6, axis=-1)]
        w_pack = _pack_w_in(w_in[l])
        wh = w_pack.astype(BF16)
        wl = (w_pack[:, C_IQ:C_GA] - wh[:, C_IQ:C_GA].astype(F32)).astype(BF16)
        wco_b, wat_b, wo_b = w_conv_out[l].astype(BF16), w_attn_out[l].astype(BF16), w_o[l].astype(BF16)
        wq_h, wq_l = _split(w_peer_q[l])
        u_b = peer_u[l].astype(BF16)
        vt_b = peer_v[l].astype(BF16).T
        conv_w = (w_dw[l], b_dw[l], ln_conv_g[l], ln_conv_b[l], wco_b)

        u, q, k, v, kb, vb, iq, ikw, ga, gb = _inproj(xp, mod_p[1], mod_p[0], g_norm_mix[l], wh, wl, *tabs_p, tm_p)
        hist0 = jnp.zeros((b, HALO, CONV_DIM), F32)
        ag = _conv_branch(u, hist0, ga, *conv_w, tc_p)
        attn = _dsa_prompt(iq, ikw, q, kb, vb, qb, kc)
        x1, h2b, s1t, s2t = _mix(xp, ag, gb, attn, mod_p[2], mod_p[4], mod_p[3], g_norm_ffn[l],
                                 wat_b, wo_b, wq_h, wq_l, peer_k1[l], peer_k2[l], tm_p)
        sel = _peer_select(s1t, s2t, tn_sel_p)
        last = l == depth - 1
        xp = _peer_expert(h2b, x1, mod_p[5], g_norm_final, u_b, vt_b, *sel, tn_exp_p, A_PER_STEP, last)
        outs[0].append(k.reshape(b, s // PAGE, PAGE, N_HEADS, HEAD_DIM))
        outs[1].append(v.reshape(b, s // PAGE, PAGE, N_HEADS, HEAD_DIM))
        outs[2].append(ikw[:, :, :IDX_DIM].reshape(b, s // PAGE, PAGE, IDX_DIM))
        outs[3].append(jnp.concatenate([hist0[:, HALO - (CONV_WIDTH - 1):], u], axis=1)[:, -(CONV_WIDTH - 1):])

        u, q, k, v, kb, vb, iq, ikw, ga, gb = _inproj(xs, mod_s[1], mod_s[0], g_norm_mix[l], wh, wl, *tabs_s, tm_s)
        u3 = u.reshape(db, t, CONV_DIM)
        hist = jnp.pad(state_conv[l], ((0, 0), (HALO - (CONV_WIDTH - 1), 0), (0, 0)))
        ag = _conv_branch(u3, hist, ga.reshape(db, t, d), *conv_w, t).reshape(1, ns, d)
        iq_ht = iq.reshape(db, t, IDX_HEADS, IDX_DIM).transpose(0, 2, 1, 3).reshape(db, IDX_HEADS * t, IDX_DIM)
        iqh, iql = _split(iq_ht)
        iw = ikw.reshape(db, t, LANES)[:, :, IDX_DIM:IDX_DIM + IDX_HEADS]
        w_rows = jnp.broadcast_to(iw.transpose(0, 2, 1).reshape(db, IDX_HEADS * t, 1), (db, IDX_HEADS * t, LANES))
        pad_rows = lambda a: jnp.pad(a.reshape(db, t, -1), ((0, 0), (0, PAGE - t), (0, 0)))
        bias = _sample_select(page_table, iqh, iql, w_rows, pad_rows(ikw[..., :IDX_DIM]), idx_pages,
                              l * n_phys, g_pages, t)
        q4 = q.reshape(db, t, N_HEADS, HEAD_DIM) * (HEAD_DIM ** -0.5)
        eye = jnp.eye(N_HEADS, dtype=F32)
        qrows = jnp.einsum('bthd,hg->bhtgd', q4, eye).reshape(db, N_HEADS * t, ATTN_DIM).astype(BF16)
        attn = _sample_attend(page_table, qrows, bias, pad_rows(k), pad_rows(v), k_pages, v_pages,
                              l * n_phys, g_pages, t).reshape(1, ns, ATTN_DIM)
        x1, h2b, s1t, s2t = _mix(xs, ag, gb, attn, mod_s[2], mod_s[4], mod_s[3], g_norm_ffn[l],
                                 wat_b, wo_b, wq_h, wq_l, peer_k1[l], peer_k2[l], tm_s)
        sel = _peer_select(s1t, s2t, tn_sel_s)
        xs = _peer_expert(h2b, x1, mod_s[5], g_norm_final, u_b, vt_b, *sel, tn_exp_s, A_PER_STEP, last)
        outs[4].append(k.reshape(db, t, N_HEADS, HEAD_DIM))
        outs[5].append(v.reshape(db, t, N_HEADS, HEAD_DIM))
        outs[6].append(ikw.reshape(db, t, LANES)[:, :, :IDX_DIM])
        outs[7].append(jnp.concatenate([state_conv[l], u3], axis=1)[:, -(CONV_WIDTH - 1):])

    stacked = [jnp.stack(o) for o in outs]
    return (xp, xs.reshape(db, t, d), *stacked)
```

```python
import functools
import math

import jax
import jax.numpy as jnp
from jax import lax
from jax.experimental import pallas as pl
from jax.experimental.pallas import tpu as pltpu

F32 = jnp.float32
BF16 = jnp.bfloat16
I32 = jnp.int32

N_HEADS = 8
HEAD_DIM = 64
ATTN_DIM = N_HEADS * HEAD_DIM
ROPE_THETA = 500000.0
IDX_HEADS = 8
IDX_DIM = 64
TOPK_MAX = 256
CONV_DIM = 512
CONV_WIDTH = 31
HALO = 32
PEER_HEADS = 8
PEER_KEYS = 128
PEER_HALF = 128
PEER_QDIM = 256
PEER_TOPK = 16
PAGE = 128
EPS = 1e-6
LANES = 128
INT_MIN = -(2 ** 31)
INT_MAX = 2 ** 31 - 1
NEG = -1e30
LOG2_E = 1.4426950408889634
NO_RANK = 999.0
COUNT_ROWS = 128
VMEM_LIMIT = 56 * 1024 * 1024

C_GLU, C_Q, C_K, C_V, C_IQ, C_IKW, C_GA, C_GB, C_END = 0, 1024, 1536, 2048, 2560, 3072, 3200, 4224, 5248


def _dg(a, b, ca, cb):
    return lax.dot_general(a, b, (((ca,), (cb,)), ((), ())), preferred_element_type=F32)


def _split(x):
    hi = x.astype(BF16)
    lo = (x - hi.astype(F32)).astype(BF16)
    return hi, lo


def _dot3(ah, al, bh, bl, ca, cb):
    return _dg(ah, bh, ca, cb) + _dg(al, bh, ca, cb) + _dg(ah, bl, ca, cb)


def _sigmoid(x):
    return 1.0 / (1.0 + jnp.exp(-x))


def _const_spec(shape):
    n = len(shape)
    return pl.BlockSpec(shape, lambda *_: (0,) * n, pipeline_mode=pl.Buffered(1))


def _params(sem):
    return pltpu.CompilerParams(dimension_semantics=sem, vmem_limit_bytes=VMEM_LIMIT)


def _sort_key(x):
    bits = lax.bitcast_convert_type(x, I32)
    key = jnp.where(bits < 0, bits ^ jnp.int32(INT_MAX), bits)
    return jnp.where(key == -1, 0, key)


def _ada_kernel(c_ref, w_ref, b_ref, o_ref):
    c = c_ref[...]
    s = c * _sigmoid(c)
    sh, sl = _split(s)
    wh, wl = _split(w_ref[...])
    o_ref[...] = _dot3(sh, sl, wh, wl, 1, 0) + b_ref[...]


def _ada(c, w_ada, b_ada):
    rows, d = c.shape
    n = w_ada.shape[1]
    tn = 1536
    return pl.pallas_call(
        _ada_kernel,
        grid=(n // tn,),
        in_specs=[pl.BlockSpec((rows, d), lambda j: (0, 0)),
                  pl.BlockSpec((d, tn), lambda j: (0, j)),
                  pl.BlockSpec((1, tn), lambda j: (0, j))],
        out_specs=pl.BlockSpec((rows, tn), lambda j: (0, j)),
        out_shape=jax.ShapeDtypeStruct((rows, n), F32),
        compiler_params=_params(("parallel",)),
        name="adaln",
    )(c, w_ada, b_ada.reshape(1, n))


def _tile_lanes(t, width):
    return jnp.concatenate([t] * (width // LANES), axis=1) if width > LANES else t


def _rope(x, c, sa, sb):
    w = x.shape[1]
    return (x * _tile_lanes(c, w) + pltpu.roll(x, w - 8, 1) * _tile_lanes(sa, w)
            + pltpu.roll(x, 8, 1) * _tile_lanes(sb, w))


def _inproj_kernel(x_ref, sc_ref, sh_ref, g_ref, wh_ref, wl_ref, cs_ref, sa_ref, sb_ref,
                   u_ref, q_ref, k_ref, v_ref, kb_ref, vb_ref, iq_ref, ikw_ref, ga_ref, gb_ref):
    x = x_ref[0]
    h = x * lax.rsqrt(jnp.mean(x * x, axis=-1, keepdims=True) + EPS) * g_ref[...]
    h = h * (1.0 + sc_ref[0]) + sh_ref[0]
    hh, hl = _split(h)
    c, sa, sb = cs_ref[...], sa_ref[...], sb_ref[...]

    glu = _dg(hh, wh_ref[:, C_GLU:C_Q], 1, 0)
    u_ref[0] = glu[:, :CONV_DIM] * _sigmoid(glu[:, CONV_DIM:])

    qk = _dg(hh, wh_ref[:, C_Q:C_V], 1, 0)
    q_ref[0] = _rope(qk[:, :ATTN_DIM], c, sa, sb)
    k = _rope(qk[:, ATTN_DIM:], c, sa, sb)
    k_ref[0] = k
    kb_ref[0] = k.astype(BF16)
    v = _dg(hh, wh_ref[:, C_V:C_IQ], 1, 0)
    v_ref[0] = v
    vb_ref[0] = v.astype(BF16)

    wih = wh_ref[:, C_IQ:C_GA]
    idx = _dg(hh, wih, 1, 0) + _dg(hl, wih, 1, 0) + _dg(hh, wl_ref[...], 1, 0)
    iq_ref[0] = _rope(idx[:, :ATTN_DIM], c, sa, sb)
    lane = lax.broadcasted_iota(I32, (1, LANES), 1)
    is_ik = lane < IDX_DIM
    iw_scale = float((IDX_HEADS * IDX_DIM) ** -0.5)
    ikw_ref[0] = _rope(idx[:, ATTN_DIM:], jnp.where(is_ik, c, iw_scale),
                       jnp.where(is_ik, sa, 0.0), jnp.where(is_ik, sb, 0.0))

    ga_ref[0] = _sigmoid(_dg(hh, wh_ref[:, C_GA:C_GB], 1, 0))
    gb_ref[0] = _sigmoid(_dg(hh, wh_ref[:, C_GB:C_END], 1, 0))


def _mod_spec(arr, tm):
    d = arr.shape[-1]
    if arr.shape[1] == 1:
        return pl.BlockSpec((1, 1, d), lambda b, i: (b, 0, 0))
    return pl.BlockSpec((1, tm, d), lambda b, i: (b, i, 0))


def _inproj(x, scale1, shift1, g_mix, wh, wl, cs, sa, sb, tm):
    bx, s, d = x.shape
    row = lambda w: pl.BlockSpec((1, tm, w), lambda b, i: (b, i, 0))
    tab = pl.BlockSpec((tm, LANES), lambda b, i: (i, 0))
    out_w = (CONV_DIM, ATTN_DIM, ATTN_DIM, ATTN_DIM, ATTN_DIM, ATTN_DIM, ATTN_DIM, LANES, d, d)
    out_t = (F32, F32, F32, F32, BF16, BF16, F32, F32, F32, F32)
    return pl.pallas_call(
        _inproj_kernel,
        grid=(bx, s // tm),
        in_specs=[row(d), _mod_spec(scale1, tm), _mod_spec(shift1, tm), _const_spec((1, d)),
                  _const_spec(wh.shape), _const_spec(wl.shape), tab, tab, tab],
        out_specs=[row(w) for w in out_w],
        out_shape=[jax.ShapeDtypeStruct((bx, s, w), t) for w, t in zip(out_w, out_t)],
        compiler_params=_params(("parallel", "parallel")),
        name="inproj",
    )(x, scale1, shift1, g_mix.reshape(1, d), wh, wl, cs, sa, sb)


def _conv_kernel(u_ref, hist_ref, ga_ref, wdw_ref, bdw_ref, lg_ref, lb_ref, wo_ref, o_ref, win_ref, *, t):
    @pl.when(pl.program_id(1) == 0)
    def _():
        win_ref[0:HALO, :] = hist_ref[0]

    win_ref[HALO:HALO + t, :] = u_ref[0]
    off = HALO - (CONV_WIDTH - 1)
    acc = jnp.zeros((t, CONV_DIM), F32)
    for j in range(CONV_WIDTH):
        acc = acc + win_ref[off + j:off + j + t, :] * wdw_ref[j:j + 1, :]
    dw = acc + bdw_ref[...]
    mu = jnp.mean(dw, axis=-1, keepdims=True)
    var = jnp.mean(jnp.square(dw - mu), axis=-1, keepdims=True)
    y = (dw - mu) * lax.rsqrt(var + EPS) * lg_ref[...] + lb_ref[...]
    y = y * _sigmoid(y)
    o_ref[0] = ga_ref[0] * _dg(y.astype(BF16), wo_ref[...], 1, 0)
    tail = win_ref[t:t + HALO, :]
    win_ref[0:HALO, :] = tail


def _conv_branch(u, hist, ga, w_dw, b_dw, ln_g, ln_b, w_out_b, t):
    bx, s, _ = u.shape
    d = ga.shape[-1]
    wdw = jnp.pad(w_dw, ((0, HALO - CONV_WIDTH), (0, 0)))
    vec = lambda a: a.reshape(1, CONV_DIM)
    return pl.pallas_call(
        functools.partial(_conv_kernel, t=t),
        grid=(bx, s // t),
        in_specs=[pl.BlockSpec((1, t, CONV_DIM), lambda b, i: (b, i, 0)),
                  pl.BlockSpec((1, HALO, CONV_DIM), lambda b, i: (b, 0, 0)),
                  pl.BlockSpec((1, t, d), lambda b, i: (b, i, 0)),
                  _const_spec((HALO, CONV_DIM)), _const_spec((1, CONV_DIM)), _const_spec((1, CONV_DIM)),
                  _const_spec((1, CONV_DIM)), _const_spec((CONV_DIM, d))],
        out_specs=pl.BlockSpec((1, t, d), lambda b, i: (b, i, 0)),
        out_shape=jax.ShapeDtypeStruct((bx, s, d), F32),
        scratch_shapes=[pltpu.VMEM((t + HALO, CONV_DIM), F32)],
        compiler_params=_params(("parallel", "arbitrary")),
        name="conv_branch",
    )(u, hist, ga, wdw, vec(b_dw), vec(ln_g), vec(ln_b), w_out_b)


def _select_threshold(count_fn, n_sel, rows, pos_bits, cut_ref):
    def bit_body(it, lo):
        cand = lo + lax.shift_left(jnp.int32(1), 31 - it)
        cnt = count_fn(lambda k, p, c: k >= c, cand)
        return jnp.where(cnt >= n_sel, cand, lo)

    tau = lax.fori_loop(0, 32, bit_body, jnp.full((rows, 1), INT_MIN, I32))
    surplus = count_fn(lambda k, p, c: k >= c, tau) - n_sel
    cut_ref[...] = jnp.full(cut_ref.shape, INT_MAX, I32)

    @pl.when(jnp.max(surplus.astype(F32)) > 0.0)
    def _():
        need = n_sel - count_fn(lambda k, p, c: k > c, tau)

        def pos_body(it, x):
            cand = x + lax.shift_left(jnp.int32(1), pos_bits - 1 - it)
            cnt = count_fn(lambda k, p, c, d: (k == c) & (p < d), tau, cand)
            return jnp.where(cnt < need, cand, x)

        cut = lax.fori_loop(0, pos_bits, pos_body, jnp.zeros((rows, 1), I32))
        cut_ref[...] = jnp.broadcast_to(cut, cut_ref.shape)

    return tau


def _hi_lo_f32(x):
    hi = x.astype(BF16).astype(F32)
    return hi, x - hi


def _dsa_prompt_kernel(iq_ref, ikwq_ref, q_ref, ikw_ref, kb_ref, vb_ref, o_ref,
                       key_ref, ik3_ref, a3_ref, w_ref, q2_ref, cut_ref, s_ref,
                       *, n_sel, qb, kc, s_len):
    i = pl.program_id(1)
    n_ch = (i * qb + qb + kc - 1) // kc
    n_sub = kc // LANES
    lane = lax.broadcasted_iota(I32, (1, LANES), 1)
    low_half = lane < HEAD_DIM

    @pl.when(i == 0)
    def _():
        def body(c, carry):
            blk = ikw_ref[0, pl.ds(pl.multiple_of(c * kc, kc), kc), :]
            hi, lo = _hi_lo_f32(blk)
            first = jnp.where(low_half, hi, pltpu.roll(lo, HEAD_DIM, 1))
            second = jnp.where(low_half, hi, 0.0)
            ik3_ref[c] = jnp.concatenate([first, second], axis=1).astype(BF16)
            return carry
        lax.fori_loop(0, s_len // kc, body, 0)

    iq = iq_ref[0]
    ikwq = ikwq_ref[0]
    q = q_ref[0] * (HEAD_DIM ** -0.5 * LOG2_E)
    for h in range(IDX_HEADS):
        pair = slice((h // 2) * LANES, (h // 2 + 1) * LANES)
        x = iq[:, pair] if h % 2 == 0 else pltpu.roll(iq[:, pair], HEAD_DIM, 1)
        hi, lo = _hi_lo_f32(x)
        first = jnp.where(low_half, hi, pltpu.roll(hi, HEAD_DIM, 1))
        second = jnp.where(low_half, lo, 0.0)
        a3_ref[h] = jnp.concatenate([first, second], axis=1).astype(BF16)
        w_ref[h] = jnp.broadcast_to(ikwq[:, IDX_DIM + h:IDX_DIM + h + 1], (qb, LANES))
        keep = low_half if h % 2 == 0 else jnp.logical_not(low_half)
        q2_ref[h] = jnp.where(keep, q[:, pair], 0.0).astype(BF16)

    qpos = i * qb + lax.broadcasted_iota(I32, (qb, 1), 0)
    sub_lane = lax.broadcasted_iota(I32, (1, kc), 1)

    def score_body(c, carry):
        kk = ik3_ref[c]
        tot = jnp.zeros((qb, kc), F32)
        for h in range(IDX_HEADS):
            s = _dg(a3_ref[h], kk, 1, 1)
            tot = tot + jnp.maximum(s, 0.0) * _tile_lanes(w_ref[h], kc)
        causal = (c * kc + sub_lane) <= qpos
        key_ref[c] = _sort_key(jnp.where(causal, tot, -jnp.inf))
        return carry
    lax.fori_loop(0, n_ch, score_body, 0)

    def count_fn(pred, *thr):
        counts = []
        for r0 in range(0, qb, COUNT_ROWS):
            rs = slice(r0, min(r0 + COUNT_ROWS, qb))
            nrows = rs.stop - rs.start
            thr_b = [jnp.broadcast_to(x[rs], (nrows, LANES)) for x in thr]

            def body(c, acc):
                kblk = key_ref[c, rs, :]
                for j in range(n_sub):
                    pos = c * kc + j * LANES + lane
                    acc = acc + pred(kblk[:, j * LANES:(j + 1) * LANES], pos, *thr_b).astype(I32)
                return acc
            acc = lax.fori_loop(0, n_ch, body, jnp.zeros((nrows, LANES), I32))
            counts.append(jnp.sum(acc, axis=1, keepdims=True))
        return jnp.concatenate(counts, axis=0) if len(counts) > 1 else counts[0]

    tau = _select_threshold(count_fn, n_sel, qb, max(1, (s_len - 1).bit_length()) + 1, cut_ref)

    for r0 in range(0, qb, COUNT_ROWS):
        rs = slice(r0, min(r0 + COUNT_ROWS, qb))
        shape = (rs.stop - rs.start, LANES)
        tau_b = jnp.broadcast_to(tau[rs], shape)
        qpos_b = jnp.broadcast_to(qpos[rs], shape)
        cut_b = cut_ref[rs, :]

        def bias_body(c, carry):
            kblk = key_ref[c, rs, :]
            for j in range(n_sub):
                k = kblk[:, j * LANES:(j + 1) * LANES]
                pos = c * kc + j * LANES + lane
                sel = ((k > tau_b) | ((k == tau_b) & (pos <= cut_b))) & (pos <= qpos_b)
                key_ref[c, rs, j * LANES:(j + 1) * LANES] = lax.bitcast_convert_type(jnp.where(sel, 0.0, NEG), I32)
            return carry
        lax.fori_loop(0, n_ch, bias_body, 0)

    def fold(x, op):
        out = x[:, :LANES]
        for j in range(1, n_sub):
            out = op(out, x[:, j * LANES:(j + 1) * LANES])
        return out

    def attn_body(c, carry):
        ms, ls, accs = carry
        rows = pl.ds(pl.multiple_of(c * kc, kc), kc)
        pair_cols = lambda h: slice((h // 2) * LANES, (h // 2 + 1) * LANES)

        def issue_logits(h):
            s_ref[h % 2] = _dg(q2_ref[h], kb_ref[0, rows, pair_cols(h)], 1, 1)

        issue_logits(0)
        new_m, new_l, new_acc = [], [], []
        for hp in range(N_HEADS // 2):
            vpair = vb_ref[0, rows, pair_cols(2 * hp)]
            pvs, alphas = [], []
            for h in (2 * hp, 2 * hp + 1):
                if h + 1 < N_HEADS:
                    issue_logits(h + 1)
                s = s_ref[h % 2] + lax.bitcast_convert_type(key_ref[c], F32)
                m_new = jnp.maximum(ms[h], jnp.max(fold(s, jnp.maximum), axis=1, keepdims=True))
                p = jnp.exp2(s - _tile_lanes(m_new, kc))
                alpha = jnp.exp2(ms[h] - m_new)
                new_m.append(m_new)
                new_l.append(alpha * ls[h] + fold(p, jnp.add))
                pvs.append(_dg(p.astype(BF16), vpair, 1, 0))
                alphas.append(alpha)
            new_acc.append(jnp.where(low_half, alphas[0], alphas[1]) * accs[hp]
                           + jnp.where(low_half, pvs[0], pvs[1]))
        return tuple(new_m), tuple(new_l), tuple(new_acc)

    zero = jnp.zeros((qb, LANES), F32)
    init = ((jnp.full((qb, LANES), NEG, F32),) * N_HEADS, (zero,) * N_HEADS, (zero,) * (N_HEADS // 2))
    _, ls, accs = lax.fori_loop(0, n_ch, attn_body, init)
    for hp in range(N_HEADS // 2):
        l0 = jnp.sum(ls[2 * hp], axis=1, keepdims=True)
        l1 = jnp.sum(ls[2 * hp + 1], axis=1, keepdims=True)
        o_ref[0, :, hp * LANES:(hp + 1) * LANES] = accs[hp] / jnp.where(low_half, l0, l1)


def _dsa_prompt(iq, ikw, q, kb, vb, qb, kc):
    b, s, _ = q.shape
    n_sel = min(TOPK_MAX, s // 4)
    full = lambda w: pl.BlockSpec((1, s, w), lambda bb, i: (bb, 0, 0), pipeline_mode=pl.Buffered(1))
    blk = lambda w: pl.BlockSpec((1, qb, w), lambda bb, i: (bb, i, 0))
    return pl.pallas_call(
        functools.partial(_dsa_prompt_kernel, n_sel=n_sel, qb=qb, kc=kc, s_len=s),
        grid=(b, s // qb),
        in_specs=[blk(ATTN_DIM), blk(LANES), blk(ATTN_DIM), full(LANES), full(ATTN_DIM), full(ATTN_DIM)],
        out_specs=blk(ATTN_DIM),
        out_shape=jax.ShapeDtypeStruct((b, s, ATTN_DIM), F32),
        scratch_shapes=[pltpu.VMEM((s // kc, qb, kc), I32),
                        pltpu.VMEM((s // kc, kc, 2 * LANES), BF16),
                        pltpu.VMEM((IDX_HEADS, qb, 2 * LANES), BF16),
                        pltpu.VMEM((IDX_HEADS, qb, LANES), F32),
                        pltpu.VMEM((N_HEADS, qb, LANES), BF16),
                        pltpu.VMEM((qb, LANES), I32),
                        pltpu.VMEM((2, qb, kc), F32)],
        compiler_params=_params(("parallel", "arbitrary")),
        name="dsa_prompt",
    )(iq, ikw, q, ikw, kb, vb)


def _sample_select_kernel(pt_ref, qh_ref, ql_ref, w_ref, new_ref, *rest, g, n_grp, n_sel, past, t):
    pages = rest[:g]
    o_ref, key_ref, cur_ref, cut_ref = rest[g:]
    j = pl.program_id(1)
    lane = lax.broadcasted_iota(I32, (1, LANES), 1)
    qh, ql, w = qh_ref[0], ql_ref[0], w_ref[0]

    def scores(ik):
        kh, kl = _split(ik)
        s = _dot3(qh, ql, kh, kl, 1, 1)
        tot = jnp.maximum(s, 0.0) * w
        return jnp.sum(tot.reshape(IDX_HEADS, t, LANES), axis=0)

    @pl.when(j < n_grp)
    def _():
        key_ref[j] = jnp.concatenate([_sort_key(scores(pages[r][0])) for r in range(g)], axis=1)

    @pl.when(j == n_grp)
    def _():
        tpos = lax.broadcasted_iota(I32, (t, 1), 0)
        cur_ref[...] = _sort_key(jnp.where(lane <= tpos, scores(new_ref[0]), -jnp.inf))

        def count_fn(pred, *thr):
            thr_b = [jnp.broadcast_to(x, (t, LANES)) for x in thr]

            def body(c, acc):
                kblk = key_ref[c]
                for r in range(g):
                    pos = (c * g + r) * LANES + lane
                    acc = acc + pred(kblk[:, r * LANES:(r + 1) * LANES], pos, *thr_b).astype(I32)
                return acc
            acc = lax.fori_loop(0, n_grp, body, jnp.zeros((t, LANES), I32))
            acc = acc + pred(cur_ref[...], past + lane, *thr_b).astype(I32)
            return jnp.sum(acc, axis=1, keepdims=True)

        tau = _select_threshold(count_fn, n_sel, t, max(1, (past + LANES - 1).bit_length()) + 1, cut_ref)
        wide_lane = lax.broadcasted_iota(I32, (1, g * LANES), 1)
        cut = cut_ref[...]

        def out_body(c, carry):
            kblk = key_ref[c]
            pos = c * g * LANES + wide_lane
            sel = (kblk > tau) | ((kblk == tau) & (pos <= _tile_lanes(cut, g * LANES)))
            o_ref[0, c] = jnp.where(sel, 0.0, NEG)
            return carry
        lax.fori_loop(0, n_grp, out_body, 0)
        kcur = cur_ref[...]
        pos = past + lane
        sel = ((kcur > tau) | ((kcur == tau) & (pos <= cut))) & (lane <= tpos)
        cur_bias = jnp.where(sel, 0.0, NEG)
        o_ref[0, n_grp] = jnp.concatenate([cur_bias] + [jnp.full((t, LANES), NEG, F32)] * (g - 1), axis=1)


def _page_specs(g, n_grp, layer_base, block_tail):
    nd = len(block_tail)

    def make(r):
        def index_map(b, j, pt):
            return (layer_base + pt[b, jnp.minimum(j, n_grp - 1) * g + r],) + (0,) * nd
        return pl.BlockSpec((1,) + block_tail, index_map)
    return [make(r) for r in range(g)]


def _sample_select(page_table, qh, ql, w, ik_new, idx_pages, layer_base, g, t):
    db, n_pages = page_table.shape
    n_grp = n_pages // g
    past = n_pages * PAGE
    n_sel = min(TOPK_MAX, (past + t) // 4)
    rows = IDX_HEADS * t
    per_b = lambda shape: pl.BlockSpec((1,) + shape, lambda b, j, pt: (b,) + (0,) * len(shape))
    kern = functools.partial(_sample_select_kernel, g=g, n_grp=n_grp, n_sel=n_sel, past=past, t=t)
    return pl.pallas_call(
        kern,
        grid_spec=pltpu.PrefetchScalarGridSpec(
            num_scalar_prefetch=1,
            grid=(db, n_grp + 1),
            in_specs=[per_b((rows, IDX_DIM)), per_b((rows, IDX_DIM)), per_b((rows, LANES)),
                      per_b((PAGE, IDX_DIM))] + _page_specs(g, n_grp, layer_base, (PAGE, IDX_DIM)),
            out_specs=per_b((n_grp + 1, t, g * LANES)),
            scratch_shapes=[pltpu.VMEM((n_grp, t, g * LANES), I32), pltpu.VMEM((t, LANES), I32),
                            pltpu.VMEM((t, LANES), I32)]),
        out_shape=jax.ShapeDtypeStruct((db, n_grp + 1, t, g * LANES), F32),
        compiler_params=_params(("parallel", "arbitrary")),
        name="sample_select",
    )(page_table, qh, ql, w, ik_new, *([idx_pages] * g))


def _sample_attend_kernel(pt_ref, q_ref, bias_ref, knew_ref, vnew_ref, *rest, g, n_grp, t):
    kpages, vpages = rest[:g], rest[g:2 * g]
    o_ref, m_ref, l_ref, acc_ref = rest[2 * g:]
    j = pl.program_id(1)
    rows = N_HEADS * t

    @pl.when(j == 0)
    def _():
        m_ref[...] = jnp.full(m_ref.shape, NEG, F32)
        l_ref[...] = jnp.zeros(l_ref.shape, F32)
        acc_ref[...] = jnp.zeros(acc_ref.shape, F32)

    def step(kmat, vmat, bias):
        n_sub = kmat.shape[0] // LANES
        s = _dg(q_ref[0], kmat.astype(BF16), 1, 1) + jnp.concatenate([bias] * N_HEADS, axis=0)
        m_old = m_ref[...]
        m_new = jnp.maximum(m_old, jnp.max(s, axis=1, keepdims=True))
        p = jnp.exp(s - _tile_lanes(m_new, kmat.shape[0]))
        alpha = jnp.exp(m_old - m_new)
        psum = p[:, :LANES]
        for r in range(1, n_sub):
            psum = psum + p[:, r * LANES:(r + 1) * LANES]
        l_ref[...] = alpha * l_ref[...] + psum
        m_ref[...] = m_new
        acc_ref[...] = (_tile_lanes(alpha, ATTN_DIM) * acc_ref[...]
                        + _dg(p.astype(BF16), vmat.astype(BF16), 1, 0))

    @pl.when(j < n_grp)
    def _():
        step(jnp.concatenate([kp[0] for kp in kpages], axis=0),
             jnp.concatenate([vp[0] for vp in vpages], axis=0), bias_ref[0, 0])

    @pl.when(j == n_grp)
    def _():
        step(knew_ref[0], vnew_ref[0], bias_ref[0, 0][:, :LANES])
        full = acc_ref[...] / jnp.sum(l_ref[...], axis=1, keepdims=True)
        lane = lax.broadcasted_iota(I32, (1, ATTN_DIM), 1)
        out = jnp.zeros((t, ATTN_DIM), F32)
        for h in range(N_HEADS):
            in_head = (lane >= h * HEAD_DIM) & (lane < (h + 1) * HEAD_DIM)
            out = out + jnp.where(in_head, full[h * t:(h + 1) * t, :], 0.0)
        o_ref[0] = out


def _sample_attend(page_table, qrows, bias, k_new, v_new, k_pages, v_pages, layer_base, g, t):
    db, n_pages = page_table.shape
    n_grp = n_pages // g
    rows = N_HEADS * t
    per_b = lambda shape: pl.BlockSpec((1,) + shape, lambda b, j, pt: (b,) + (0,) * len(shape))
    kern = functools.partial(_sample_attend_kernel, g=g, n_grp=n_grp, t=t)
    return pl.pallas_call(
        kern,
        grid_spec=pltpu.PrefetchScalarGridSpec(
            num_scalar_prefetch=1,
            grid=(db, n_grp + 1),
            in_specs=[per_b((rows, ATTN_DIM)),
                      pl.BlockSpec((1, 1, t, g * LANES), lambda b, j, pt: (b, j, 0, 0)),
                      per_b((PAGE, ATTN_DIM)), per_b((PAGE, ATTN_DIM))]
                     + _page_specs(g, n_grp, layer_base, (PAGE, ATTN_DIM))
                     + _page_specs(g, n_grp, layer_base, (PAGE, ATTN_DIM)),
            out_specs=per_b((t, ATTN_DIM)),
            scratch_shapes=[pltpu.VMEM((rows, LANES), F32), pltpu.VMEM((rows, LANES), F32),
                            pltpu.VMEM((rows, ATTN_DIM), F32)]),
        out_shape=jax.ShapeDtypeStruct((db, t, ATTN_DIM), F32),
        compiler_params=_params(("parallel", "arbitrary")),
        name="sample_attend",
    )(page_table, qrows, bias, k_new, v_new, *([k_pages] * g), *([v_pages] * g))


def _mix_kernel(x_ref, ag_ref, gb_ref, at_ref, g1_ref, sc2_ref, sh2_ref, gf_ref,
                wat_ref, wo_ref, wqh_ref, wql_ref, k1h_ref, k1l_ref, k2h_ref, k2l_ref,
                x1_ref, h2_ref, s1_ref, s2_ref):
    bb = _dg(at_ref[0].astype(BF16), wat_ref[...], 1, 0)
    mixed = ag_ref[0] + gb_ref[0] * bb
    x1 = x_ref[0] + g1_ref[0] * _dg(mixed.astype(BF16), wo_ref[...], 1, 0)
    x1_ref[0] = x1
    h2 = x1 * lax.rsqrt(jnp.mean(x1 * x1, axis=-1, keepdims=True) + EPS) * gf_ref[...]
    h2 = h2 * (1.0 + sc2_ref[0]) + sh2_ref[0]
    h2_ref[0] = h2.astype(BF16)
    hh, hl = _split(h2)
    qp = _dot3(hh, hl, wqh_ref[...], wql_ref[...], 1, 0)
    for h in range(PEER_HEADS):
        ah, al = _split(qp[:, h * PEER_QDIM:h * PEER_QDIM + PEER_HALF])
        bh, bl = _split(qp[:, h * PEER_QDIM + PEER_HALF:(h + 1) * PEER_QDIM])
        s1_ref[h] = _dot3(k1h_ref[...], k1l_ref[...], ah, al, 1, 1)
        s2_ref[h] = _dot3(k2h_ref[...], k2l_ref[...], bh, bl, 1, 1)


def _mix(x, ag, gb, attn, gate1, scale2, shift2, g_ffn, wat_b, wo_b, wq_h, wq_l, k1, k2, tm):
    bx, s, d = x.shape
    n = bx * s
    row = lambda w: pl.BlockSpec((1, tm, w), lambda b, i: (b, i, 0))
    tr = pl.BlockSpec((PEER_HEADS, PEER_KEYS, tm), lambda b, i: (0, 0, b * (s // tm) + i))
    k1h, k1l = _split(k1)
    k2h, k2l = _split(k2)
    return pl.pallas_call(
        _mix_kernel,
        grid=(bx, s // tm),
        in_specs=[row(d), row(d), row(d), row(ATTN_DIM), _mod_spec(gate1, tm), _mod_spec(scale2, tm),
                  _mod_spec(shift2, tm), _const_spec((1, d)), _const_spec(wat_b.shape), _const_spec(wo_b.shape),
                  _const_spec(wq_h.shape), _const_spec(wq_l.shape)] + [_const_spec((PEER_KEYS, PEER_HALF))] * 4,
        out_specs=[row(d), row(d), tr, tr],
        out_shape=[jax.ShapeDtypeStruct((bx, s, d), F32), jax.ShapeDtypeStruct((bx, s, d), BF16),
                   jax.ShapeDtypeStruct((PEER_HEADS, PEER_KEYS, n), F32),
                   jax.ShapeDtypeStruct((PEER_HEADS, PEER_KEYS, n), F32)],
        compiler_params=_params(("parallel", "parallel")),
        name="mix_peer_query",
    )(x, ag, gb, attn, gate1, scale2, shift2, g_ffn.reshape(1, d), wat_b, wo_b, wq_h, wq_l, k1h, k1l, k2h, k2l)


A_PER_STEP = 16
PAGES_PER_STEP = 16
PEER_COLS = 256
N_CAND = 80


def _top16(s, break_ties):
    rows = lax.broadcasted_iota(I32, s.shape, 0).astype(F32)
    rank = jnp.full(s.shape, NO_RANK, F32)
    vals = []
    for it in range(PEER_TOPK):
        m = jnp.max(s, axis=0, keepdims=True)
        pick = s == m
        if break_ties:
            pick = rows == jnp.min(jnp.where(pick, rows, float(PEER_KEYS)), axis=0, keepdims=True)
        rank = jnp.where(pick, float(it), rank)
        s = jnp.where(pick, -jnp.inf, s)
        vals.append(m)
    return jnp.concatenate(vals, axis=0), rank


def _peer_select_kernel(s1_ref, s2_ref, e1_ref, jr_ref, e2_ref, r2_ref):
    tn = s1_ref.shape[-1]
    r = lax.broadcasted_iota(I32, (N_CAND, 1), 0)
    mid_i = lax.shift_right_arithmetic(r - 16, 3) + 1
    mid_j = (r - 16) & 7
    flat = jnp.where(r < 16, r, jnp.where(r < 72, mid_i * 16 + mid_j, (r - 64) * 16)).astype(F32)
    limit = jnp.where(mid_i == 1, 8, jnp.where(mid_i == 2, 5, jnp.where(mid_i == 3, 4, jnp.where(mid_i == 4, 3, 2))))
    cell_ok = (r < 16) | (r >= 72) | (mid_j < limit)
    row8 = lax.broadcasted_iota(I32, (8, 1), 0)

    def select_head(h, break_ties):
        s1, s2 = s1_ref[h], s2_ref[h]
        v1, rank1 = _top16(s1, break_ties)
        v2, rank2 = _top16(s2, break_ties)
        groups = [v1[0:1] + v2[0:16]] + [v1[i:i + 1] + v2[0:8] for i in range(1, 8)] + [v1[8:16] + v2[0:1]]
        cand = jnp.where(cell_ok, jnp.concatenate(groups, axis=0), -jnp.inf)
        chosen = jnp.zeros((N_CAND, tn), F32)
        m0 = v1[0:1] + v2[0:1]
        zsum = jnp.zeros((1, tn), F32)
        for it in range(PEER_TOPK):
            m = jnp.max(cand, axis=0, keepdims=True)
            first = jnp.min(jnp.where(cand == m, flat, 4096.0), axis=0, keepdims=True)
            pick = flat == first
            chosen = jnp.where(pick, 1.0, chosen)
            cand = jnp.where(pick, -jnp.inf, cand)
            zsum = zsum + jnp.exp(m - m0)
        j_low = jnp.zeros((8, tn), F32)
        j_low = jnp.where(row8 == 0, jnp.sum(chosen[0:16], axis=0, keepdims=True), j_low)
        for i in range(1, 8):
            j_low = jnp.where(row8 == i, jnp.sum(chosen[8 + 8 * i:16 + 8 * i], axis=0, keepdims=True), j_low)
        j_high = chosen[72:80]
        jr = jnp.zeros((PEER_KEYS, tn), F32)
        for i in range(PEER_TOPK):
            ji = j_low[i:i + 1] if i < 8 else j_high[i - 8:i - 7]
            jr = jnp.where(rank1 == float(i), ji, jr)
        jr_ref[h] = jr
        e1_ref[h] = jnp.where(rank1 < NO_RANK, jnp.exp(s1 - v1[0:1]), 0.0) * (0.5 / zsum)
        e2_ref[h] = jnp.where(rank2 < NO_RANK, jnp.exp(s2 - v2[0:1]), 0.0).astype(BF16)
        r2_ref[h] = rank2.astype(BF16)
        ranked = (rank1 < NO_RANK).astype(F32) + (rank2 < NO_RANK).astype(F32)
        return jnp.max(jnp.sum(ranked, axis=0, keepdims=True))

    def head_body(h, carry):
        n_ranked = select_head(h, False)

        @pl.when(n_ranked > 2.0 * PEER_TOPK)
        def _():
            select_head(h, True)
        return carry
    lax.fori_loop(0, PEER_HEADS, head_body, 0)


def _peer_select(s1t, s2t, tn):
    _, _, n = s1t.shape
    spec = pl.BlockSpec((PEER_HEADS, PEER_KEYS, tn), lambda i: (0, 0, i))
    shape = jax.ShapeDtypeStruct(s1t.shape, F32)
    shape_b = jax.ShapeDtypeStruct(s1t.shape, BF16)
    return pl.pallas_call(
        _peer_select_kernel,
        grid=(n // tn,),
        in_specs=[spec, spec],
        out_specs=[spec] * 4,
        out_shape=[shape, shape, shape_b, shape_b],
        compiler_params=_params(("parallel",)),
        name="peer_select",
    )(s1t, s2t)


def _peer_expert_kernel(h2_ref, x1_ref, g2_ref, gfin_ref, u_ref, vt_ref, e1_ref, jr_ref, e2_ref, r2_ref,
                        y_ref, acc_ref, w_ref, z_ref, p_ref, *, a_per, final_norm):
    j = pl.program_id(2)
    tn = h2_ref.shape[1]
    cols = min(PEER_COLS, tn)
    tiles = [slice(c, c + cols) for c in range(0, tn, cols)]

    @pl.when(j == 0)
    def _():
        acc_ref[...] = jnp.zeros(acc_ref.shape, F32)

    def build_weights(cs):
        for al in range(a_per):
            wsum = jnp.zeros((PEER_KEYS, cols), BF16)
            for h in range(PEER_HEADS):
                jr = jr_ref[h, al:al + 1, cs].astype(BF16)
                e1 = e1_ref[h, al:al + 1, cs].astype(BF16)
                wsum = wsum + jnp.where(r2_ref[h, :, cs] < jr, e2_ref[h, :, cs] * e1, jnp.zeros((), BF16))
            w_ref[al * PEER_KEYS:(al + 1) * PEER_KEYS, cs] = wsum

    def expert_logits(cs):
        z_ref[:, cs] = _dg(u_ref[...], h2_ref[0, cs, :], 1, 1)

    def gate(cs):
        zt = z_ref[:, cs]
        gz = (zt * (1.0 + lax.erf(zt * (2.0 ** -0.5)))).astype(BF16)
        p_ref[:, cs] = w_ref[:, cs] * gz

    def fold(cs):
        acc_ref[:, cs] += _dg(vt_ref[...], p_ref[:, cs], 1, 0)

    expert_logits(tiles[0])
    for k, cs in enumerate(tiles):
        build_weights(cs)
        gate(cs)
        if k + 1 < len(tiles):
            expert_logits(tiles[k + 1])
        fold(cs)

    @pl.when(j == pl.num_programs(2) - 1)
    def _():
        x2 = x1_ref[0] + g2_ref[0] * acc_ref[...].T
        if final_norm:
            x2 = x2 * lax.rsqrt(jnp.mean(x2 * x2, axis=-1, keepdims=True) + EPS) * gfin_ref[...]
        y_ref[0] = x2


def _peer_expert(h2b, x1, gate2, g_final, u_b, vt_b, e1, jr, e2, r2, tn, a_per, final_norm):
    bx, s, d = x1.shape
    n_exp = u_b.shape[0]
    ec = a_per * PEER_KEYS
    tiles = s // tn
    row = lambda w: pl.BlockSpec((1, tn, w), lambda b, i, j: (b, i, 0))
    mod = (pl.BlockSpec((1, 1, d), lambda b, i, j: (b, 0, 0)) if gate2.shape[1] == 1
           else pl.BlockSpec((1, tn, d), lambda b, i, j: (b, i, 0)))
    per_a = pl.BlockSpec((PEER_HEADS, a_per, tn), lambda b, i, j: (0, j, b * tiles + i))
    per_b = pl.BlockSpec((PEER_HEADS, PEER_KEYS, tn), lambda b, i, j: (0, 0, b * tiles + i))
    return pl.pallas_call(
        functools.partial(_peer_expert_kernel, a_per=a_per, final_norm=final_norm),
        grid=(bx, tiles, n_exp // ec),
        in_specs=[row(d), row(d), mod, pl.BlockSpec((1, d), lambda b, i, j: (0, 0)),
                  pl.BlockSpec((ec, d), lambda b, i, j: (j, 0)),
                  pl.BlockSpec((d, ec), lambda b, i, j: (0, j)),
                  per_a, per_a, per_b, per_b],
        out_specs=row(d),
        out_shape=jax.ShapeDtypeStruct((bx, s, d), F32),
        scratch_shapes=[pltpu.VMEM((d, tn), F32), pltpu.VMEM((ec, tn), BF16), pltpu.VMEM((ec, tn), F32),
                        pltpu.VMEM((ec, tn), BF16)],
        compiler_params=_params(("parallel", "parallel", "arbitrary")),
        name="peer_expert",
    )(h2b, x1, gate2, g_final.reshape(1, d), u_b, vt_b, e1, jr, e2, r2)


def _rope_tables(pos):
    rot = HEAD_DIM // 4
    half = rot // 2
    inv_freq = ROPE_THETA ** (-(jnp.arange(half, dtype=F32) * (2.0 / rot)))
    ang = pos.astype(F32)[:, None] * inv_freq[None, :]
    cos, sin = jnp.cos(ang), jnp.sin(ang)
    t = pos.shape[0]
    ones = jnp.ones((t, HEAD_DIM - rot), F32)
    zeros = jnp.zeros((t, HEAD_DIM - rot), F32)
    z8 = jnp.zeros((t, half), F32)
    c = jnp.concatenate([cos, cos, ones], axis=1)
    sa = jnp.concatenate([-sin, z8, zeros], axis=1)
    sb = jnp.concatenate([z8, sin, zeros], axis=1)
    dup = lambda a: jnp.concatenate([a, a], axis=1)
    return dup(c), dup(sa), dup(sb)


def _pack_w_in(w_in):
    d = w_in.shape[0]
    cols = C_IKW - C_GLU
    pad = jnp.zeros((d, LANES - IDX_DIM - IDX_HEADS), w_in.dtype)
    return jnp.concatenate([w_in[:, :cols + IDX_DIM + IDX_HEADS], pad, w_in[:, cols + IDX_DIM + IDX_HEADS:]], axis=1)


def _pick_tile(n, pref):
    t = min(n, pref)
    while n % t:
        t //= 2
    return t


def kernel(x_prompt, x_sample, c_prompt, c_sample, cache_k, cache_v, cache_idx_k, state_conv, page_table, w_ada, b_ada, g_norm_mix, w_in, w_dw, b_dw, ln_conv_g, ln_conv_b, w_conv_out, w_attn_out, w_o, g_norm_ffn, w_peer_q, peer_k1, peer_k2, peer_u, peer_v, g_norm_final):
    b, s, d = x_prompt.shape
    db, t, _ = x_sample.shape
    depth = w_ada.shape[0]
    n_phys = cache_k.shape[1]
    n_pages = page_table.shape[1]
    past = n_pages * PAGE
    ns = db * t
    g_pages = math.gcd(n_pages, PAGES_PER_STEP)

    tabs_p = _rope_tables(jnp.arange(s))
    tabs_s = _rope_tables(jnp.tile(past + jnp.arange(t), db))
    idx_pages = cache_idx_k.reshape(depth * n_phys, PAGE, IDX_DIM)
    k_pages = cache_k.reshape(depth * n_phys, PAGE, ATTN_DIM)
    v_pages = cache_v.reshape(depth * n_phys, PAGE, ATTN_DIM)
    c_all = jnp.concatenate([c_prompt, c_sample], axis=0)
    c_rows = -(-c_all.shape[0] // 8) * 8
    c_all = jnp.pad(c_all, ((0, c_rows - c_all.shape[0]), (0, 0)))

    tm_p, tm_s = _pick_tile(s, 256), _pick_tile(ns, 256)
    tc_p = _pick_tile(s, 256)
    qb = _pick_tile(s, 256)
    kc = _pick_tile(s, 1024)
    tn_sel_p, tn_sel_s = _pick_tile(b * s, 512), _pick_tile(ns, 512)
    tn_exp_p, tn_exp_s = _pick_tile(s, 512), _pick_tile(ns, 512)

    xp, xs = x_prompt, x_sample.reshape(1, ns, d)
    outs = [[] for _ in range(8)]
    for l in range(depth):
        mod = _ada(c_all, w_ada[l], b_ada[l])
        mod_p = [m.reshape(b, 1, d) for m in jnp.split(mod[:b], 6, axis=-1)]
        mod_s = [jnp.repeat(m, t, axis=0).reshape(1, ns, d) for m in jnp.split(mod[b:b + db], 6, axis=-1)]
        w_pack = _pack_w_in(w_in[l])
        wh = w_pack.astype(BF16)
        wl = (w_pack[:, C_IQ:C_GA] - wh[:, C_IQ:C_GA].astype(F32)).astype(BF16)
        wco_b, wat_b, wo_b = w_conv_out[l].astype(BF16), w_attn_out[l].astype(BF16), w_o[l].astype(BF16)
        wq_h, wq_l = _split(w_peer_q[l])
        u_b = peer_u[l].astype(BF16)
        vt_b = peer_v[l].astype(BF16).T
        conv_w = (w_dw[l], b_dw[l], ln_conv_g[l], ln_conv_b[l], wco_b)

        u, q, k, v, kb, vb, iq, ikw, ga, gb = _inproj(xp, mod_p[1], mod_p[0], g_norm_mix[l], wh, wl, *tabs_p, tm_p)
        hist0 = jnp.zeros((b, HALO, CONV_DIM), F32)
        ag = _conv_branch(u, hist0, ga, *conv_w, tc_p)
        attn = _dsa_prompt(iq, ikw, q, kb, vb, qb, kc)
        x1, h2b, s1t, s2t = _mix(xp, ag, gb, attn, mod_p[2], mod_p[4], mod_p[3], g_norm_ffn[l],
                                 wat_b, wo_b, wq_h, wq_l, peer_k1[l], peer_k2[l], tm_p)
        sel = _peer_select(s1t, s2t, tn_sel_p)
        last = l == depth - 1
        xp = _peer_expert(h2b, x1, mod_p[5], g_norm_final, u_b, vt_b, *sel, tn_exp_p, A_PER_STEP, last)
        outs[0].append(k.reshape(b, s // PAGE, PAGE, N_HEADS, HEAD_DIM))
        outs[1].append(v.reshape(b, s // PAGE, PAGE, N_HEADS, HEAD_DIM))
        outs[2].append(ikw[:, :, :IDX_DIM].reshape(b, s // PAGE, PAGE, IDX_DIM))
        outs[3].append(jnp.concatenate([hist0[:, HALO - (CONV_WIDTH - 1):], u], axis=1)[:, -(CONV_WIDTH - 1):])

        u, q, k, v, kb, vb, iq, ikw, ga, gb = _inproj(xs, mod_s[1], mod_s[0], g_norm_mix[l], wh, wl, *tabs_s, tm_s)
        u3 = u.reshape(db, t, CONV_DIM)
        hist = jnp.pad(state_conv[l], ((0, 0), (HALO - (CONV_WIDTH - 1), 0), (0, 0)))
        ag = _conv_branch(u3, hist, ga.reshape(db, t, d), *conv_w, t).reshape(1, ns, d)
        iq_ht = iq.reshape(db, t, IDX_HEADS, IDX_DIM).transpose(0, 2, 1, 3).reshape(db, IDX_HEADS * t, IDX_DIM)
        iqh, iql = _split(iq_ht)
        iw = ikw.reshape(db, t, LANES)[:, :, IDX_DIM:IDX_DIM + IDX_HEADS]
        w_rows = jnp.broadcast_to(iw.transpose(0, 2, 1).reshape(db, IDX_HEADS * t, 1), (db, IDX_HEADS * t, LANES))
        pad_rows = lambda a: jnp.pad(a.reshape(db, t, -1), ((0, 0), (0, PAGE - t), (0, 0)))
        bias = _sample_select(page_table, iqh, iql, w_rows, pad_rows(ikw[..., :IDX_DIM]), idx_pages,
                              l * n_phys, g_pages, t)
        q4 = q.reshape(db, t, N_HEADS, HEAD_DIM) * (HEAD_DIM ** -0.5)
        eye = jnp.eye(N_HEADS, dtype=F32)
        qrows = jnp.einsum('bthd,hg->bhtgd', q4, eye).reshape(db, N_HEADS * t, ATTN_DIM).astype(BF16)
        attn = _sample_attend(page_table, qrows, bias, pad_rows(k), pad_rows(v), k_pages, v_pages,
                              l * n_phys, g_pages, t).reshape(1, ns, ATTN_DIM)
        x1, h2b, s1t, s2t = _mix(xs, ag, gb, attn, mod_s[2], mod_s[4], mod_s[3], g_norm_ffn[l],
                                 wat_b, wo_b, wq_h, wq_l, peer_k1[l], peer_k2[l], tm_s)
        sel = _peer_select(s1t, s2t, tn_sel_s)
        xs = _peer_expert(h2b, x1, mod_s[5], g_norm_final, u_b, vt_b, *sel, tn_exp_s, A_PER_STEP, last)
        outs[4].append(k.reshape(db, t, N_HEADS, HEAD_DIM))
        outs[5].append(v.reshape(db, t, N_HEADS, HEAD_DIM))
        outs[6].append(ikw.reshape(db, t, LANES)[:, :, :IDX_DIM])
        outs[7].append(jnp.concatenate([state_conv[l], u3], axis=1)[:, -(CONV_WIDTH - 1):])

    stacked = [jnp.stack(o) for o in outs]
    return (xp, xs.reshape(db, t, d), *stacked)
```
